```python
import math
import jax, jax.numpy as jnp
from jax import lax
import numpy as np

D_MODEL = 1024
BATCH = 2
SEQ = 8192
DEPTH = 2

GRID_W = 64
CTX_LEN = 256
EPS = 1e-6
NEG_INF = -1e30

CHUNK = 128
SGU_GROUPS = 4
SGU_CH = 128
SGU_WIDTH = SGU_GROUPS * SGU_CH
HEAD_DIM = 64
N_Q_HEADS = 8
N_KV_HEADS = 2
Q_PER_KV = N_Q_HEADS // N_KV_HEADS
ATTN_WIDTH = N_Q_HEADS * HEAD_DIM
KV_WIDTH = N_KV_HEADS * HEAD_DIM
WINDOW = 128
ATTN_BLOCK = 128
ATTN_SCALE = HEAD_DIM ** -0.5
ROPE_BASE = 10000.0
ROPE_FREQS = HEAD_DIM // 4
EVEN_SPLITS = (SGU_WIDTH, 2 * SGU_WIDTH, 2 * SGU_WIDTH + ATTN_WIDTH, 2 * SGU_WIDTH + ATTN_WIDTH + KV_WIDTH)
IN_EVEN = 2 * SGU_WIDTH + ATTN_WIDTH + 2 * KV_WIDTH
MIX_EVEN = SGU_WIDTH + ATTN_WIDTH
D_RNN = 1280
LRU_BLOCKS = 10
LRU_BLOCK = D_RNN // LRU_BLOCKS
CONV_W = 4
CONV_LEFT = 2
LRU_C = 8.0
D_FF = 2816
N_EXPERTS = 8
TOP_K = 2

N_EVEN = (DEPTH + 1) // 2
N_ODD = DEPTH // 2

kernel_name = "hybrid_sgu_swa_rglru_moe_dit_block"


def rmsnorm(x, g):
    xf = x.astype(jnp.float32)
    y = xf * lax.rsqrt(jnp.mean(xf * xf, axis=-1, keepdims=True) + EPS)
    return (y * g.astype(jnp.float32)).astype(x.dtype)


def adaln_params(act, w, b):
    return jnp.split(act @ w + b, 6, axis=-1)


def axial_rope_tables(rows):
    row = jnp.repeat(jnp.arange(rows, dtype=jnp.float32), GRID_W)
    col = jnp.tile(jnp.arange(GRID_W, dtype=jnp.float32), rows)
    freqs = ROPE_BASE ** (-jnp.arange(ROPE_FREQS, dtype=jnp.float32) / ROPE_FREQS)
    ang = jnp.stack([row[:, None] * freqs, col[:, None] * freqs], axis=1)
    return jnp.cos(ang), jnp.sin(ang)


def apply_axial_rope(t, cos, sin):
    B, S, H, _ = t.shape
    tf = t.astype(jnp.float32).reshape(B, S, H, 2, 2, ROPE_FREQS)
    t1, t2 = tf[..., 0, :], tf[..., 1, :]
    cs, sn = cos[None, :, None], sin[None, :, None]
    out = jnp.stack([t1 * cs - t2 * sn, t2 * cs + t1 * sn], axis=-2)
    return out.reshape(B, S, H, HEAD_DIM).astype(t.dtype)


def chunk_spatial_gating(u, v, w_s, b_s):
    B, N, _ = v.shape
    vg = v.astype(jnp.float32).reshape(B, N // CHUNK, CHUNK, SGU_GROUPS, SGU_CH)
    mu = jnp.mean(vg, axis=-1, keepdims=True)
    var = jnp.mean(jnp.square(vg - mu), axis=-1, keepdims=True)
    vg = (vg - mu) * lax.rsqrt(var + EPS)
    mixed = jnp.einsum('gpq,bnqgc->bnpgc', w_s.astype(jnp.float32), vg) \
        + b_s.T.astype(jnp.float32)[None, None, :, :, None]
    return (u.astype(jnp.float32) * mixed.reshape(B, N, SGU_WIDTH)).astype(u.dtype)


def window_attention(q, k, v, k_ctx, v_ctx, sink):
    B, S = q.shape[:2]
    nb = S // ATTN_BLOCK
    qb = q.reshape(B, nb, ATTN_BLOCK, N_KV_HEADS, Q_PER_KV, HEAD_DIM)
    pad = ((0, 0), (ATTN_BLOCK, ATTN_BLOCK), (0, 0), (0, 0))

    def band(t):
        tb = jnp.pad(t, pad).reshape(B, nb + 2, ATTN_BLOCK, N_KV_HEADS, HEAD_DIM)
        return jnp.concatenate([tb[:, :-2], tb[:, 1:-1], tb[:, 2:]], axis=2)

    kw, vw = band(k), band(v)
    s_loc = jnp.einsum('bnqkgd,bnskd->bnkgqs', qb, kw,
                       preferred_element_type=jnp.float32) * ATTN_SCALE
    q_off = jnp.arange(ATTN_BLOCK)[:, None]
    k_off = jnp.arange(3 * ATTN_BLOCK)[None, :] - ATTN_BLOCK
    k_pos = jnp.arange(nb)[:, None, None] * ATTN_BLOCK + k_off[None]
    valid = (jnp.abs(k_off - q_off) <= WINDOW)[None] & (k_pos >= 0) & (k_pos < S)
    s_loc = jnp.where(valid[None, :, None, None], s_loc, NEG_INF)
    s_ctx = jnp.einsum('bnqkgd,blkd->bnkgql', qb, k_ctx,
                       preferred_element_type=jnp.float32) * ATTN_SCALE
    s_sink = sink.astype(jnp.float32).reshape(1, 1, N_KV_HEADS, Q_PER_KV, 1, 1)
    m = jnp.maximum(jnp.maximum(jnp.max(s_loc, axis=-1, keepdims=True),
                                jnp.max(s_ctx, axis=-1, keepdims=True)), s_sink)
    p_loc = jnp.exp(s_loc - m)
    p_ctx = jnp.exp(s_ctx - m)
    inv = 1.0 / (jnp.sum(p_loc, axis=-1, keepdims=True) + jnp.sum(p_ctx, axis=-1, keepdims=True)
                 + jnp.exp(s_sink - m))
    o = jnp.einsum('bnkgqs,bnskd->bnqkgd', (p_loc * inv).astype(v.dtype), vw) \
        + jnp.einsum('bnkgql,blkd->bnqkgd', (p_ctx * inv).astype(v.dtype), v_ctx)
    return o.reshape(B, S, ATTN_WIDTH)


def context_attention(q, k, v, sink):
    B, L = q.shape[:2]
    qg = q.reshape(B, L, N_KV_HEADS, Q_PER_KV, HEAD_DIM)
    s = jnp.einsum('bqkgd,blkd->bkgql', qg, k, preferred_element_type=jnp.float32) * ATTN_SCALE
    sk = jnp.broadcast_to(sink.astype(jnp.float32).reshape(1, N_KV_HEADS, Q_PER_KV, 1, 1), s.shape[:-1] + (1,))
    p = jax.nn.softmax(jnp.concatenate([s, sk], axis=-1), axis=-1)[..., :-1]
    o = jnp.einsum('bkgql,blkd->bqkgd', p.astype(v.dtype), v)
    return o.reshape(B, L, ATTN_WIDTH)


def centred_depthwise_conv(t, w, b):
    out = lax.conv_general_dilated(t, w[:, None, :], window_strides=(1,),
                                   padding=[(CONV_LEFT, CONV_W - 1 - CONV_LEFT)],
                                   dimension_numbers=('NWC', 'WIO', 'NWC'),
                                   feature_group_count=t.shape[-1])
    return out + b


def block_diag_linear(t, w, b):
    lead = t.shape[:-1]
    tb = t.reshape(*lead, LRU_BLOCKS, LRU_BLOCK)
    return jnp.einsum('...hi,hij->...hj', tb, w).reshape(*lead, D_RNN) + b


def rglru_coefficients(t, w_a, b_a, w_x, b_x, lam):
    tf = t.astype(jnp.float32)
    r = jax.nn.sigmoid(block_diag_linear(tf, w_a.astype(jnp.float32), b_a.astype(jnp.float32)))
    i = jax.nn.sigmoid(block_diag_linear(tf, w_x.astype(jnp.float32), b_x.astype(jnp.float32)))
    log_a = -LRU_C * r * jax.nn.softplus(-lam.astype(jnp.float32))
    return jnp.exp(log_a), jnp.sqrt(-jnp.expm1(2.0 * log_a)) * (i * tf)


def _lin_combine(e1, e2):
    a1, b1 = e1
    a2, b2 = e2
    return a1 * a2, a2 * b1 + b2


def linear_recurrence(a, b, h0, reverse):
    edge = -1 if reverse else 0
    b = b.at[:, edge].add(a[:, edge] * h0)
    _, h = lax.associative_scan(_lin_combine, (a, b), reverse=reverse, axis=1)
    return h


def swiglu(t, w1, w3, w2):
    return (jax.nn.silu(t @ w1) * (t @ w3)) @ w2


def moe_swiglu(h, router_w, w1, w3, w2):
    B, N, D = h.shape
    t = h.reshape(B * N, D)
    logits = (t @ router_w).astype(jnp.float32)
    top_v, top_i = lax.top_k(logits, TOP_K)
    gk = jax.nn.softmax(top_v, axis=-1)
    gates = jnp.sum(jax.nn.one_hot(top_i, N_EXPERTS, dtype=jnp.float32) * gk[..., None], axis=1)
    out = jnp.zeros_like(t)
    for e in range(N_EXPERTS):
        out = out + gates[:, e:e + 1].astype(t.dtype) * swiglu(t, w1[e], w3[e], w2[e])
    return out.reshape(B, N, D)


def even_layer(x, xc, c_act, cc_act, cos, sin, ada_w, ada_b, n1, n2, w_in, sgu_w, sgu_b, sink,
               w_out, w1, w3, w2, need_ctx_out):
    B, S, _ = x.shape
    L = xc.shape[1]
    sh1, sc1, g1, sh2, sc2, g2 = [m[:, None, :] for m in adaln_params(c_act, ada_w, ada_b)]
    csh1, csc1, cg1, csh2, csc2, cg2 = adaln_params(cc_act, ada_w, ada_b)
    h = rmsnorm(x, n1) * (1.0 + sc1) + sh1
    hc = rmsnorm(xc, n1) * (1.0 + csc1) + csh1
    u, v, q, k, val = jnp.split(h @ w_in, EVEN_SPLITS, axis=-1)
    uc, vc, qc, kc, valc = jnp.split(hc @ w_in, EVEN_SPLITS, axis=-1)
    kc = kc.reshape(B, L, N_KV_HEADS, HEAD_DIM)
    valc = valc.reshape(B, L, N_KV_HEADS, HEAD_DIM)
    a_out = chunk_spatial_gating(jax.nn.gelu(u), jax.nn.gelu(v), sgu_w, sgu_b)
    q = apply_axial_rope(q.reshape(B, S, N_Q_HEADS, HEAD_DIM), cos, sin)
    k = apply_axial_rope(k.reshape(B, S, N_KV_HEADS, HEAD_DIM), cos, sin)
    b_out = window_attention(q, k, val.reshape(B, S, N_KV_HEADS, HEAD_DIM), kc, valc, sink)
    x = x + g1 * (jnp.concatenate([a_out, b_out], axis=-1) @ w_out)
    x = x + g2 * swiglu(rmsnorm(x, n2) * (1.0 + sc2) + sh2, w1, w3, w2)
    if need_ctx_out:
        ac = chunk_spatial_gating(jax.nn.gelu(uc), jax.nn.gelu(vc), sgu_w, sgu_b)
        bc = context_attention(qc.reshape(B, L, N_Q_HEADS, HEAD_DIM), kc, valc, sink)
        xc = xc + cg1 * (jnp.concatenate([ac, bc], axis=-1) @ w_out)
        xc = xc + cg2 * swiglu(rmsnorm(xc, n2) * (1.0 + csc2) + csh2, w1, w3, w2)
    return x, xc


def odd_layer(x, xc, c_act, cc_act, ada_w, ada_b, n1, n2, w_in, conv_w, conv_b, lru_wa, lru_ba,
              lru_wx, lru_bx, lru_lambda, w_out, router_w, w1, w3, w2, need_ctx_out):
    B = x.shape[0]
    sh1, sc1, g1, sh2, sc2, g2 = [m[:, None, :] for m in adaln_params(c_act, ada_w, ada_b)]
    csh1, csc1, cg1, csh2, csc2, cg2 = adaln_params(cc_act, ada_w, ada_b)
    h = rmsnorm(x, n1) * (1.0 + sc1) + sh1
    hc = rmsnorm(xc, n1) * (1.0 + csc1) + csh1
    gate_l, rec_l = jnp.split(h @ w_in, 2, axis=-1)
    gate_c, rec_c = jnp.split(hc @ w_in, 2, axis=-1)
    rec_l = centred_depthwise_conv(rec_l, conv_w, conv_b)
    rec_c = centred_depthwise_conv(rec_c, conv_w, conv_b)
    lat_dirs, ctx_dirs = [], []
    for d in range(2):
        rev = d == 1
        a_c, b_c = rglru_coefficients(rec_c, lru_wa[d], lru_ba[d], lru_wx[d], lru_bx[d], lru_lambda[d])
        h_ctx = linear_recurrence(a_c, b_c, jnp.zeros((B, D_RNN), jnp.float32), rev)
        h0 = h_ctx[:, 0] if rev else h_ctx[:, -1]
        a_l, b_l = rglru_coefficients(rec_l, lru_wa[d], lru_ba[d], lru_wx[d], lru_bx[d], lru_lambda[d])
        lat_dirs.append(linear_recurrence(a_l, b_l, h0, rev))
        ctx_dirs.append(h_ctx)
    y = jax.nn.gelu(gate_l) * (lat_dirs[0] + lat_dirs[1]).astype(x.dtype)
    x = x + g1 * (y @ w_out)
    x = x + g2 * moe_swiglu(rmsnorm(x, n2) * (1.0 + sc2) + sh2, router_w, w1, w3, w2)
    if need_ctx_out:
        yc = jax.nn.gelu(gate_c) * (ctx_dirs[0] + ctx_dirs[1]).astype(xc.dtype)
        xc = xc + cg1 * (yc @ w_out)
        xc = xc + cg2 * moe_swiglu(rmsnorm(xc, n2) * (1.0 + csc2) + csh2, router_w, w1, w3, w2)
    return x, xc


def setup_inputs(seed: int = 0) -> dict:
    key = jax.random.key(seed)
    keys = iter(jax.random.split(key, 48))
    D = D_MODEL

    def nrm(shape, scale):
        return scale * jax.random.normal(next(keys), shape, jnp.float32)

    u_lam = jax.random.uniform(next(keys), (N_ODD, 2, D_RNN), jnp.float32, minval=0.9, maxval=0.999)
    a0 = u_lam ** (1.0 / LRU_C)
    lru_lambda = jnp.log(a0) - jnp.log1p(-a0)
    return {
        "x": nrm((BATCH, SEQ, D), 1.0),
        "c": nrm((BATCH, D), 1.0),
        "ctx": nrm((BATCH, CTX_LEN, D), 1.0),
        "c_ctx": nrm((D,), 1.0),
        "ada_w_e": nrm((N_EVEN, D, 6 * D), 0.5 * D ** -0.5),
        "ada_b_e": nrm((N_EVEN, 6 * D), 0.02),
        "norm1_e": 1.0 + nrm((N_EVEN, D), 0.05),
        "norm2_e": 1.0 + nrm((N_EVEN, D), 0.05),
        "w_in_e": nrm((N_EVEN, D, IN_EVEN), D ** -0.5),
        "sgu_w": nrm((N_EVEN, SGU_GROUPS, CHUNK, CHUNK), 0.5 * CHUNK ** -0.5),
        "sgu_b": 1.0 + nrm((N_EVEN, SGU_GROUPS, CHUNK), 0.05),
        "attn_sink": nrm((N_EVEN, N_Q_HEADS), 0.5),
        "w_out_e": nrm((N_EVEN, MIX_EVEN, D), MIX_EVEN ** -0.5),
        "ffn_w1": nrm((N_EVEN, D, D_FF), D ** -0.5),
        "ffn_w3": nrm((N_EVEN, D, D_FF), D ** -0.5),
        "ffn_w2": nrm((N_EVEN, D_FF, D), D_FF ** -0.5),
        "ada_w_o": nrm((N_ODD, D, 6 * D), 0.5 * D ** -0.5),
        "ada_b_o": nrm((N_ODD, 6 * D), 0.02),
        "norm1_o": 1.0 + nrm((N_ODD, D), 0.05),
        "norm2_o": 1.0 + nrm((N_ODD, D), 0.05),
        "w_in_o": nrm((N_ODD, D, 2 * D_RNN), D ** -0.5),
        "conv_w": nrm((N_ODD, CONV_W, D_RNN), CONV_W ** -0.5),
        "conv_b": nrm((N_ODD, D_RNN), 0.02),
        "lru_wa": nrm((N_ODD, 2, LRU_BLOCKS, LRU_BLOCK, LRU_BLOCK), LRU_BLOCK ** -0.5),
        "lru_ba": nrm((N_ODD, 2, D_RNN), 0.02),
        "lru_wx": nrm((N_ODD, 2, LRU_BLOCKS, LRU_BLOCK, LRU_BLOCK), LRU_BLOCK ** -0.5),
        "lru_bx": nrm((N_ODD, 2, D_RNN), 0.02),
        "lru_lambda": lru_lambda,
        "w_out_o": nrm((N_ODD, D_RNN, D), D_RNN ** -0.5),
        "router_w": nrm((N_ODD, D, N_EXPERTS), D ** -0.5),
        "moe_w1": nrm((N_ODD, N_EXPERTS, D, D_FF), D ** -0.5),
        "moe_w3": nrm((N_ODD, N_EXPERTS, D, D_FF), D ** -0.5),
        "moe_w2": nrm((N_ODD, N_EXPERTS, D_FF, D), D_FF ** -0.5),
        "final_norm": 1.0 + nrm((D,), 0.05),
    }


def reference(x, c, ctx, c_ctx, ada_w_e, ada_b_e, norm1_e, norm2_e, w_in_e, sgu_w, sgu_b, attn_sink,
              w_out_e, ffn_w1, ffn_w3, ffn_w2, ada_w_o, ada_b_o, norm1_o, norm2_o, w_in_o, conv_w, conv_b,
              lru_wa, lru_ba, lru_wx, lru_bx, lru_lambda, w_out_o, router_w, moe_w1, moe_w3, moe_w2,
              final_norm):
    n_tok = x.shape[1]
    rows = n_tok // GRID_W
    cos, sin = axial_rope_tables(rows)
    c_act = jax.nn.silu(c)
    cc_act = jax.nn.silu(c_ctx)
    xc = ctx
    for layer in range(DEPTH):
        need_ctx_out = layer < DEPTH - 1
        j = layer // 2
        if layer % 2 == 0:
            x, xc = even_layer(x, xc, c_act, cc_act, cos, sin, ada_w_e[j], ada_b_e[j], norm1_e[j], norm2_e[j],
                               w_in_e[j], sgu_w[j], sgu_b[j], attn_sink[j], w_out_e[j],
                               ffn_w1[j], ffn_w3[j], ffn_w2[j], need_ctx_out)
        else:
            x, xc = odd_layer(x, xc, c_act, cc_act, ada_w_o[j], ada_b_o[j], norm1_o[j], norm2_o[j],
                              w_in_o[j], conv_w[j], conv_b[j], lru_wa[j], lru_ba[j], lru_wx[j], lru_bx[j],
                              lru_lambda[j], w_out_o[j], router_w[j], moe_w1[j], moe_w3[j], moe_w2[j],
                              need_ctx_out)
    return rmsnorm(x, final_norm)
```

```python
import functools

import jax
import jax.numpy as jnp
from jax import lax
from jax.experimental import pallas as pl
from jax.experimental.pallas import tpu as pltpu

F32 = jnp.float32
BF16 = jnp.bfloat16

D_MODEL = 1024
GRID_W = 64
EPS = 1e-6
NEG_INF = -1e30
CHUNK = 128
SGU_GROUPS = 4
SGU_WIDTH = 512
HEAD_DIM = 64
N_Q_HEADS = 8
N_KV_HEADS = 2
ATTN_WIDTH = 512
KV_WIDTH = 128
WINDOW = 128
ATTN_BLOCK = 128
ATTN_SCALE = HEAD_DIM ** -0.5
ROPE_BASE = 10000.0
ROPE_FREQS = 16
IN_EVEN = 1792
D_RNN = 1280
LRU_BLOCKS = 10
LRU_BLOCK = 128
LRU_C = 8.0
D_FF = 2816
N_EXPERTS = 8
LANES = 128
SUBLANES = 8
VMEM_LIMIT = 56 * 1024 * 1024


def _cparams(sem):
    return pltpu.CompilerParams(dimension_semantics=sem, vmem_limit_bytes=VMEM_LIMIT)


def _dot(a, b):
    return jnp.dot(a, b, preferred_element_type=F32)


def _dot_nt(a, b):
    return lax.dot_general(a, b, (((1,), (1,)), ((), ())), preferred_element_type=F32)


def _gelu(x):
    return 0.5 * x * (1.0 + jnp.tanh(0.7978845608028654 * (x + 0.044715 * (x * x * x))))


def _sigmoid(x):
    return 1.0 / (1.0 + jnp.exp(-x))


def _rms(x, nw):
    return (x * lax.rsqrt(jnp.mean(x * x, axis=-1, keepdims=True) + EPS)) * nw


def _rms_mod(x, nw, scale, shift):
    return _rms(x, nw) * (1.0 + scale) + shift


def _ada_kernel(c_ref, w_ref, b_ref, o_ref):
    c = c_ref[...]
    act = c * _sigmoid(c)
    o_ref[...] = jnp.dot(act, w_ref[...], precision=lax.Precision.HIGHEST,
                         preferred_element_type=F32) + b_ref[...]


def _ada_params(cvec, w, b):
    n = w.shape[1]
    tn = 1536
    out = pl.pallas_call(
        _ada_kernel,
        grid=(n // tn,),
        in_specs=[pl.BlockSpec((SUBLANES, D_MODEL), lambda j: (0, 0)),
                  pl.BlockSpec((D_MODEL, tn), lambda j: (0, j)),
                  pl.BlockSpec((1, tn), lambda j: (0, j))],
        out_specs=pl.BlockSpec((SUBLANES, tn), lambda j: (0, j)),
        out_shape=jax.ShapeDtypeStruct((SUBLANES, n), F32),
        compiler_params=_cparams(("parallel",)),
        name="ada_params",
    )(cvec, w, b.reshape(1, n))
    return out.reshape(SUBLANES, 6, D_MODEL)


def _proj_even_kernel(x_ref, mod_ref, nw_ref, w_ref, cos_ref, sin_ref,
                      u_ref, v_ref, q_ref, k_ref, val_ref):
    m = mod_ref[0]
    h = _rms_mod(x_ref[0], nw_ref[...], m[1:2], m[0:1]).astype(BF16)
    u_ref[0] = _gelu(_dot(h, w_ref[:, 0:SGU_WIDTH]))
    v_ref[0] = _gelu(_dot(h, w_ref[:, SGU_WIDTH:2 * SGU_WIDTH]))
    cos = cos_ref[...]
    sin = sin_ref[...]
    lane = lax.broadcasted_iota(jnp.int32, cos.shape, 1)
    first_half = (lane % 32) < ROPE_FREQS

    def rope(t):
        partner = jnp.where(first_half, pltpu.roll(t, LANES - ROPE_FREQS, 1), pltpu.roll(t, ROPE_FREQS, 1))
        return t * cos + partner * sin

    q = _dot(h, w_ref[:, 2 * SGU_WIDTH:2 * SGU_WIDTH + ATTN_WIDTH])
    for g in range(ATTN_WIDTH // LANES):
        q_ref[0, :, g * LANES:(g + 1) * LANES] = rope(q[:, g * LANES:(g + 1) * LANES]).astype(BF16)
    kv = _dot(h, w_ref[:, 2 * SGU_WIDTH + ATTN_WIDTH:IN_EVEN])
    k = rope(kv[:, 0:KV_WIDTH])
    val = kv[:, KV_WIDTH:2 * KV_WIDTH]
    k_ref[0, :, 0:LANES] = k.astype(BF16)
    k_ref[0, :, LANES:2 * LANES] = pltpu.roll(k, HEAD_DIM, 1).astype(BF16)
    val_ref[0, :, 0:LANES] = val.astype(BF16)
    val_ref[0, :, LANES:2 * LANES] = pltpu.roll(val, HEAD_DIM, 1).astype(BF16)


def _proj_even(x, mod, mod_per_batch, nw, w, cos, sin, tm):
    B, S, D = x.shape
    mod_map = (lambda b, i: (b, 0, 0)) if mod_per_batch else (lambda b, i: (0, 0, 0))
    tok = lambda width: pl.BlockSpec((1, tm, width), lambda b, i: (b, i, 0))
    return pl.pallas_call(
        _proj_even_kernel,
        grid=(B, S // tm),
        in_specs=[tok(D),
                  pl.BlockSpec((1, 6, D), mod_map),
                  pl.BlockSpec((1, D), lambda b, i: (0, 0)),
                  pl.BlockSpec((D, IN_EVEN), lambda b, i: (0, 0)),
                  pl.BlockSpec((tm, LANES), lambda b, i: (i, 0)),
                  pl.BlockSpec((tm, LANES), lambda b, i: (i, 0))],
        out_specs=[tok(SGU_WIDTH), tok(SGU_WIDTH), tok(ATTN_WIDTH), tok(2 * KV_WIDTH), tok(2 * KV_WIDTH)],
        out_shape=[jax.ShapeDtypeStruct((B, S, SGU_WIDTH), F32),
                   jax.ShapeDtypeStruct((B, S, SGU_WIDTH), F32),
                   jax.ShapeDtypeStruct((B, S, ATTN_WIDTH), BF16),
                   jax.ShapeDtypeStruct((B, S, 2 * KV_WIDTH), BF16),
                   jax.ShapeDtypeStruct((B, S, 2 * KV_WIDTH), BF16)],
        compiler_params=_cparams(("parallel", "parallel")),
        name="proj_even",
    )(x, mod, nw, w, cos, sin)


def _mixer_even_kernel(sink_ref, x_ref, mod_ref, u_ref, v_ref, q_ref, k_ref, val_ref, kc_ref, vc_ref,
                       ws_ref, bs_ref, wout_ref, o_ref, mix_ref, *, seq_len, is_ctx):
    tq = x_ref.shape[1]
    n_chunks = tq // CHUNK
    i = pl.program_id(1)
    lane = lax.broadcasted_iota(jnp.int32, (1, LANES), 1)
    lo = lane < HEAD_DIM
    zero = jnp.zeros((), BF16)

    def halves(ref_slice, kh):
        nat, swp = ref_slice[:, 0:LANES], ref_slice[:, LANES:2 * LANES]
        if kh == 0:
            return jnp.where(lo, nat, zero), jnp.where(lo, zero, swp)
        return jnp.where(lo, swp, zero), jnp.where(lo, zero, nat)

    kc_all = kc_ref[0]
    vc_all = vc_ref[0]

    def chunk_body(c, carry):
        r0 = pl.multiple_of(c * CHUNK, CHUNK)
        rows = pl.ds(r0, CHUNK)
        vch = v_ref[0, rows, :]
        uch = u_ref[0, rows, :]
        for g in range(SGU_GROUPS):
            cols = slice(g * LANES, (g + 1) * LANES)
            vg = vch[:, cols]
            dev = vg - jnp.mean(vg, axis=-1, keepdims=True)
            vn = dev * lax.rsqrt(jnp.mean(dev * dev, axis=-1, keepdims=True) + EPS)
            mixed = _dot(ws_ref[g], vn.astype(BF16)) + bs_ref[:, g:g + 1]
            mix_ref[rows, cols] = (uch[:, cols] * mixed).astype(BF16)
        qch = q_ref[0, rows, :]
        if not is_ctx:
            blk = i * n_chunks + c
            start = pl.multiple_of(jnp.clip((blk - 1) * ATTN_BLOCK, 0, seq_len - 3 * ATTN_BLOCK), ATTN_BLOCK)
            k3 = k_ref[0, pl.ds(start, 3 * ATTN_BLOCK), :]
            v3 = val_ref[0, pl.ds(start, 3 * ATTN_BLOCK), :]
            q_pos = blk * ATTN_BLOCK + lax.broadcasted_iota(jnp.int32, (ATTN_BLOCK, 3 * ATTN_BLOCK), 0)
            k_pos = start + lax.broadcasted_iota(jnp.int32, (ATTN_BLOCK, 3 * ATTN_BLOCK), 1)
            in_window = jnp.abs(k_pos - q_pos) <= WINDOW
        for kh in range(N_KV_HEADS):
            kc_h = halves(kc_all, kh)
            vc_h = halves(vc_all, kh)
            if not is_ctx:
                k_h = halves(k3, kh)
                v_h = halves(v3, kh)
            for p in (2 * kh, 2 * kh + 1):
                qg = qch[:, p * LANES:(p + 1) * LANES]
                acc = jnp.zeros((ATTN_BLOCK, LANES), F32)
                for half in range(2):
                    snk = sink_ref[2 * p + half]
                    s_ctx = _dot_nt(qg, kc_h[half]) * ATTN_SCALE
                    m = jnp.maximum(jnp.max(s_ctx, axis=-1, keepdims=True), snk)
                    if not is_ctx:
                        s_loc = jnp.where(in_window, _dot_nt(qg, k_h[half]) * ATTN_SCALE, NEG_INF)
                        m = jnp.maximum(m, jnp.max(s_loc, axis=-1, keepdims=True))
                    p_ctx = jnp.exp(s_ctx - m)
                    den = jnp.sum(p_ctx, axis=-1, keepdims=True) + jnp.exp(snk - m)
                    o = _dot(p_ctx.astype(BF16), vc_h[half])
                    if not is_ctx:
                        p_loc = jnp.exp(s_loc - m)
                        den = den + jnp.sum(p_loc, axis=-1, keepdims=True)
                        o = o + _dot(p_loc.astype(BF16), v_h[half])
                    acc = acc + o * (1.0 / den)
                mix_ref[rows, SGU_WIDTH + p * LANES:SGU_WIDTH + (p + 1) * LANES] = acc.astype(BF16)
        return carry

    lax.fori_loop(0, n_chunks, chunk_body, 0)
    y = _dot(mix_ref[...], wout_ref[...])
    o_ref[0] = x_ref[0] + mod_ref[0][2:3] * y


def _mixer_even(x, mod, mod_per_batch, u, v, q, k2, v2, kc2, vc2, ws, bs_t, sink, wout, tq, is_ctx):
    B, S, D = x.shape
    Sk = k2.shape[1]
    Lc = kc2.shape[1]
    mod_map = (lambda b, i: (b, 0, 0)) if mod_per_batch else (lambda b, i: (0, 0, 0))
    tok = lambda width: pl.BlockSpec((1, tq, width), lambda b, i: (b, i, 0))
    per_batch = lambda rows: pl.BlockSpec((1, rows, 2 * KV_WIDTH), lambda b, i: (b, 0, 0))
    return pl.pallas_call(
        functools.partial(_mixer_even_kernel, seq_len=S, is_ctx=is_ctx),
        grid=(B, S // tq),
        in_specs=[pl.BlockSpec(memory_space=pltpu.SMEM),
                  tok(D),
                  pl.BlockSpec((1, 6, D), mod_map),
                  tok(SGU_WIDTH), tok(SGU_WIDTH), tok(ATTN_WIDTH),
                  per_batch(Sk), per_batch(Sk), per_batch(Lc), per_batch(Lc),
                  pl.BlockSpec((SGU_GROUPS, CHUNK, CHUNK), lambda b, i: (0, 0, 0)),
                  pl.BlockSpec((CHUNK, SGU_GROUPS), lambda b, i: (0, 0)),
                  pl.BlockSpec((D, D), lambda b, i: (0, 0))],
        out_specs=tok(D),
        out_shape=jax.ShapeDtypeStruct((B, S, D), F32),
        scratch_shapes=[pltpu.VMEM((tq, D), BF16)],
        compiler_params=_cparams(("parallel", "arbitrary")),
        name="mixer_ctx" if is_ctx else "mixer_even",
    )(sink, x, mod, u, v, q, k2, v2, kc2, vc2, ws, bs_t, wout)


def _router_kernel(x_ref, mod_ref, nw_ref, rw_ref, g_ref):
    m = mod_ref[0]
    h = _rms_mod(x_ref[0], nw_ref[...], m[4:5], m[3:4])
    logits = jnp.dot(h, rw_ref[...], precision=lax.Precision.HIGHEST, preferred_element_type=F32)
    lane = lax.broadcasted_iota(jnp.int32, logits.shape, 1)
    lg = jnp.where(lane < N_EXPERTS, logits, -jnp.inf)
    m1 = jnp.max(lg, axis=-1, keepdims=True)
    i1 = jnp.min(jnp.where(lg == m1, lane, LANES), axis=-1, keepdims=True)
    lg2 = jnp.where(lane == i1, -jnp.inf, lg)
    m2 = jnp.max(lg2, axis=-1, keepdims=True)
    i2 = jnp.min(jnp.where(lg2 == m2, lane, LANES), axis=-1, keepdims=True)
    e2 = jnp.exp(m2 - m1)
    den = 1.0 + e2
    g_ref[0] = jnp.where(lane == i1, 1.0 / den, 0.0) + jnp.where(lane == i2, e2 / den, 0.0)


def _router(x, mod, nw, rw, tm):
    B, S, D = x.shape
    return pl.pallas_call(
        _router_kernel,
        grid=(B, S // tm),
        in_specs=[pl.BlockSpec((1, tm, D), lambda b, i: (b, i, 0)),
                  pl.BlockSpec((1, 6, D), lambda b, i: (b, 0, 0)),
                  pl.BlockSpec((1, D), lambda b, i: (0, 0)),
                  pl.BlockSpec((D, LANES), lambda b, i: (0, 0))],
        out_specs=pl.BlockSpec((1, tm, LANES), lambda b, i: (b, i, 0)),
        out_shape=jax.ShapeDtypeStruct((B, S, LANES), F32),
        compiler_params=_cparams(("parallel", "parallel")),
        name="router",
    )(x, mod, nw, rw)


def _ffn_kernel(*refs, gated, final_norm):
    x_ref, mod_ref, nw_ref = refs[0:3]
    pos = 3
    if gated:
        g_ref = refs[pos]
        pos += 1
    w1_ref, w3_ref, w2_ref = refs[pos:pos + 3]
    pos += 3
    if final_norm:
        fn_ref = refs[pos]
        pos += 1
    o_ref, h_ref, acc_ref = refs[pos:pos + 3]
    e = pl.program_id(2)
    f = pl.program_id(3)
    first = jnp.logical_and(e == 0, f == 0)
    last = jnp.logical_and(e == pl.num_programs(2) - 1, f == pl.num_programs(3) - 1)

    @pl.when(first)
    def _():
        m = mod_ref[0]
        h_ref[...] = _rms_mod(x_ref[0], nw_ref[...], m[4:5], m[3:4]).astype(BF16)
        acc_ref[...] = jnp.zeros_like(acc_ref)

    h = h_ref[...]
    a = _dot(h, w1_ref[0])
    b = _dot(h, w3_ref[0])
    act = ((a * _sigmoid(a)) * b).astype(BF16)
    y = _dot(act, w2_ref[0])
    if gated:
        g = g_ref[0]
        lane = lax.broadcasted_iota(jnp.int32, g.shape, 1)
        y = jnp.sum(jnp.where(lane == e, g, 0.0), axis=-1, keepdims=True) * y
    acc_ref[...] += y

    @pl.when(last)
    def _():
        out = x_ref[0] + mod_ref[0][5:6] * acc_ref[...]
        if final_norm:
            out = _rms(out, fn_ref[...])
        o_ref[0] = out


def _ffn(x, mod, mod_per_batch, nw, w1, w3, w2, tm, tf, gates=None, fin=None):
    B, S, D = x.shape
    E, _, F = w1.shape
    mod_map = (lambda b, i, e, f: (b, 0, 0)) if mod_per_batch else (lambda b, i, e, f: (0, 0, 0))
    in_specs = [pl.BlockSpec((1, tm, D), lambda b, i, e, f: (b, i, 0)),
                pl.BlockSpec((1, 6, D), mod_map),
                pl.BlockSpec((1, D), lambda b, i, e, f: (0, 0))]
    args = [x, mod, nw]
    if gates is not None:
        in_specs.append(pl.BlockSpec((1, tm, LANES), lambda b, i, e, f: (b, i, 0)))
        args.append(gates)
    in_specs += [pl.BlockSpec((1, D, tf), lambda b, i, e, f: (e, 0, f)),
                 pl.BlockSpec((1, D, tf), lambda b, i, e, f: (e, 0, f)),
                 pl.BlockSpec((1, tf, D), lambda b, i, e, f: (e, f, 0))]
    args += [w1, w3, w2]
    if fin is not None:
        in_specs.append(pl.BlockSpec((1, D), lambda b, i, e, f: (0, 0)))
        args.append(fin)
    return pl.pallas_call(
        functools.partial(_ffn_kernel, gated=gates is not None, final_norm=fin is not None),
        grid=(B, S // tm, E, F // tf),
        in_specs=in_specs,
        out_specs=pl.BlockSpec((1, tm, D), lambda b, i, e, f: (b, i, 0)),
        out_shape=jax.ShapeDtypeStruct((B, S, D), F32),
        scratch_shapes=[pltpu.VMEM((tm, D), BF16), pltpu.VMEM((tm, D), F32)],
        compiler_params=_cparams(("parallel", "parallel", "arbitrary", "arbitrary")),
        name="moe_ffn" if gates is not None else "ffn",
    )(*args)


def _proj_odd_kernel(x_ref, mod_ref, nw_ref, w_ref, gate_ref, rec_ref):
    m = mod_ref[0]
    h = _rms_mod(x_ref[0], nw_ref[...], m[1:2], m[0:1]).astype(BF16)
    gate_ref[0] = _gelu(_dot(h, w_ref[:, 0:D_RNN]))
    rec_ref[0] = _dot(h, w_ref[:, D_RNN:2 * D_RNN])


def _proj_odd(x, mod, mod_per_batch, nw, w, tm):
    B, S, D = x.shape
    mod_map = (lambda b, i: (b, 0, 0)) if mod_per_batch else (lambda b, i: (0, 0, 0))
    tok = lambda width: pl.BlockSpec((1, tm, width), lambda b, i: (b, i, 0))
    return pl.pallas_call(
        _proj_odd_kernel,
        grid=(B, S // tm),
        in_specs=[tok(D),
                  pl.BlockSpec((1, 6, D), mod_map),
                  pl.BlockSpec((1, D), lambda b, i: (0, 0)),
                  pl.BlockSpec((D, 2 * D_RNN), lambda b, i: (0, 0))],
        out_specs=[tok(D_RNN), tok(D_RNN)],
        out_shape=[jax.ShapeDtypeStruct((B, S, D_RNN), F32), jax.ShapeDtypeStruct((B, S, D_RNN), F32)],
        compiler_params=_cparams(("parallel", "parallel")),
        name="proj_odd",
    )(x, mod, nw, w)


def _scan8(a, b, h, row, reverse):
    for s in (1, 2, 4):
        if reverse:
            a_s, b_s, live = pltpu.roll(a, SUBLANES - s, 0), pltpu.roll(b, SUBLANES - s, 0), row < SUBLANES - s
        else:
            a_s, b_s, live = pltpu.roll(a, s, 0), pltpu.roll(b, s, 0), row >= s
        b = jnp.where(live, a * b_s + b, b)
        a = jnp.where(live, a * a_s, a)
    hr = a * h + b
    return hr, (hr[0:1] if reverse else hr[SUBLANES - 1:SUBLANES])


def _lru_kernel(rec_ref, recc_ref, cw_ref, cb_ref, wa_ref, ba_ref, wx_ref, bx_ref, lam_ref,
                s_ref, pad_ref, a_ref, b_ref, cpad_ref, ca_ref, cbb_ref, *, tile):
    S = rec_ref.shape[1]
    L = recc_ref.shape[1]
    cw = cw_ref[...]
    cb = cb_ref[...]
    lam = lam_ref[...]
    sp = jnp.maximum(-lam, 0.0) + jnp.log1p(jnp.exp(-jnp.abs(lam)))
    zeros8 = jnp.zeros((SUBLANES, LANES), F32)

    def coefficients(src_ref, dst_a, dst_b, n_rows, t):
        pad = cpad_ref if src_ref is recc_ref else pad_ref
        pad[0:SUBLANES, :] = zeros8
        pad[SUBLANES + n_rows:2 * SUBLANES + n_rows, :] = zeros8

        def copy(j, carry):
            r0 = pl.multiple_of(j * t, t)
            pad[pl.ds(SUBLANES + r0, t), :] = src_ref[0, pl.ds(r0, t), :]
            return carry

        lax.fori_loop(0, n_rows // t, copy, 0)

        def body(j, carry):
            r0 = pl.multiple_of(j * t, t)
            ext = pad[pl.ds(r0, t + 2 * SUBLANES), :]
            conv = cb
            for tap in range(4):
                conv = conv + cw[tap:tap + 1] * ext[SUBLANES - 2 + tap:SUBLANES - 2 + tap + t]
            cbf = conv.astype(BF16)
            for d in range(2):
                r = _sigmoid(_dot(cbf, wa_ref[d, 0]) + ba_ref[d:d + 1])
                gi = _sigmoid(_dot(cbf, wx_ref[d, 0]) + bx_ref[d:d + 1])
                log_a = -LRU_C * r * sp[d:d + 1]
                th = jnp.tanh(log_a)
                dst_a[d, pl.ds(r0, t), :] = jnp.exp(log_a)
                dst_b[d, pl.ds(r0, t), :] = jnp.sqrt(-2.0 * th / (1.0 - th)) * (gi * conv)
            return carry

        lax.fori_loop(0, n_rows // t, body, 0)

    row = lax.broadcasted_iota(jnp.int32, (SUBLANES, LANES), 0)
    h_zero = jnp.zeros((1, LANES), F32)

    coefficients(recc_ref, ca_ref, cbb_ref, L, L)
    nc = L // SUBLANES

    def ctx_body(j, carry):
        hf, hb = carry
        rf = pl.multiple_of(j * SUBLANES, SUBLANES)
        rb = pl.multiple_of((nc - 1 - j) * SUBLANES, SUBLANES)
        _, hf = _scan8(ca_ref[0, pl.ds(rf, SUBLANES), :], cbb_ref[0, pl.ds(rf, SUBLANES), :], hf, row, False)
        _, hb = _scan8(ca_ref[1, pl.ds(rb, SUBLANES), :], cbb_ref[1, pl.ds(rb, SUBLANES), :], hb, row, True)
        return hf, hb

    h0f, h0b = lax.fori_loop(0, nc, ctx_body, (h_zero, h_zero))

    coefficients(rec_ref, a_ref, b_ref, S, tile)
    n = S // SUBLANES

    def lat_body(accumulate):
        def body(j, carry):
            hf, hb = carry
            rf = pl.multiple_of(j * SUBLANES, SUBLANES)
            rb = pl.multiple_of((n - 1 - j) * SUBLANES, SUBLANES)
            of, hf = _scan8(a_ref[0, pl.ds(rf, SUBLANES), :], b_ref[0, pl.ds(rf, SUBLANES), :], hf, row, False)
            ob, hb = _scan8(a_ref[1, pl.ds(rb, SUBLANES), :], b_ref[1, pl.ds(rb, SUBLANES), :], hb, row, True)
            if accumulate:
                s_ref[0, pl.ds(rf, SUBLANES), :] += of
                s_ref[0, pl.ds(rb, SUBLANES), :] += ob
            else:
                s_ref[0, pl.ds(rf, SUBLANES), :] = of
                s_ref[0, pl.ds(rb, SUBLANES), :] = ob
            return hf, hb
        return body

    mid = lax.fori_loop(0, n // 2, lat_body(False), (h0f, h0b), unroll=4)
    lax.fori_loop(n // 2, n, lat_body(True), mid, unroll=4)


def _lru(rec, rec_c, conv_w, conv_b, wa, ba, wx, bx, lam, tile):
    B, S, _ = rec.shape
    L = rec_c.shape[1]
    blk = lambda rows: pl.BlockSpec((1, rows, LRU_BLOCK), lambda b, j: (b, 0, j))
    vec = lambda rows: pl.BlockSpec((rows, LRU_BLOCK), lambda b, j: (0, j))
    wspec = pl.BlockSpec((2, 1, LRU_BLOCK, LRU_BLOCK), lambda b, j: (0, j, 0, 0))
    return pl.pallas_call(
        functools.partial(_lru_kernel, tile=tile),
        grid=(B, LRU_BLOCKS),
        in_specs=[blk(S), blk(L), vec(4), vec(1), wspec, vec(2), wspec, vec(2), vec(2)],
        out_specs=blk(S),
        out_shape=jax.ShapeDtypeStruct((B, S, D_RNN), F32),
        scratch_shapes=[pltpu.VMEM((S + 2 * SUBLANES, LRU_BLOCK), F32),
                        pltpu.VMEM((2, S, LRU_BLOCK), F32),
                        pltpu.VMEM((2, S, LRU_BLOCK), F32),
                        pltpu.VMEM((L + 2 * SUBLANES, LRU_BLOCK), F32),
                        pltpu.VMEM((2, L, LRU_BLOCK), F32),
                        pltpu.VMEM((2, L, LRU_BLOCK), F32)],
        compiler_params=_cparams(("parallel", "parallel")),
        name="lru_scan",
    )(rec, rec_c, conv_w, conv_b, wa, ba, wx, bx, lam)


def _lru_out_kernel(x_ref, mod_ref, gate_ref, s_ref, w_ref, o_ref):
    y = (gate_ref[0] * s_ref[0]).astype(BF16)
    o_ref[0] = x_ref[0] + mod_ref[0][2:3] * _dot(y, w_ref[...])


def _lru_out(x, mod, gate, s, w, tm):
    B, S, D = x.shape
    tok = lambda width: pl.BlockSpec((1, tm, width), lambda b, i: (b, i, 0))
    return pl.pallas_call(
        _lru_out_kernel,
        grid=(B, S // tm),
        in_specs=[tok(D), pl.BlockSpec((1, 6, D), lambda b, i: (b, 0, 0)), tok(D_RNN), tok(D_RNN),
                  pl.BlockSpec((D_RNN, D), lambda b, i: (0, 0))],
        out_specs=tok(D),
        out_shape=jax.ShapeDtypeStruct((B, S, D), F32),
        compiler_params=_cparams(("parallel", "parallel")),
        name="lru_out",
    )(x, mod, gate, s, w)


def _rope_tables(n_tok):
    rows = n_tok // GRID_W
    row = jnp.repeat(jnp.arange(rows, dtype=F32), GRID_W)
    col = jnp.tile(jnp.arange(GRID_W, dtype=F32), rows)
    freqs = ROPE_BASE ** (-jnp.arange(ROPE_FREQS, dtype=F32) / ROPE_FREQS)
    ar, ac = row[:, None] * freqs, col[:, None] * freqs
    cos = jnp.concatenate([jnp.cos(ar), jnp.cos(ar), jnp.cos(ac), jnp.cos(ac)], axis=-1)
    sin = jnp.concatenate([-jnp.sin(ar), jnp.sin(ar), -jnp.sin(ac), jnp.sin(ac)], axis=-1)
    return jnp.tile(cos, (1, LANES // HEAD_DIM)), jnp.tile(sin, (1, LANES // HEAD_DIM))


def kernel(x, c, ctx, c_ctx, ada_w_e, ada_b_e, norm1_e, norm2_e, w_in_e, sgu_w, sgu_b, attn_sink, w_out_e, ffn_w1, ffn_w3, ffn_w2, ada_w_o, ada_b_o, norm1_o, norm2_o, w_in_o, conv_w, conv_b, lru_wa, lru_ba, lru_wx, lru_bx, lru_lambda, w_out_o, router_w, moe_w1, moe_w3, moe_w2, final_norm):
    B, S, D = x.shape
    L = ctx.shape[1]
    cvec = jnp.concatenate([c, c_ctx[None], jnp.zeros((SUBLANES - B - 1, D), F32)], axis=0)
    mod_e = _ada_params(cvec, ada_w_e[0], ada_b_e[0])
    mod_o = _ada_params(cvec, ada_w_o[0], ada_b_o[0])
    lat_e, ctx_e = mod_e[0:B], mod_e[B:B + 1]
    lat_o, ctx_o = mod_o[0:B], mod_o[B:B + 1]
    bf = lambda t: t.astype(BF16)
    row = lambda t: t.reshape(1, -1)

    cos, sin = _rope_tables(S)
    cos_c, sin_c = jnp.ones((L, LANES), F32), jnp.zeros((L, LANES), F32)
    w_in = bf(w_in_e[0])
    n1, n2 = row(norm1_e[0]), row(norm2_e[0])
    uc, vc, qc, kc2, vc2 = _proj_even(ctx, ctx_e, False, n1, w_in, cos_c, sin_c, L)
    u, v, q, k2, v2 = _proj_even(x, lat_e, True, n1, w_in, cos, sin, 512)
    ws, bs_t, wout = bf(sgu_w[0]), sgu_b[0].T, bf(w_out_e[0])
    sink = attn_sink[0]
    x = _mixer_even(x, lat_e, True, u, v, q, k2, v2, kc2, vc2, ws, bs_t, sink, wout, 512, False)
    xc = _mixer_even(ctx, ctx_e, False, uc, vc, qc, kc2, vc2, kc2, vc2, ws, bs_t, sink, wout, L, True)
    w1, w3, w2 = bf(ffn_w1), bf(ffn_w3), bf(ffn_w2)
    x = _ffn(x, lat_e, True, n2, w1, w3, w2, 512, 1408)
    xc = _ffn(xc, ctx_e, False, n2, w1, w3, w2, L, 1408)

    w_in = bf(w_in_o[0])
    n1, n2 = row(norm1_o[0]), row(norm2_o[0])
    _, rec_c = _proj_odd(xc, ctx_o, False, n1, w_in, L)
    gate, rec = _proj_odd(x, lat_o, True, n1, w_in, 512)
    s = _lru(rec, rec_c, conv_w[0], row(conv_b[0]), bf(lru_wa[0]), lru_ba[0], bf(lru_wx[0]), lru_bx[0],
             lru_lambda[0], 512)
    x = _lru_out(x, lat_o, gate, s, bf(w_out_o[0]), 512)
    rw = jnp.pad(router_w[0], ((0, 0), (0, LANES - N_EXPERTS)))
    gates = _router(x, lat_o, n2, rw, 512)
    return _ffn(x, lat_o, True, n2, bf(moe_w1[0]), bf(moe_w3[0]), bf(moe_w2[0]), 512, 1408,
                gates=gates, fin=row(final_norm))
```

```python
import functools

import jax
import jax.numpy as jnp
from jax import lax
from jax.experimental import pallas as pl
from jax.experimental.pallas import tpu as pltpu

F32 = jnp.float32
BF16 = jnp.bfloat16

D_MODEL = 1024
GRID_W = 64
EPS = 1e-6
NEG_INF = -1e30
CHUNK = 128
SGU_GROUPS = 4
SGU_WIDTH = 512
HEAD_DIM = 64
N_Q_HEADS = 8
N_KV_HEADS = 2
ATTN_WIDTH = 512
KV_WIDTH = 128
WINDOW = 128
ATTN_BLOCK = 128
ATTN_SCALE = HEAD_DIM ** -0.5
ROPE_BASE = 10000.0
ROPE_FREQS = 16
IN_EVEN = 1792
D_RNN = 1280
LRU_BLOCKS = 10
LRU_BLOCK = 128
LRU_C = 8.0
D_FF = 2816
N_EXPERTS = 8
LANES = 128
SUBLANES = 8
VMEM_LIMIT = 56 * 1024 * 1024


def _cparams(sem):
    return pltpu.CompilerParams(dimension_semantics=sem, vmem_limit_bytes=VMEM_LIMIT)


def _dot(a, b):
    return jnp.dot(a, b, preferred_element_type=F32)


def _dot_nt(a, b):
    return lax.dot_general(a, b, (((1,), (1,)), ((), ())), preferred_element_type=F32)


def _gelu(x):
    return 0.5 * x * (1.0 + jnp.tanh(0.7978845608028654 * (x + 0.044715 * (x * x * x))))


def _sigmoid(x):
    return 1.0 / (1.0 + jnp.exp(-x))


def _rms(x, nw):
    return (x * lax.rsqrt(jnp.mean(x * x, axis=-1, keepdims=True) + EPS)) * nw


def _rms_mod(x, nw, scale, shift):
    return _rms(x, nw) * (1.0 + scale) + shift


def _ada_kernel(c_ref, w_ref, b_ref, o_ref):
    c = c_ref[...]
    act = c * _sigmoid(c)
    o_ref[...] = jnp.dot(act, w_ref[...], precision=lax.Precision.HIGHEST,
                         preferred_element_type=F32) + b_ref[...]


def _ada_params(cvec, w, b):
    n = w.shape[1]
    tn = 1536
    out = pl.pallas_call(
        _ada_kernel,
        grid=(n // tn,),
        in_specs=[pl.BlockSpec((SUBLANES, D_MODEL), lambda j: (0, 0)),
                  pl.BlockSpec((D_MODEL, tn), lambda j: (0, j)),
                  pl.BlockSpec((1, tn), lambda j: (0, j))],
        out_specs=pl.BlockSpec((SUBLANES, tn), lambda j: (0, j)),
        out_shape=jax.ShapeDtypeStruct((SUBLANES, n), F32),
        compiler_params=_cparams(("parallel",)),
        name="ada_params",
    )(cvec, w, b.reshape(1, n))
    return out.reshape(SUBLANES, 6, D_MODEL)


def _proj_even_kernel(x_ref, mod_ref, nw_ref, w_ref, cos_ref, sin_ref,
                      u_ref, v_ref, q_ref, k_ref, val_ref):
    m = mod_ref[0]
    h = _rms_mod(x_ref[0], nw_ref[...], m[1:2], m[0:1]).astype(BF16)
    u_ref[0] = _gelu(_dot(h, w_ref[:, 0:SGU_WIDTH]))
    v_ref[0] = _gelu(_dot(h, w_ref[:, SGU_WIDTH:2 * SGU_WIDTH]))
    cos = cos_ref[...]
    sin = sin_ref[...]
    lane = lax.broadcasted_iota(jnp.int32, cos.shape, 1)
    first_half = (lane % 32) < ROPE_FREQS

    def rope(t):
        partner = jnp.where(first_half, pltpu.roll(t, LANES - ROPE_FREQS, 1), pltpu.roll(t, ROPE_FREQS, 1))
        return t * cos + partner * sin

    q = _dot(h, w_ref[:, 2 * SGU_WIDTH:2 * SGU_WIDTH + ATTN_WIDTH])
    for g in range(ATTN_WIDTH // LANES):
        q_ref[0, :, g * LANES:(g + 1) * LANES] = rope(q[:, g * LANES:(g + 1) * LANES]).astype(BF16)
    kv = _dot(h, w_ref[:, 2 * SGU_WIDTH + ATTN_WIDTH:IN_EVEN])
    k = rope(kv[:, 0:KV_WIDTH])
    val = kv[:, KV_WIDTH:2 * KV_WIDTH]
    k_ref[0, :, 0:LANES] = k.astype(BF16)
    k_ref[0, :, LANES:2 * LANES] = pltpu.roll(k, HEAD_DIM, 1).astype(BF16)
    val_ref[0, :, 0:LANES] = val.astype(BF16)
    val_ref[0, :, LANES:2 * LANES] = pltpu.roll(val, HEAD_DIM, 1).astype(BF16)


def _proj_even(x, mod, mod_per_batch, nw, w, cos, sin, tm):
    B, S, D = x.shape
    mod_map = (lambda b, i: (b, 0, 0)) if mod_per_batch else (lambda b, i: (0, 0, 0))
    tok = lambda width: pl.BlockSpec((1, tm, width), lambda b, i: (b, i, 0))
    return pl.pallas_call(
        _proj_even_kernel,
        grid=(B, S // tm),
        in_specs=[tok(D),
                  pl.BlockSpec((1, 6, D), mod_map),
                  pl.BlockSpec((1, D), lambda b, i: (0, 0)),
                  pl.BlockSpec((D, IN_EVEN), lambda b, i: (0, 0)),
                  pl.BlockSpec((tm, LANES), lambda b, i: (i, 0)),
                  pl.BlockSpec((tm, LANES), lambda b, i: (i, 0))],
        out_specs=[tok(SGU_WIDTH), tok(SGU_WIDTH), tok(ATTN_WIDTH), tok(2 * KV_WIDTH), tok(2 * KV_WIDTH)],
        out_shape=[jax.ShapeDtypeStruct((B, S, SGU_WIDTH), F32),
                   jax.ShapeDtypeStruct((B, S, SGU_WIDTH), F32),
                   jax.ShapeDtypeStruct((B, S, ATTN_WIDTH), BF16),
                   jax.ShapeDtypeStruct((B, S, 2 * KV_WIDTH), BF16),
                   jax.ShapeDtypeStruct((B, S, 2 * KV_WIDTH), BF16)],
        compiler_params=_cparams(("parallel", "parallel")),
        name="proj_even",
    )(x, mod, nw, w, cos, sin)


def _mixer_even_kernel(sink_ref, x_ref, mod_ref, u_ref, v_ref, q_ref, k_ref, val_ref, kc_ref, vc_ref,
                       ws_ref, bs_ref, wout_ref, o_ref, mix_ref, *, seq_len, is_ctx):
    tq = x_ref.shape[1]
    n_chunks = tq // CHUNK
    i = pl.program_id(1)
    lane = lax.broadcasted_iota(jnp.int32, (1, LANES), 1)
    lo = lane < HEAD_DIM
    zero = jnp.zeros((), BF16)

    def halves(ref_slice, kh):
        nat, swp = ref_slice[:, 0:LANES], ref_slice[:, LANES:2 * LANES]
        if kh == 0:
            return jnp.where(lo, nat, zero), jnp.where(lo, zero, swp)
        return jnp.where(lo, swp, zero), jnp.where(lo, zero, nat)

    kc_all = kc_ref[0]
    vc_all = vc_ref[0]

    def chunk_body(c, carry):
        r0 = pl.multiple_of(c * CHUNK, CHUNK)
        rows = pl.ds(r0, CHUNK)
        vch = v_ref[0, rows, :]
        uch = u_ref[0, rows, :]
        for g in range(SGU_GROUPS):
            cols = slice(g * LANES, (g + 1) * LANES)
            vg = vch[:, cols]
            dev = vg - jnp.mean(vg, axis=-1, keepdims=True)
            vn = dev * lax.rsqrt(jnp.mean(dev * dev, axis=-1, keepdims=True) + EPS)
            mixed = _dot(ws_ref[g], vn.astype(BF16)) + bs_ref[:, g:g + 1]
            mix_ref[rows, cols] = (uch[:, cols] * mixed).astype(BF16)
        qch = q_ref[0, rows, :]
        if not is_ctx:
            blk = i * n_chunks + c
            start = pl.multiple_of(jnp.clip((blk - 1) * ATTN_BLOCK, 0, seq_len - 3 * ATTN_BLOCK), ATTN_BLOCK)
            k3 = k_ref[0, pl.ds(start, 3 * ATTN_BLOCK), :]
            v3 = val_ref[0, pl.ds(start, 3 * ATTN_BLOCK), :]
            q_pos = blk * ATTN_BLOCK + lax.broadcasted_iota(jnp.int32, (ATTN_BLOCK, 3 * ATTN_BLOCK), 0)
            k_pos = start + lax.broadcasted_iota(jnp.int32, (ATTN_BLOCK, 3 * ATTN_BLOCK), 1)
            in_window = jnp.abs(k_pos - q_pos) <= WINDOW
        for kh in range(N_KV_HEADS):
            kc_h = halves(kc_all, kh)
            vc_h = halves(vc_all, kh)
            if not is_ctx:
                k_h = halves(k3, kh)
                v_h = halves(v3, kh)
            for p in (2 * kh, 2 * kh + 1):
                qg = qch[:, p * LANES:(p + 1) * LANES]
                acc = jnp.zeros((ATTN_BLOCK, LANES), F32)
                for half in range(2):
                    snk = sink_ref[2 * p + half]
                    s_ctx = _dot_nt(qg, kc_h[half]) * ATTN_SCALE
                    m = jnp.maximum(jnp.max(s_ctx, axis=-1, keepdims=True), snk)
                    if not is_ctx:
                        s_loc = jnp.where(in_window, _dot_nt(qg, k_h[half]) * ATTN_SCALE, NEG_INF)
                        m = jnp.maximum(m, jnp.max(s_loc, axis=-1, keepdims=True))
                    p_ctx = jnp.exp(s_ctx - m)
                    den = jnp.sum(p_ctx, axis=-1, keepdims=True) + jnp.exp(snk - m)
                    o = _dot(p_ctx.astype(BF16), vc_h[half])
                    if not is_ctx:
                        p_loc = jnp.exp(s_loc - m)
                        den = den + jnp.sum(p_loc, axis=-1, keepdims=True)
                        o = o + _dot(p_loc.astype(BF16), v_h[half])
                    acc = acc + o * (1.0 / den)
                mix_ref[rows, SGU_WIDTH + p * LANES:SGU_WIDTH + (p + 1) * LANES] = acc.astype(BF16)
        return carry

    lax.fori_loop(0, n_chunks, chunk_body, 0)
    y = _dot(mix_ref[...], wout_ref[...])
    o_ref[0] = x_ref[0] + mod_ref[0][2:3] * y


def _mixer_even(x, mod, mod_per_batch, u, v, q, k2, v2, kc2, vc2, ws, bs_t, sink, wout, tq, is_ctx):
    B, S, D = x.shape
    Sk = k2.shape[1]
    Lc = kc2.shape[1]
    mod_map = (lambda b, i: (b, 0, 0)) if mod_per_batch else (lambda b, i: (0, 0, 0))
    tok = lambda width: pl.BlockSpec((1, tq, width), lambda b, i: (b, i, 0))
    per_batch = lambda rows: pl.BlockSpec((1, rows, 2 * KV_WIDTH), lambda b, i: (b, 0, 0))
    return pl.pallas_call(
        functools.partial(_mixer_even_kernel, seq_len=S, is_ctx=is_ctx),
        grid=(B, S // tq),
        in_specs=[pl.BlockSpec(memory_space=pltpu.SMEM),
                  tok(D),
                  pl.BlockSpec((1, 6, D), mod_map),
                  tok(SGU_WIDTH), tok(SGU_WIDTH), tok(ATTN_WIDTH),
                  per_batch(Sk), per_batch(Sk), per_batch(Lc), per_batch(Lc),
                  pl.BlockSpec((SGU_GROUPS, CHUNK, CHUNK), lambda b, i: (0, 0, 0)),
                  pl.BlockSpec((CHUNK, SGU_GROUPS), lambda b, i: (0, 0)),
                  pl.BlockSpec((D, D), lambda b, i: (0, 0))],
        out_specs=tok(D),
        out_shape=jax.ShapeDtypeStruct((B, S, D), F32),
        scratch_shapes=[pltpu.VMEM((tq, D), BF16)],
        compiler_params=_cparams(("parallel", "arbitrary")),
        name="mixer_ctx" if is_ctx else "mixer_even",
    )(sink, x, mod, u, v, q, k2, v2, kc2, vc2, ws, bs_t, wout)


def _route_kernel(x_ref, mod_ref, nw_ref, rw_ref, hn_ref, meta_ref, pos_ref, cnt_ref, base_ref, *, cap):
    @pl.when(jnp.logical_and(pl.program_id(0) == 0, pl.program_id(1) == 0))
    def _():
        base_ref[...] = jnp.zeros_like(base_ref)

    m = mod_ref[0]
    h = _rms_mod(x_ref[0], nw_ref[...], m[4:5], m[3:4])
    hn_ref[0] = h
    w = rw_ref[...]
    w_hi = w.astype(BF16)
    w_lo = (w - w_hi.astype(F32)).astype(BF16)
    h_hi = h.astype(BF16)
    h_lo = (h - h_hi.astype(F32)).astype(BF16)
    logits = _dot(h_hi, w_hi) + (_dot(h_lo, w_hi) + _dot(h_hi, w_lo))
    tm = logits.shape[0]
    lane = lax.broadcasted_iota(jnp.int32, logits.shape, 1)
    lg = jnp.where(lane < N_EXPERTS, logits, -jnp.inf)
    m1 = jnp.max(lg, axis=-1, keepdims=True)
    i1 = jnp.min(jnp.where(lg == m1, lane, LANES), axis=-1, keepdims=True)
    lg2 = jnp.where(lane == i1, -jnp.inf, lg)
    m2 = jnp.max(lg2, axis=-1, keepdims=True)
    i2 = jnp.min(jnp.where(lg2 == m2, lane, LANES), axis=-1, keepdims=True)
    e2 = jnp.exp(m2 - m1)
    den = 1.0 + e2
    hot = jnp.where(jnp.logical_or(lane == i1, lane == i2), 1.0, 0.0)
    r = lax.broadcasted_iota(jnp.int32, (tm, tm), 0)
    c = lax.broadcasted_iota(jnp.int32, (tm, tm), 1)
    before = jnp.where(r > c, 1.0, 0.0).astype(BF16)
    tot = base_ref[...] + _dot(before, hot.astype(BF16))
    rank1 = jnp.sum(jnp.where(lane == i1, tot, 0.0), axis=-1, keepdims=True)
    rank2 = jnp.sum(jnp.where(lane == i2, tot, 0.0), axis=-1, keepdims=True)
    pos1 = i1.astype(F32) * cap + rank1
    pos2 = i2.astype(F32) * cap + rank2
    meta = (jnp.where(lane == 0, pos1, 0.0) + jnp.where(lane == 1, pos2, 0.0)
            + jnp.where(lane == 2, 1.0 / den, 0.0) + jnp.where(lane == 3, e2 / den, 0.0))
    meta_ref[0] = meta
    pos_ref[0, 0] = meta.T[0:SUBLANES].astype(jnp.int32)
    base_ref[...] += jnp.sum(hot, axis=0, keepdims=True)
    cnt_ref[...] = jnp.broadcast_to(base_ref[...], cnt_ref.shape)


def _route(x, mod, nw, rw, tm, cap):
    B, S, D = x.shape
    return pl.pallas_call(
        functools.partial(_route_kernel, cap=float(cap)),
        grid=(B, S // tm),
        in_specs=[pl.BlockSpec((1, tm, D), lambda b, i: (b, i, 0)),
                  pl.BlockSpec((1, 6, D), lambda b, i: (b, 0, 0)),
                  pl.BlockSpec((1, D), lambda b, i: (0, 0)),
                  pl.BlockSpec((D, LANES), lambda b, i: (0, 0))],
        out_specs=[pl.BlockSpec((1, tm, D), lambda b, i: (b, i, 0)),
                   pl.BlockSpec((1, tm, LANES), lambda b, i: (b, i, 0)),
                   pl.BlockSpec((1, 1, SUBLANES, tm), lambda b, i: (b, i, 0, 0)),
                   pl.BlockSpec((SUBLANES, LANES), lambda b, i: (0, 0))],
        out_shape=[jax.ShapeDtypeStruct((B, S, D), F32),
                   jax.ShapeDtypeStruct((B, S, LANES), F32),
                   jax.ShapeDtypeStruct((B, S // tm, SUBLANES, tm), jnp.int32),
                   jax.ShapeDtypeStruct((SUBLANES, LANES), F32)],
        scratch_shapes=[pltpu.VMEM((1, LANES), F32)],
        compiler_params=_cparams(("arbitrary", "arbitrary")),
        name="moe_route",
    )(x, mod, nw, rw)


def _dispatch_kernel(pos_ref, cnt_ref, hn_ref, xs_ref, zero_ref, sem, zsem, *, n_tok, chunk, cap, tg):
    zero_ref[...] = jnp.zeros_like(zero_ref)

    def tail_copy(e):
        start = pl.multiple_of(e * cap + (cnt_ref[e] // SUBLANES) * SUBLANES, SUBLANES)
        return pltpu.make_async_copy(zero_ref, xs_ref.at[pl.ds(start, tg)], zsem)

    for e in range(N_EXPERTS):
        tail_copy(e).start()
    for e in range(N_EXPERTS):
        tail_copy(e).wait()

    def chunk_wait():
        pltpu.make_async_copy(hn_ref.at[pl.ds(0, 2 * chunk)], xs_ref.at[pl.ds(0, 2 * chunk)], sem).wait()

    def chunk_body(j, carry):
        def tok_body(t, c):
            tok = j * chunk + t
            pltpu.make_async_copy(hn_ref.at[pl.ds(tok, 1)], xs_ref.at[pl.ds(pos_ref[tok], 1)], sem).start()
            pltpu.make_async_copy(hn_ref.at[pl.ds(tok, 1)], xs_ref.at[pl.ds(pos_ref[n_tok + tok], 1)], sem).start()
            return c

        lax.fori_loop(0, chunk, tok_body, 0, unroll=8)

        @pl.when(j > 0)
        def _():
            chunk_wait()

        return carry

    lax.fori_loop(0, n_tok // chunk, chunk_body, 0)
    chunk_wait()


def _dispatch(pos, cnt, hn, cap, tg):
    T, D = hn.shape
    return pl.pallas_call(
        functools.partial(_dispatch_kernel, n_tok=T, chunk=256, cap=cap, tg=tg),
        grid_spec=pltpu.PrefetchScalarGridSpec(
            num_scalar_prefetch=2,
            grid=(1,),
            in_specs=[pl.BlockSpec(memory_space=pl.ANY)],
            out_specs=pl.BlockSpec(memory_space=pl.ANY),
            scratch_shapes=[pltpu.VMEM((tg, D), F32), pltpu.SemaphoreType.DMA, pltpu.SemaphoreType.DMA]),
        out_shape=jax.ShapeDtypeStruct((N_EXPERTS * cap, D), F32),
        compiler_params=_cparams(("arbitrary",)),
        name="moe_dispatch",
    )(pos, cnt, hn)


def _moe_group_kernel(blk_ref, exp_ref, fa_ref, fb_ref, nact_ref, x_ref, w1_ref, w3_ref, w2_ref, o_ref,
                      h_ref, acc_ref):
    t = pl.program_id(0)
    f = pl.program_id(1)

    @pl.when(t < nact_ref[0])
    def _():
        @pl.when(f == 0)
        def _():
            h_ref[...] = x_ref[...].astype(BF16)

        h = h_ref[...]
        a = _dot(h, w1_ref[0])
        b = _dot(h, w3_ref[0])
        y = _dot(((a * _sigmoid(a)) * b).astype(BF16), w2_ref[0])

        @pl.when(f == 0)
        def _():
            acc_ref[...] = y

        @pl.when(f == 1)
        def _():
            o_ref[...] = acc_ref[...] + y


def _moe_group(blk, exp, fa, fb, nact, xs, w1, w3, w2, tg):
    R, D = xs.shape
    F = w1.shape[2]
    tf = F // 2
    fsel = lambda f, fa, fb, t: jnp.where(f == 0, fa[t], fb[t])
    return pl.pallas_call(
        _moe_group_kernel,
        grid_spec=pltpu.PrefetchScalarGridSpec(
            num_scalar_prefetch=5,
            grid=(blk.shape[0], 2),
            in_specs=[pl.BlockSpec((tg, D), lambda t, f, blk, exp, fa, fb, na: (blk[t], 0)),
                      pl.BlockSpec((1, D, tf), lambda t, f, blk, exp, fa, fb, na: (exp[t], 0, fsel(f, fa, fb, t))),
                      pl.BlockSpec((1, D, tf), lambda t, f, blk, exp, fa, fb, na: (exp[t], 0, fsel(f, fa, fb, t))),
                      pl.BlockSpec((1, tf, D), lambda t, f, blk, exp, fa, fb, na: (exp[t], fsel(f, fa, fb, t), 0))],
            out_specs=pl.BlockSpec((tg, D), lambda t, f, blk, exp, fa, fb, na: (blk[t], 0)),
            scratch_shapes=[pltpu.VMEM((tg, D), BF16), pltpu.VMEM((tg, D), F32)]),
        out_shape=jax.ShapeDtypeStruct((R, D), F32),
        compiler_params=_cparams(("arbitrary", "arbitrary")),
        name="moe_experts",
    )(blk, exp, fa, fb, nact, xs, w1, w3, w2)


def _combine_kernel(pos_ref, x_ref, mod_ref, meta_ref, fn_ref, y_ref, o_ref, buf_ref, sem, *, n_tok, tc):
    i = pl.program_id(0)
    n = pl.num_programs(0)

    def issue(tile, slot):
        def body(t, c):
            tok = tile * tc + t
            pltpu.make_async_copy(y_ref.at[pl.ds(pos_ref[tok], 1)], buf_ref.at[slot, 0, pl.ds(t, 1)],
                                  sem.at[slot]).start()
            pltpu.make_async_copy(y_ref.at[pl.ds(pos_ref[n_tok + tok], 1)], buf_ref.at[slot, 1, pl.ds(t, 1)],
                                  sem.at[slot]).start()
            return c

        lax.fori_loop(0, tc, body, 0, unroll=8)

    @pl.when(i == 0)
    def _():
        issue(0, 0)

    slot = i % 2

    @pl.when(i + 1 < n)
    def _():
        issue(i + 1, 1 - slot)

    pltpu.make_async_copy(buf_ref.at[slot], buf_ref.at[slot], sem.at[slot]).wait()
    meta = meta_ref[...]
    mix = meta[:, 2:3] * buf_ref[slot, 0] + meta[:, 3:4] * buf_ref[slot, 1]
    o_ref[...] = _rms(x_ref[...] + mod_ref[0][5:6] * mix, fn_ref[...])


def _combine(pos, x, mod, meta, fin, y, seq_len, tc):
    T, D = x.shape
    per_batch = seq_len // tc
    return pl.pallas_call(
        functools.partial(_combine_kernel, n_tok=T, tc=tc),
        grid_spec=pltpu.PrefetchScalarGridSpec(
            num_scalar_prefetch=1,
            grid=(T // tc,),
            in_specs=[pl.BlockSpec((tc, D), lambda i, pos: (i, 0)),
                      pl.BlockSpec((1, 6, D), lambda i, pos: (i // per_batch, 0, 0)),
                      pl.BlockSpec((tc, LANES), lambda i, pos: (i, 0)),
                      pl.BlockSpec((1, D), lambda i, pos: (0, 0)),
                      pl.BlockSpec(memory_space=pl.ANY)],
            out_specs=pl.BlockSpec((tc, D), lambda i, pos: (i, 0)),
            scratch_shapes=[pltpu.VMEM((2, 2, tc, D), F32), pltpu.SemaphoreType.DMA((2,))]),
        out_shape=jax.ShapeDtypeStruct((T, D), F32),
        compiler_params=_cparams(("arbitrary",)),
        name="moe_combine",
    )(pos, x, mod, meta, fin, y)


def _moe_tiles(cnt, n_tiles, tg, cap):
    per = (cnt + tg - 1) // tg
    cum = jnp.cumsum(per)
    nact = cum[-1]
    t = jnp.arange(n_tiles, dtype=jnp.int32)
    tt = jnp.minimum(t, nact - 1)
    exp = jnp.minimum(jnp.sum((tt[:, None] >= cum[None, :]).astype(jnp.int32), axis=1), N_EXPERTS - 1)
    blk = exp * (cap // tg) + tt - (cum - per)[exp]
    odd = tt % 2
    fa = jnp.where(t < nact, odd, 1 - odd)
    fb = 1 - odd
    i32 = lambda v: v.astype(jnp.int32)
    return i32(blk), i32(exp), i32(fa), i32(fb), i32(nact.reshape(1))


def _ffn_kernel(x_ref, mod_ref, nw_ref, w1_ref, w3_ref, w2_ref, o_ref, h_ref, acc_ref):
    f = pl.program_id(2)

    @pl.when(f == 0)
    def _():
        m = mod_ref[0]
        h_ref[...] = _rms_mod(x_ref[0], nw_ref[...], m[4:5], m[3:4]).astype(BF16)
        acc_ref[...] = jnp.zeros_like(acc_ref)

    h = h_ref[...]
    a = _dot(h, w1_ref[...])
    b = _dot(h, w3_ref[...])
    acc_ref[...] += _dot(((a * _sigmoid(a)) * b).astype(BF16), w2_ref[...])

    @pl.when(f == pl.num_programs(2) - 1)
    def _():
        o_ref[0] = x_ref[0] + mod_ref[0][5:6] * acc_ref[...]


def _ffn(x, mod, mod_per_batch, nw, w1, w3, w2, tm, tf):
    B, S, D = x.shape
    F = w1.shape[1]
    mod_map = (lambda b, i, f: (b, 0, 0)) if mod_per_batch else (lambda b, i, f: (0, 0, 0))
    return pl.pallas_call(
        _ffn_kernel,
        grid=(B, S // tm, F // tf),
        in_specs=[pl.BlockSpec((1, tm, D), lambda b, i, f: (b, i, 0)),
                  pl.BlockSpec((1, 6, D), mod_map),
                  pl.BlockSpec((1, D), lambda b, i, f: (0, 0)),
                  pl.BlockSpec((D, tf), lambda b, i, f: (0, f)),
                  pl.BlockSpec((D, tf), lambda b, i, f: (0, f)),
                  pl.BlockSpec((tf, D), lambda b, i, f: (f, 0))],
        out_specs=pl.BlockSpec((1, tm, D), lambda b, i, f: (b, i, 0)),
        out_shape=jax.ShapeDtypeStruct((B, S, D), F32),
        scratch_shapes=[pltpu.VMEM((tm, D), BF16), pltpu.VMEM((tm, D), F32)],
        compiler_params=_cparams(("parallel", "parallel", "arbitrary")),
        name="ffn",
    )(x, mod, nw, w1, w3, w2)


def _proj_odd_kernel(x_ref, mod_ref, nw_ref, w_ref, gate_ref, rec_ref):
    m = mod_ref[0]
    h = _rms_mod(x_ref[0], nw_ref[...], m[1:2], m[0:1]).astype(BF16)
    gate_ref[0] = _gelu(_dot(h, w_ref[:, 0:D_RNN]))
    rec_ref[0] = _dot(h, w_ref[:, D_RNN:2 * D_RNN])


def _proj_odd(x, mod, mod_per_batch, nw, w, tm):
    B, S, D = x.shape
    mod_map = (lambda b, i: (b, 0, 0)) if mod_per_batch else (lambda b, i: (0, 0, 0))
    tok = lambda width: pl.BlockSpec((1, tm, width), lambda b, i: (b, i, 0))
    return pl.pallas_call(
        _proj_odd_kernel,
        grid=(B, S // tm),
        in_specs=[tok(D),
                  pl.BlockSpec((1, 6, D), mod_map),
                  pl.BlockSpec((1, D), lambda b, i: (0, 0)),
                  pl.BlockSpec((D, 2 * D_RNN), lambda b, i: (0, 0))],
        out_specs=[tok(D_RNN), tok(D_RNN)],
        out_shape=[jax.ShapeDtypeStruct((B, S, D_RNN), F32), jax.ShapeDtypeStruct((B, S, D_RNN), F32)],
        compiler_params=_cparams(("parallel", "parallel")),
        name="proj_odd",
    )(x, mod, nw, w)


def _scan8(a, b, h, row, reverse):
    for s in (1, 2, 4):
        if reverse:
            a_s, b_s, live = pltpu.roll(a, SUBLANES - s, 0), pltpu.roll(b, SUBLANES - s, 0), row < SUBLANES - s
        else:
            a_s, b_s, live = pltpu.roll(a, s, 0), pltpu.roll(b, s, 0), row >= s
        b = jnp.where(live, a * b_s + b, b)
        a = jnp.where(live, a * a_s, a)
    hr = a * h + b
    return hr, (hr[0:1] if reverse else hr[SUBLANES - 1:SUBLANES])


def _lru_kernel(rec_ref, recc_ref, cw_ref, cb_ref, wa_ref, ba_ref, wx_ref, bx_ref, lam_ref,
                s_ref, pad_ref, a_ref, b_ref, cpad_ref, ca_ref, cbb_ref, *, tile):
    S = rec_ref.shape[1]
    L = recc_ref.shape[1]
    cw = cw_ref[...]
    cb = cb_ref[...]
    lam = lam_ref[...]
    sp = jnp.maximum(-lam, 0.0) + jnp.log1p(jnp.exp(-jnp.abs(lam)))
    zeros8 = jnp.zeros((SUBLANES, LANES), F32)

    def coefficients(src_ref, dst_a, dst_b, n_rows, t):
        pad = cpad_ref if src_ref is recc_ref else pad_ref
        pad[0:SUBLANES, :] = zeros8
        pad[SUBLANES + n_rows:2 * SUBLANES + n_rows, :] = zeros8

        def copy(j, carry):
            r0 = pl.multiple_of(j * t, t)
            pad[pl.ds(SUBLANES + r0, t), :] = src_ref[0, pl.ds(r0, t), :]
            return carry

        lax.fori_loop(0, n_rows // t, copy, 0)

        def body(j, carry):
            r0 = pl.multiple_of(j * t, t)
            ext = pad[pl.ds(r0, t + 2 * SUBLANES), :]
            conv = cb
            for tap in range(4):
                conv = conv + cw[tap:tap + 1] * ext[SUBLANES - 2 + tap:SUBLANES - 2 + tap + t]
            cbf = conv.astype(BF16)
            for d in range(2):
                r = _sigmoid(_dot(cbf, wa_ref[d, 0]) + ba_ref[d:d + 1])
                gi = _sigmoid(_dot(cbf, wx_ref[d, 0]) + bx_ref[d:d + 1])
                log_a = -LRU_C * r * sp[d:d + 1]
                th = jnp.tanh(log_a)
                dst_a[d, pl.ds(r0, t), :] = jnp.exp(log_a)
                dst_b[d, pl.ds(r0, t), :] = jnp.sqrt(-2.0 * th / (1.0 - th)) * (gi * conv)
            return carry

        lax.fori_loop(0, n_rows // t, body, 0)

    row = lax.broadcasted_iota(jnp.int32, (SUBLANES, LANES), 0)
    h_zero = jnp.zeros((1, LANES), F32)

    coefficients(recc_ref, ca_ref, cbb_ref, L, L)
    nc = L // SUBLANES

    def ctx_body(j, carry):
        hf, hb = carry
        rf = pl.multiple_of(j * SUBLANES, SUBLANES)
        rb = pl.multiple_of((nc - 1 - j) * SUBLANES, SUBLANES)
        _, hf = _scan8(ca_ref[0, pl.ds(rf, SUBLANES), :], cbb_ref[0, pl.ds(rf, SUBLANES), :], hf, row, False)
        _, hb = _scan8(ca_ref[1, pl.ds(rb, SUBLANES), :], cbb_ref[1, pl.ds(rb, SUBLANES), :], hb, row, True)
        return hf, hb

    h0f, h0b = lax.fori_loop(0, nc, ctx_body, (h_zero, h_zero))

    coefficients(rec_ref, a_ref, b_ref, S, tile)
    n = S // SUBLANES

    def lat_body(accumulate):
        def body(j, carry):
            hf, hb = carry
            rf = pl.multiple_of(j * SUBLANES, SUBLANES)
            rb = pl.multiple_of((n - 1 - j) * SUBLANES, SUBLANES)
            of, hf = _scan8(a_ref[0, pl.ds(rf, SUBLANES), :], b_ref[0, pl.ds(rf, SUBLANES), :], hf, row, False)
            ob, hb = _scan8(a_ref[1, pl.ds(rb, SUBLANES), :], b_ref[1, pl.ds(rb, SUBLANES), :], hb, row, True)
            if accumulate:
                s_ref[0, pl.ds(rf, SUBLANES), :] += of
                s_ref[0, pl.ds(rb, SUBLANES), :] += ob
            else:
                s_ref[0, pl.ds(rf, SUBLANES), :] = of
                s_ref[0, pl.ds(rb, SUBLANES), :] = ob
            return hf, hb
        return body

    mid = lax.fori_loop(0, n // 2, lat_body(False), (h0f, h0b), unroll=4)
    lax.fori_loop(n // 2, n, lat_body(True), mid, unroll=4)


def _lru(rec, rec_c, conv_w, conv_b, wa, ba, wx, bx, lam, tile):
    B, S, _ = rec.shape
    L = rec_c.shape[1]
    blk = lambda rows: pl.BlockSpec((1, rows, LRU_BLOCK), lambda b, j: (b, 0, j))
    vec = lambda rows: pl.BlockSpec((rows, LRU_BLOCK), lambda b, j: (0, j))
    wspec = pl.BlockSpec((2, 1, LRU_BLOCK, LRU_BLOCK), lambda b, j: (0, j, 0, 0))
    return pl.pallas_call(
        functools.partial(_lru_kernel, tile=tile),
        grid=(B, LRU_BLOCKS),
        in_specs=[blk(S), blk(L), vec(4), vec(1), wspec, vec(2), wspec, vec(2), vec(2)],
        out_specs=blk(S),
        out_shape=jax.ShapeDtypeStruct((B, S, D_RNN), F32),
        scratch_shapes=[pltpu.VMEM((S + 2 * SUBLANES, LRU_BLOCK), F32),
                        pltpu.VMEM((2, S, LRU_BLOCK), F32),
                        pltpu.VMEM((2, S, LRU_BLOCK), F32),
                        pltpu.VMEM((L + 2 * SUBLANES, LRU_BLOCK), F32),
                        pltpu.VMEM((2, L, LRU_BLOCK), F32),
                        pltpu.VMEM((2, L, LRU_BLOCK), F32)],
        compiler_params=_cparams(("parallel", "parallel")),
        name="lru_scan",
    )(rec, rec_c, conv_w, conv_b, wa, ba, wx, bx, lam)


def _lru_out_kernel(x_ref, mod_ref, gate_ref, s_ref, w_ref, o_ref):
    y = (gate_ref[0] * s_ref[0]).astype(BF16)
    o_ref[0] = x_ref[0] + mod_ref[0][2:3] * _dot(y, w_ref[...])


def _lru_out(x, mod, gate, s, w, tm):
    B, S, D = x.shape
    tok = lambda width: pl.BlockSpec((1, tm, width), lambda b, i: (b, i, 0))
    return pl.pallas_call(
        _lru_out_kernel,
        grid=(B, S // tm),
        in_specs=[tok(D), pl.BlockSpec((1, 6, D), lambda b, i: (b, 0, 0)), tok(D_RNN), tok(D_RNN),
                  pl.BlockSpec((D_RNN, D), lambda b, i: (0, 0))],
        out_specs=tok(D),
        out_shape=jax.ShapeDtypeStruct((B, S, D), F32),
        compiler_params=_cparams(("parallel", "parallel")),
        name="lru_out",
    )(x, mod, gate, s, w)


def _rope_tables(n_tok):
    rows = n_tok // GRID_W
    row = jnp.repeat(jnp.arange(rows, dtype=F32), GRID_W)
    col = jnp.tile(jnp.arange(GRID_W, dtype=F32), rows)
    freqs = ROPE_BASE ** (-jnp.arange(ROPE_FREQS, dtype=F32) / ROPE_FREQS)
    ar, ac = row[:, None] * freqs, col[:, None] * freqs
    cos = jnp.concatenate([jnp.cos(ar), jnp.cos(ar), jnp.cos(ac), jnp.cos(ac)], axis=-1)
    sin = jnp.concatenate([-jnp.sin(ar), jnp.sin(ar), -jnp.sin(ac), jnp.sin(ac)], axis=-1)
    return jnp.tile(cos, (1, LANES // HEAD_DIM)), jnp.tile(sin, (1, LANES // HEAD_DIM))


def kernel(x, c, ctx, c_ctx, ada_w_e, ada_b_e, norm1_e, norm2_e, w_in_e, sgu_w, sgu_b, attn_sink, w_out_e, ffn_w1, ffn_w3, ffn_w2, ada_w_o, ada_b_o, norm1_o, norm2_o, w_in_o, conv_w, conv_b, lru_wa, lru_ba, lru_wx, lru_bx, lru_lambda, w_out_o, router_w, moe_w1, moe_w3, moe_w2, final_norm):
    B, S, D = x.shape
    L = ctx.shape[1]
    cvec = jnp.concatenate([c, c_ctx[None], jnp.zeros((SUBLANES - B - 1, D), F32)], axis=0)
    mod_e = _ada_params(cvec, ada_w_e[0], ada_b_e[0])
    mod_o = _ada_params(cvec, ada_w_o[0], ada_b_o[0])
    lat_e, ctx_e = mod_e[0:B], mod_e[B:B + 1]
    lat_o, ctx_o = mod_o[0:B], mod_o[B:B + 1]
    bf = lambda t: t.astype(BF16)
    row = lambda t: t.reshape(1, -1)

    cos, sin = _rope_tables(S)
    cos_c, sin_c = jnp.ones((L, LANES), F32), jnp.zeros((L, LANES), F32)
    w_in = bf(w_in_e[0])
    n1, n2 = row(norm1_e[0]), row(norm2_e[0])
    uc, vc, qc, kc2, vc2 = _proj_even(ctx, ctx_e, False, n1, w_in, cos_c, sin_c, L)
    u, v, q, k2, v2 = _proj_even(x, lat_e, True, n1, w_in, cos, sin, 512)
    ws, bs_t, wout = bf(sgu_w[0]), sgu_b[0].T, bf(w_out_e[0])
    sink = attn_sink[0]
    x = _mixer_even(x, lat_e, True, u, v, q, k2, v2, kc2, vc2, ws, bs_t, sink, wout, 512, False)
    xc = _mixer_even(ctx, ctx_e, False, uc, vc, qc, kc2, vc2, kc2, vc2, ws, bs_t, sink, wout, L, True)
    w1, w3, w2 = bf(ffn_w1[0]), bf(ffn_w3[0]), bf(ffn_w2[0])
    x = _ffn(x, lat_e, True, n2, w1, w3, w2, 512, 1408)
    xc = _ffn(xc, ctx_e, False, n2, w1, w3, w2, L, 1408)

    w_in = bf(w_in_o[0])
    n1, n2 = row(norm1_o[0]), row(norm2_o[0])
    _, rec_c = _proj_odd(xc, ctx_o, False, n1, w_in, L)
    gate, rec = _proj_odd(x, lat_o, True, n1, w_in, 512)
    s = _lru(rec, rec_c, conv_w[0], row(conv_b[0]), bf(lru_wa[0]), lru_ba[0], bf(lru_wx[0]), lru_bx[0],
             lru_lambda[0], 512)
    x = _lru_out(x, lat_o, gate, s, bf(w_out_o[0]), 512)
    rw = jnp.pad(router_w[0], ((0, 0), (0, LANES - N_EXPERTS)))
    T = B * S
    tg = 512
    cap = T + tg
    hn, meta, pos4, cnt = _route(x, lat_o, n2, rw, 512, cap)
    pos = jnp.transpose(pos4[:, :, 0:2, :], (2, 0, 1, 3)).reshape(2 * T)
    cnt = cnt[0, 0:N_EXPERTS].astype(jnp.int32)
    xs = _dispatch(pos, cnt, hn.reshape(T, D), cap, tg)
    blk, exp, fa, fb, nact = _moe_tiles(cnt, 2 * T // tg + N_EXPERTS, tg, cap)
    y = _moe_group(blk, exp, fa, fb, nact, xs, bf(moe_w1[0]), bf(moe_w3[0]), bf(moe_w2[0]), tg)
    out = _combine(pos, x.reshape(T, D), lat_o, meta.reshape(T, LANES), row(final_norm), y, S, 256)
    return out.reshape(B, S, D)
```

```python
import functools

import jax
import jax.numpy as jnp
from jax import lax
from jax.experimental import pallas as pl
from jax.experimental.pallas import tpu as pltpu

F32 = jnp.float32
BF16 = jnp.bfloat16

D_MODEL = 1024
GRID_W = 64
EPS = 1e-6
NEG_INF = -1e30
CHUNK = 128
SGU_GROUPS = 4
SGU_WIDTH = 512
HEAD_DIM = 64
N_Q_HEADS = 8
N_KV_HEADS = 2
ATTN_WIDTH = 512
KV_WIDTH = 128
WINDOW = 128
ATTN_BLOCK = 128
ATTN_SCALE = HEAD_DIM ** -0.5
LOG2E = 1.4426950408889634
ROW_BLOCK = 32
ROPE_BASE = 10000.0
ROPE_FREQS = 16
IN_EVEN = 1792
D_RNN = 1280
LRU_BLOCKS = 10
LRU_BLOCK = 128
LRU_C = 8.0
D_FF = 2816
N_EXPERTS = 8
LANES = 128
SUBLANES = 8
VMEM_LIMIT = 56 * 1024 * 1024


def _cparams(sem):
    return pltpu.CompilerParams(dimension_semantics=sem, vmem_limit_bytes=VMEM_LIMIT)


def _dot(a, b):
    return jnp.dot(a, b, preferred_element_type=F32)


def _dot_nt(a, b):
    return lax.dot_general(a, b, (((1,), (1,)), ((), ())), preferred_element_type=F32)


def _gelu(x):
    return 0.5 * x * (1.0 + jnp.tanh(0.7978845608028654 * (x + 0.044715 * (x * x * x))))


def _sigmoid(x):
    return 0.5 * jnp.tanh(0.5 * x) + 0.5


def _rms(x, nw):
    return (x * lax.rsqrt(jnp.mean(x * x, axis=-1, keepdims=True) + EPS)) * nw


def _rms_mod(x, nw, scale, shift):
    return _rms(x, nw) * (1.0 + scale) + shift


def _ada_kernel(c_ref, w_ref, b_ref, o_ref):
    c = c_ref[...]
    act = c * _sigmoid(c)
    o_ref[...] = jnp.dot(act, w_ref[...], precision=lax.Precision.HIGHEST,
                         preferred_element_type=F32) + b_ref[...]


def _ada_params(cvec, w, b):
    n = w.shape[1]
    tn = 1536
    out = pl.pallas_call(
        _ada_kernel,
        grid=(n // tn,),
        in_specs=[pl.BlockSpec((SUBLANES, D_MODEL), lambda j: (0, 0)),
                  pl.BlockSpec((D_MODEL, tn), lambda j: (0, j)),
                  pl.BlockSpec((1, tn), lambda j: (0, j))],
        out_specs=pl.BlockSpec((SUBLANES, tn), lambda j: (0, j)),
        out_shape=jax.ShapeDtypeStruct((SUBLANES, n), F32),
        compiler_params=_cparams(("parallel",)),
        name="ada_params",
    )(cvec, w, b.reshape(1, n))
    return out.reshape(SUBLANES, 6, D_MODEL)


def _proj_even_kernel(x_ref, mod_ref, nw_ref, w_ref, cos_ref, sin_ref,
                      u_ref, v_ref, q_ref, k_ref, val_ref):
    m = mod_ref[0]
    h = _rms_mod(x_ref[0], nw_ref[...], m[1:2], m[0:1]).astype(BF16)
    u_ref[0] = _gelu(_dot(h, w_ref[:, 0:SGU_WIDTH]))
    v_ref[0] = _gelu(_dot(h, w_ref[:, SGU_WIDTH:2 * SGU_WIDTH]))
    cos = cos_ref[...]
    sin = sin_ref[...]
    lane = lax.broadcasted_iota(jnp.int32, cos.shape, 1)
    first_half = (lane % 32) < ROPE_FREQS

    def rope(t):
        partner = jnp.where(first_half, pltpu.roll(t, LANES - ROPE_FREQS, 1), pltpu.roll(t, ROPE_FREQS, 1))
        return t * cos + partner * sin

    q = _dot(h, w_ref[:, 2 * SGU_WIDTH:2 * SGU_WIDTH + ATTN_WIDTH]) * (ATTN_SCALE * LOG2E)
    for g in range(ATTN_WIDTH // LANES):
        q_ref[0, :, g * LANES:(g + 1) * LANES] = rope(q[:, g * LANES:(g + 1) * LANES]).astype(BF16)
    kv = _dot(h, w_ref[:, 2 * SGU_WIDTH + ATTN_WIDTH:IN_EVEN])
    k = rope(kv[:, 0:KV_WIDTH])
    val = kv[:, KV_WIDTH:2 * KV_WIDTH]
    k_ref[0, :, 0:LANES] = k.astype(BF16)
    k_ref[0, :, LANES:2 * LANES] = pltpu.roll(k, HEAD_DIM, 1).astype(BF16)
    val_ref[0, :, 0:LANES] = val.astype(BF16)
    val_ref[0, :, LANES:2 * LANES] = pltpu.roll(val, HEAD_DIM, 1).astype(BF16)


def _proj_even(x, mod, mod_per_batch, nw, w, cos, sin, tm):
    B, S, D = x.shape
    mod_map = (lambda b, i: (b, 0, 0)) if mod_per_batch else (lambda b, i: (0, 0, 0))
    tok = lambda width: pl.BlockSpec((1, tm, width), lambda b, i: (b, i, 0))
    return pl.pallas_call(
        _proj_even_kernel,
        grid=(B, S // tm),
        in_specs=[tok(D),
                  pl.BlockSpec((1, 6, D), mod_map),
                  pl.BlockSpec((1, D), lambda b, i: (0, 0)),
                  pl.BlockSpec((D, IN_EVEN), lambda b, i: (0, 0)),
                  pl.BlockSpec((tm, LANES), lambda b, i: (i, 0)),
                  pl.BlockSpec((tm, LANES), lambda b, i: (i, 0))],
        out_specs=[tok(SGU_WIDTH), tok(SGU_WIDTH), tok(ATTN_WIDTH), tok(2 * KV_WIDTH), tok(2 * KV_WIDTH)],
        out_shape=[jax.ShapeDtypeStruct((B, S, SGU_WIDTH), F32),
                   jax.ShapeDtypeStruct((B, S, SGU_WIDTH), F32),
                   jax.ShapeDtypeStruct((B, S, ATTN_WIDTH), BF16),
                   jax.ShapeDtypeStruct((B, S, 2 * KV_WIDTH), BF16),
                   jax.ShapeDtypeStruct((B, S, 2 * KV_WIDTH), BF16)],
        compiler_params=_cparams(("parallel", "parallel")),
        name="proj_even",
    )(x, mod, nw, w, cos, sin)


def _mixer_even_kernel(sink_ref, x_ref, mod_ref, u_ref, v_ref, q_ref, k_ref, val_ref, kc_ref, vc_ref,
                       ws_ref, bs_ref, wout_ref, bias_ref, o_ref, mix_ref, s_ref, p_ref, inv_ref,
                       *, seq_len, is_ctx):
    tq = x_ref.shape[1]
    n_chunks = tq // CHUNK
    i = pl.program_id(1)
    nk = kc_ref.shape[1] + (0 if is_ctx else 3 * ATTN_BLOCK)
    lane = lax.broadcasted_iota(jnp.int32, (1, LANES), 1)
    lo = lane < HEAD_DIM
    zero = jnp.zeros((), BF16)

    def halves(ref_slice, kh):
        nat, swp = ref_slice[:, 0:LANES], ref_slice[:, LANES:2 * LANES]
        if kh == 0:
            return jnp.where(lo, nat, zero), jnp.where(lo, zero, swp)
        return jnp.where(lo, swp, zero), jnp.where(lo, zero, nat)

    kc_all = kc_ref[0]
    vc_all = vc_ref[0]

    def chunk_body(c, carry):
        r0 = pl.multiple_of(c * CHUNK, CHUNK)
        rows = pl.ds(r0, CHUNK)
        vch = v_ref[0, rows, :]
        uch = u_ref[0, rows, :]
        for g in range(SGU_GROUPS):
            cols = slice(g * LANES, (g + 1) * LANES)
            vg = vch[:, cols]
            dev = vg - jnp.mean(vg, axis=-1, keepdims=True)
            vn = dev * lax.rsqrt(jnp.mean(dev * dev, axis=-1, keepdims=True) + EPS)
            mixed = _dot(ws_ref[g], vn.astype(BF16)) + bs_ref[:, g:g + 1]
            mix_ref[rows, cols] = (uch[:, cols] * mixed).astype(BF16)
        qch = q_ref[0, rows, :]
        if not is_ctx:
            blk = i * n_chunks + c
            n_blk = seq_len // ATTN_BLOCK
            start = pl.multiple_of(jnp.clip((blk - 1) * ATTN_BLOCK, 0, seq_len - 3 * ATTN_BLOCK), ATTN_BLOCK)
            k3 = k_ref[0, pl.ds(start, 3 * ATTN_BLOCK), :]
            v3 = val_ref[0, pl.ds(start, 3 * ATTN_BLOCK), :]
            case = jnp.where(blk == 0, 0, jnp.where(blk == n_blk - 1, 2, 1))
        for kh in range(N_KV_HEADS):
            kc_lo, kc_hi = halves(kc_all, kh)
            vc_lo, vc_hi = halves(vc_all, kh)
            if is_ctx:
                k_cat = jnp.concatenate([kc_lo, kc_hi], axis=0)
                v_cat = (vc_lo, vc_hi)
            else:
                k_lo, k_hi = halves(k3, kh)
                v_lo, v_hi = halves(v3, kh)
                k_cat = jnp.concatenate([k_lo, kc_lo, k_hi, kc_hi], axis=0)
                v_cat = (jnp.concatenate([v_lo, vc_lo], axis=0), jnp.concatenate([v_hi, vc_hi], axis=0))
            q2 = jnp.concatenate([qch[:, 2 * kh * LANES:(2 * kh + 1) * LANES],
                                  qch[:, (2 * kh + 1) * LANES:(2 * kh + 2) * LANES]], axis=0)
            s_ref[:, 0:2 * nk] = _dot_nt(q2, k_cat)
            for half in range(2):
                for rb in range(2 * ATTN_BLOCK // ROW_BLOCK):
                    rsl = slice(rb * ROW_BLOCK, (rb + 1) * ROW_BLOCK)
                    snk = sink_ref[2 * (2 * kh + rb * ROW_BLOCK // ATTN_BLOCK) + half]
                    s = s_ref[rsl, half * nk:(half + 1) * nk]
                    if not is_ctx:
                        qoff = (rb * ROW_BLOCK) % ATTN_BLOCK
                        s_loc = s[:, 0:3 * ATTN_BLOCK] + bias_ref[case, qoff:qoff + ROW_BLOCK, :]
                        s = jnp.concatenate([s_loc, s[:, 3 * ATTN_BLOCK:]], axis=1)
                    m = jnp.maximum(jnp.max(s, axis=-1, keepdims=True), snk)
                    p = jnp.exp2(s - m)
                    den = jnp.sum(p, axis=-1, keepdims=True) + jnp.exp2(snk - m)
                    p_ref[rsl, half * nk:(half + 1) * nk] = p.astype(BF16)
                    inv_ref[rsl, half * LANES:(half + 1) * LANES] = jnp.broadcast_to(1.0 / den, (ROW_BLOCK, LANES))
            o_lo = _dot(p_ref[:, 0:nk], v_cat[0])
            o_hi = _dot(p_ref[:, nk:2 * nk], v_cat[1])
            acc = (o_lo * inv_ref[:, 0:LANES] + o_hi * inv_ref[:, LANES:2 * LANES]).astype(BF16)
            for g in range(2):
                col = SGU_WIDTH + (2 * kh + g) * LANES
                mix_ref[rows, col:col + LANES] = acc[g * ATTN_BLOCK:(g + 1) * ATTN_BLOCK]
        return carry

    lax.fori_loop(0, n_chunks, chunk_body, 0)
    y = _dot(mix_ref[...], wout_ref[...])
    o_ref[0] = x_ref[0] + mod_ref[0][2:3] * y


def _window_bias():
    case = jnp.arange(3, dtype=jnp.int32)[:, None, None]
    qi = jnp.arange(ATTN_BLOCK, dtype=jnp.int32)[None, :, None]
    kj = jnp.arange(3 * ATTN_BLOCK, dtype=jnp.int32)[None, None, :]
    return jnp.where(jnp.abs(kj - case * ATTN_BLOCK - qi) <= WINDOW, 0.0, NEG_INF).astype(F32)


def _mixer_even(x, mod, mod_per_batch, u, v, q, k2, v2, kc2, vc2, ws, bs_t, sink, wout, tq, is_ctx):
    B, S, D = x.shape
    Sk = k2.shape[1]
    Lc = kc2.shape[1]
    nk = Lc + (0 if is_ctx else 3 * ATTN_BLOCK)
    mod_map = (lambda b, i: (b, 0, 0)) if mod_per_batch else (lambda b, i: (0, 0, 0))
    tok = lambda width: pl.BlockSpec((1, tq, width), lambda b, i: (b, i, 0))
    per_batch = lambda rows: pl.BlockSpec((1, rows, 2 * KV_WIDTH), lambda b, i: (b, 0, 0))
    return pl.pallas_call(
        functools.partial(_mixer_even_kernel, seq_len=S, is_ctx=is_ctx),
        grid=(B, S // tq),
        in_specs=[pl.BlockSpec(memory_space=pltpu.SMEM),
                  tok(D),
                  pl.BlockSpec((1, 6, D), mod_map),
                  tok(SGU_WIDTH), tok(SGU_WIDTH), tok(ATTN_WIDTH),
                  per_batch(Sk), per_batch(Sk), per_batch(Lc), per_batch(Lc),
                  pl.BlockSpec((SGU_GROUPS, CHUNK, CHUNK), lambda b, i: (0, 0, 0)),
                  pl.BlockSpec((CHUNK, SGU_GROUPS), lambda b, i: (0, 0)),
                  pl.BlockSpec((D, D), lambda b, i: (0, 0)),
                  pl.BlockSpec((3, ATTN_BLOCK, 3 * ATTN_BLOCK), lambda b, i: (0, 0, 0))],
        out_specs=tok(D),
        out_shape=jax.ShapeDtypeStruct((B, S, D), F32),
        scratch_shapes=[pltpu.VMEM((tq, D), BF16),
                        pltpu.VMEM((2 * ATTN_BLOCK, 2 * nk), F32),
                        pltpu.VMEM((2 * ATTN_BLOCK, 2 * nk), BF16),
                        pltpu.VMEM((2 * ATTN_BLOCK, 2 * LANES), F32)],
        compiler_params=_cparams(("parallel", "arbitrary")),
        name="mixer_ctx" if is_ctx else "mixer_even",
    )(sink, x, mod, u, v, q, k2, v2, kc2, vc2, ws, bs_t, wout, _window_bias())


def _route_kernel(x_ref, mod_ref, nw_ref, rw_ref, hn_ref, meta_ref, pos_ref, cnt_ref, base_ref, *, cap):
    @pl.when(jnp.logical_and(pl.program_id(0) == 0, pl.program_id(1) == 0))
    def _():
        base_ref[...] = jnp.zeros_like(base_ref)

    m = mod_ref[0]
    h = _rms_mod(x_ref[0], nw_ref[...], m[4:5], m[3:4])
    hn_ref[0] = h
    w = rw_ref[...]
    w_hi = w.astype(BF16)
    w_lo = (w - w_hi.astype(F32)).astype(BF16)
    h_hi = h.astype(BF16)
    h_lo = (h - h_hi.astype(F32)).astype(BF16)
    logits = _dot(h_hi, w_hi) + (_dot(h_lo, w_hi) + _dot(h_hi, w_lo))
    tm = logits.shape[0]
    lane = lax.broadcasted_iota(jnp.int32, logits.shape, 1)
    lg = jnp.where(lane < N_EXPERTS, logits, -jnp.inf)
    m1 = jnp.max(lg, axis=-1, keepdims=True)
    i1 = jnp.min(jnp.where(lg == m1, lane, LANES), axis=-1, keepdims=True)
    lg2 = jnp.where(lane == i1, -jnp.inf, lg)
    m2 = jnp.max(lg2, axis=-1, keepdims=True)
    i2 = jnp.min(jnp.where(lg2 == m2, lane, LANES), axis=-1, keepdims=True)
    e2 = jnp.exp(m2 - m1)
    den = 1.0 + e2
    hot = jnp.where(jnp.logical_or(lane == i1, lane == i2), 1.0, 0.0)
    r = lax.broadcasted_iota(jnp.int32, (tm, tm), 0)
    c = lax.broadcasted_iota(jnp.int32, (tm, tm), 1)
    before = jnp.where(r > c, 1.0, 0.0).astype(BF16)
    tot = base_ref[...] + _dot(before, hot.astype(BF16))
    rank1 = jnp.sum(jnp.where(lane == i1, tot, 0.0), axis=-1, keepdims=True)
    rank2 = jnp.sum(jnp.where(lane == i2, tot, 0.0), axis=-1, keepdims=True)
    pos1 = i1.astype(F32) * cap + rank1
    pos2 = i2.astype(F32) * cap + rank2
    meta = (jnp.where(lane == 0, pos1, 0.0) + jnp.where(lane == 1, pos2, 0.0)
            + jnp.where(lane == 2, 1.0 / den, 0.0) + jnp.where(lane == 3, e2 / den, 0.0))
    meta_ref[0] = meta
    pos_ref[0, 0] = meta.T[0:SUBLANES].astype(jnp.int32)
    base_ref[...] += jnp.sum(hot, axis=0, keepdims=True)
    cnt_ref[...] = jnp.broadcast_to(base_ref[...], cnt_ref.shape)


def _route(x, mod, nw, rw, tm, cap):
    B, S, D = x.shape
    return pl.pallas_call(
        functools.partial(_route_kernel, cap=float(cap)),
        grid=(B, S // tm),
        in_specs=[pl.BlockSpec((1, tm, D), lambda b, i: (b, i, 0)),
                  pl.BlockSpec((1, 6, D), lambda b, i: (b, 0, 0)),
                  pl.BlockSpec((1, D), lambda b, i: (0, 0)),
                  pl.BlockSpec((D, LANES), lambda b, i: (0, 0))],
        out_specs=[pl.BlockSpec((1, tm, D), lambda b, i: (b, i, 0)),
                   pl.BlockSpec((1, tm, LANES), lambda b, i: (b, i, 0)),
                   pl.BlockSpec((1, 1, SUBLANES, tm), lambda b, i: (b, i, 0, 0)),
                   pl.BlockSpec((SUBLANES, LANES), lambda b, i: (0, 0))],
        out_shape=[jax.ShapeDtypeStruct((B, S, D), F32),
                   jax.ShapeDtypeStruct((B, S, LANES), F32),
                   jax.ShapeDtypeStruct((B, S // tm, SUBLANES, tm), jnp.int32),
                   jax.ShapeDtypeStruct((SUBLANES, LANES), F32)],
        scratch_shapes=[pltpu.VMEM((1, LANES), F32)],
        compiler_params=_cparams(("arbitrary", "arbitrary")),
        name="moe_route",
    )(x, mod, nw, rw)


def _dispatch_kernel(pos_ref, cnt_ref, hn_ref, xs_ref, zero_ref, sem, zsem, *, n_tok, cap, tg):
    i = pl.program_id(0)
    td = hn_ref.shape[0]

    @pl.when(i == 0)
    def _():
        zero_ref[...] = jnp.zeros_like(zero_ref)

        def tail_copy(e):
            start = pl.multiple_of(e * cap + (cnt_ref[e] // SUBLANES) * SUBLANES, SUBLANES)
            return pltpu.make_async_copy(zero_ref, xs_ref.at[pl.ds(start, tg)], zsem)

        for e in range(N_EXPERTS):
            tail_copy(e).start()
        for e in range(N_EXPERTS):
            tail_copy(e).wait()

    def tok_body(t, c):
        tok = i * td + t
        pltpu.make_async_copy(hn_ref.at[pl.ds(t, 1)], xs_ref.at[pl.ds(pos_ref[tok], 1)], sem).start()
        pltpu.make_async_copy(hn_ref.at[pl.ds(t, 1)], xs_ref.at[pl.ds(pos_ref[n_tok + tok], 1)], sem).start()
        return c

    lax.fori_loop(0, td, tok_body, 0, unroll=8)
    for _ in range(2):
        pltpu.make_async_copy(hn_ref, hn_ref, sem).wait()


def _dispatch(pos, cnt, hn, cap, tg, td):
    T, D = hn.shape
    return pl.pallas_call(
        functools.partial(_dispatch_kernel, n_tok=T, cap=cap, tg=tg),
        grid_spec=pltpu.PrefetchScalarGridSpec(
            num_scalar_prefetch=2,
            grid=(T // td,),
            in_specs=[pl.BlockSpec((td, D), lambda i, pos, cnt: (i, 0))],
            out_specs=pl.BlockSpec(memory_space=pl.ANY),
            scratch_shapes=[pltpu.VMEM((tg, D), F32), pltpu.SemaphoreType.DMA, pltpu.SemaphoreType.DMA]),
        out_shape=jax.ShapeDtypeStruct((N_EXPERTS * cap, D), F32),
        compiler_params=_cparams(("arbitrary",)),
        name="moe_dispatch",
    )(pos, cnt, hn)


def _moe_group_kernel(blk_ref, exp_ref, fa_ref, fb_ref, nact_ref, x_ref, w1_ref, w3_ref, w2_ref, o_ref,
                      h_ref, acc_ref):
    t = pl.program_id(0)
    f = pl.program_id(1)

    @pl.when(t < nact_ref[0])
    def _():
        @pl.when(f == 0)
        def _():
            h_ref[...] = x_ref[...].astype(BF16)

        h = h_ref[...]
        a = _dot(h, w1_ref[0])
        b = _dot(h, w3_ref[0])
        y = _dot(((a * _sigmoid(a)) * b).astype(BF16), w2_ref[0])

        @pl.when(f == 0)
        def _():
            acc_ref[...] = y

        @pl.when(f == 1)
        def _():
            o_ref[...] = acc_ref[...] + y


def _moe_group(blk, exp, fa, fb, nact, xs, w1, w3, w2, tg):
    R, D = xs.shape
    F = w1.shape[2]
    tf = F // 2
    fsel = lambda f, fa, fb, t: jnp.where(f == 0, fa[t], fb[t])
    return pl.pallas_call(
        _moe_group_kernel,
        grid_spec=pltpu.PrefetchScalarGridSpec(
            num_scalar_prefetch=5,
            grid=(blk.shape[0], 2),
            in_specs=[pl.BlockSpec((tg, D), lambda t, f, blk, exp, fa, fb, na: (blk[t], 0)),
                      pl.BlockSpec((1, D, tf), lambda t, f, blk, exp, fa, fb, na: (exp[t], 0, fsel(f, fa, fb, t))),
                      pl.BlockSpec((1, D, tf), lambda t, f, blk, exp, fa, fb, na: (exp[t], 0, fsel(f, fa, fb, t))),
                      pl.BlockSpec((1, tf, D), lambda t, f, blk, exp, fa, fb, na: (exp[t], fsel(f, fa, fb, t), 0))],
            out_specs=pl.BlockSpec((tg, D), lambda t, f, blk, exp, fa, fb, na: (blk[t], 0)),
            scratch_shapes=[pltpu.VMEM((tg, D), BF16), pltpu.VMEM((tg, D), F32)]),
        out_shape=jax.ShapeDtypeStruct((R, D), F32),
        compiler_params=_cparams(("arbitrary", "arbitrary")),
        name="moe_experts",
    )(blk, exp, fa, fb, nact, xs, w1, w3, w2)


def _combine_kernel(pos_ref, x_ref, mod_ref, meta_ref, fn_ref, y_ref, o_ref, buf_ref, sem, *, n_tok, tc):
    i = pl.program_id(0)
    n = pl.num_programs(0)

    def issue(tile, slot):
        def body(t, c):
            tok = tile * tc + t
            pltpu.make_async_copy(y_ref.at[pl.ds(pos_ref[tok], 1)], buf_ref.at[slot, 0, pl.ds(t, 1)],
                                  sem.at[slot]).start()
            pltpu.make_async_copy(y_ref.at[pl.ds(pos_ref[n_tok + tok], 1)], buf_ref.at[slot, 1, pl.ds(t, 1)],
                                  sem.at[slot]).start()
            return c

        lax.fori_loop(0, tc, body, 0, unroll=8)

    @pl.when(i == 0)
    def _():
        issue(0, 0)

    slot = i % 2

    @pl.when(i + 1 < n)
    def _():
        issue(i + 1, 1 - slot)

    pltpu.make_async_copy(buf_ref.at[slot], buf_ref.at[slot], sem.at[slot]).wait()
    meta = meta_ref[...]
    mix = meta[:, 2:3] * buf_ref[slot, 0] + meta[:, 3:4] * buf_ref[slot, 1]
    o_ref[...] = _rms(x_ref[...] + mod_ref[0][5:6] * mix, fn_ref[...])


def _combine(pos, x, mod, meta, fin, y, seq_len, tc):
    T, D = x.shape
    per_batch = seq_len // tc
    return pl.pallas_call(
        functools.partial(_combine_kernel, n_tok=T, tc=tc),
        grid_spec=pltpu.PrefetchScalarGridSpec(
            num_scalar_prefetch=1,
            grid=(T // tc,),
            in_specs=[pl.BlockSpec((tc, D), lambda i, pos: (i, 0)),
                      pl.BlockSpec((1, 6, D), lambda i, pos: (i // per_batch, 0, 0)),
                      pl.BlockSpec((tc, LANES), lambda i, pos: (i, 0)),
                      pl.BlockSpec((1, D), lambda i, pos: (0, 0)),
                      pl.BlockSpec(memory_space=pl.ANY)],
            out_specs=pl.BlockSpec((tc, D), lambda i, pos: (i, 0)),
            scratch_shapes=[pltpu.VMEM((2, 2, tc, D), F32), pltpu.SemaphoreType.DMA((2,))]),
        out_shape=jax.ShapeDtypeStruct((T, D), F32),
        compiler_params=_cparams(("arbitrary",)),
        name="moe_combine",
    )(pos, x, mod, meta, fin, y)


def _moe_tiles(cnt, n_tiles, tg, cap):
    per = (cnt + tg - 1) // tg
    cum = jnp.cumsum(per)
    nact = cum[-1]
    t = jnp.arange(n_tiles, dtype=jnp.int32)
    tt = jnp.minimum(t, nact - 1)
    exp = jnp.minimum(jnp.sum((tt[:, None] >= cum[None, :]).astype(jnp.int32), axis=1), N_EXPERTS - 1)
    blk = exp * (cap // tg) + tt - (cum - per)[exp]
    odd = tt % 2
    fa = jnp.where(t < nact, odd, 1 - odd)
    fb = 1 - odd
    i32 = lambda v: v.astype(jnp.int32)
    return i32(blk), i32(exp), i32(fa), i32(fb), i32(nact.reshape(1))


def _ffn_kernel(x_ref, mod_ref, nw_ref, w1_ref, w3_ref, w2_ref, o_ref, h_ref, acc_ref):
    f = pl.program_id(2)

    @pl.when(f == 0)
    def _():
        m = mod_ref[0]
        h_ref[...] = _rms_mod(x_ref[0], nw_ref[...], m[4:5], m[3:4]).astype(BF16)
        acc_ref[...] = jnp.zeros_like(acc_ref)

    h = h_ref[...]
    a = _dot(h, w1_ref[...])
    b = _dot(h, w3_ref[...])
    acc_ref[...] += _dot(((a * _sigmoid(a)) * b).astype(BF16), w2_ref[...])

    @pl.when(f == pl.num_programs(2) - 1)
    def _():
        o_ref[0] = x_ref[0] + mod_ref[0][5:6] * acc_ref[...]


def _ffn(x, mod, mod_per_batch, nw, w1, w3, w2, tm, tf):
    B, S, D = x.shape
    F = w1.shape[1]
    mod_map = (lambda b, i, f: (b, 0, 0)) if mod_per_batch else (lambda b, i, f: (0, 0, 0))
    return pl.pallas_call(
        _ffn_kernel,
        grid=(B, S // tm, F // tf),
        in_specs=[pl.BlockSpec((1, tm, D), lambda b, i, f: (b, i, 0)),
                  pl.BlockSpec((1, 6, D), mod_map),
                  pl.BlockSpec((1, D), lambda b, i, f: (0, 0)),
                  pl.BlockSpec((D, tf), lambda b, i, f: (0, f)),
                  pl.BlockSpec((D, tf), lambda b, i, f: (0, f)),
                  pl.BlockSpec((tf, D), lambda b, i, f: (f, 0))],
        out_specs=pl.BlockSpec((1, tm, D), lambda b, i, f: (b, i, 0)),
        out_shape=jax.ShapeDtypeStruct((B, S, D), F32),
        scratch_shapes=[pltpu.VMEM((tm, D), BF16), pltpu.VMEM((tm, D), F32)],
        compiler_params=_cparams(("parallel", "parallel", "arbitrary")),
        name="ffn",
    )(x, mod, nw, w1, w3, w2)


def _proj_odd_kernel(x_ref, mod_ref, nw_ref, w_ref, gate_ref, rec_ref):
    m = mod_ref[0]
    h = _rms_mod(x_ref[0], nw_ref[...], m[1:2], m[0:1]).astype(BF16)
    gate_ref[0] = _gelu(_dot(h, w_ref[:, 0:D_RNN]))
    rec_ref[0] = _dot(h, w_ref[:, D_RNN:2 * D_RNN])


def _proj_odd(x, mod, mod_per_batch, nw, w, tm):
    B, S, D = x.shape
    mod_map = (lambda b, i: (b, 0, 0)) if mod_per_batch else (lambda b, i: (0, 0, 0))
    tok = lambda width: pl.BlockSpec((1, tm, width), lambda b, i: (b, i, 0))
    return pl.pallas_call(
        _proj_odd_kernel,
        grid=(B, S // tm),
        in_specs=[tok(D),
                  pl.BlockSpec((1, 6, D), mod_map),
                  pl.BlockSpec((1, D), lambda b, i: (0, 0)),
                  pl.BlockSpec((D, 2 * D_RNN), lambda b, i: (0, 0))],
        out_specs=[tok(D_RNN), tok(D_RNN)],
        out_shape=[jax.ShapeDtypeStruct((B, S, D_RNN), F32), jax.ShapeDtypeStruct((B, S, D_RNN), F32)],
        compiler_params=_cparams(("parallel", "parallel")),
        name="proj_odd",
    )(x, mod, nw, w)


def _scan8(a, b, h, row, reverse):
    for s in (1, 2, 4):
        if reverse:
            a_s, b_s, live = pltpu.roll(a, SUBLANES - s, 0), pltpu.roll(b, SUBLANES - s, 0), row < SUBLANES - s
        else:
            a_s, b_s, live = pltpu.roll(a, s, 0), pltpu.roll(b, s, 0), row >= s
        b = jnp.where(live, a * b_s + b, b)
        a = jnp.where(live, a * a_s, a)
    hr = a * h + b
    return hr, (hr[0:1] if reverse else hr[SUBLANES - 1:SUBLANES])


def _lru_kernel(rec_ref, recc_ref, cw_ref, cb_ref, wa_ref, ba_ref, wx_ref, bx_ref, lam_ref,
                s_ref, pad_ref, a_ref, b_ref, cpad_ref, ca_ref, cbb_ref, *, tile):
    S = rec_ref.shape[1]
    L = recc_ref.shape[1]
    cw = cw_ref[...]
    cb = cb_ref[...]
    lam = lam_ref[...]
    sp = jnp.maximum(-lam, 0.0) + jnp.log1p(jnp.exp(-jnp.abs(lam)))
    zeros8 = jnp.zeros((SUBLANES, LANES), F32)

    def coefficients(src_ref, dst_a, dst_b, n_rows, t):
        pad = cpad_ref if src_ref is recc_ref else pad_ref
        pad[0:SUBLANES, :] = zeros8
        pad[SUBLANES + n_rows:2 * SUBLANES + n_rows, :] = zeros8

        def copy(j, carry):
            r0 = pl.multiple_of(j * t, t)
            pad[pl.ds(SUBLANES + r0, t), :] = src_ref[0, pl.ds(r0, t), :]
            return carry

        lax.fori_loop(0, n_rows // t, copy, 0)

        def body(j, carry):
            r0 = pl.multiple_of(j * t, t)
            ext = pad[pl.ds(r0, t + 2 * SUBLANES), :]
            conv = cb
            for tap in range(4):
                conv = conv + cw[tap:tap + 1] * ext[SUBLANES - 2 + tap:SUBLANES - 2 + tap + t]
            cbf = conv.astype(BF16)
            for d in range(2):
                r = _sigmoid(_dot(cbf, wa_ref[d, 0]) + ba_ref[d:d + 1])
                gi = _sigmoid(_dot(cbf, wx_ref[d, 0]) + bx_ref[d:d + 1])
                log_a = -LRU_C * r * sp[d:d + 1]
                a = jnp.exp(log_a)
                y = -jnp.tanh(0.5 * log_a)
                root = jnp.where(y > 0.0, y * lax.rsqrt(y), 0.0)
                dst_a[d, pl.ds(r0, t), :] = a
                dst_b[d, pl.ds(r0, t), :] = (root * (1.0 + a)) * (gi * conv)
            return carry

        lax.fori_loop(0, n_rows // t, body, 0)

    row = lax.broadcasted_iota(jnp.int32, (SUBLANES, LANES), 0)
    h_zero = jnp.zeros((1, LANES), F32)

    coefficients(recc_ref, ca_ref, cbb_ref, L, L)
    nc = L // SUBLANES

    def ctx_body(j, carry):
        hf, hb = carry
        rf = pl.multiple_of(j * SUBLANES, SUBLANES)
        rb = pl.multiple_of((nc - 1 - j) * SUBLANES, SUBLANES)
        _, hf = _scan8(ca_ref[0, pl.ds(rf, SUBLANES), :], cbb_ref[0, pl.ds(rf, SUBLANES), :], hf, row, False)
        _, hb = _scan8(ca_ref[1, pl.ds(rb, SUBLANES), :], cbb_ref[1, pl.ds(rb, SUBLANES), :], hb, row, True)
        return hf, hb

    h0f, h0b = lax.fori_loop(0, nc, ctx_body, (h_zero, h_zero))

    coefficients(rec_ref, a_ref, b_ref, S, tile)
    n = S // SUBLANES

    def lat_body(accumulate):
        def body(j, carry):
            hf, hb = carry
            rf = pl.multiple_of(j * SUBLANES, SUBLANES)
            rb = pl.multiple_of((n - 1 - j) * SUBLANES, SUBLANES)
            of, hf = _scan8(a_ref[0, pl.ds(rf, SUBLANES), :], b_ref[0, pl.ds(rf, SUBLANES), :], hf, row, False)
            ob, hb = _scan8(a_ref[1, pl.ds(rb, SUBLANES), :], b_ref[1, pl.ds(rb, SUBLANES), :], hb, row, True)
            if accumulate:
                s_ref[0, pl.ds(rf, SUBLANES), :] += of
                s_ref[0, pl.ds(rb, SUBLANES), :] += ob
            else:
                s_ref[0, pl.ds(rf, SUBLANES), :] = of
                s_ref[0, pl.ds(rb, SUBLANES), :] = ob
            return hf, hb
        return body

    mid = lax.fori_loop(0, n // 2, lat_body(False), (h0f, h0b), unroll=4)
    lax.fori_loop(n // 2, n, lat_body(True), mid, unroll=4)


def _lru(rec, rec_c, conv_w, conv_b, wa, ba, wx, bx, lam, tile):
    B, S, _ = rec.shape
    L = rec_c.shape[1]
    blk = lambda rows: pl.BlockSpec((1, rows, LRU_BLOCK), lambda b, j: (b, 0, j))
    vec = lambda rows: pl.BlockSpec((rows, LRU_BLOCK), lambda b, j: (0, j))
    wspec = pl.BlockSpec((2, 1, LRU_BLOCK, LRU_BLOCK), lambda b, j: (0, j, 0, 0))
    return pl.pallas_call(
        functools.partial(_lru_kernel, tile=tile),
        grid=(B, LRU_BLOCKS),
        in_specs=[blk(S), blk(L), vec(4), vec(1), wspec, vec(2), wspec, vec(2), vec(2)],
        out_specs=blk(S),
        out_shape=jax.ShapeDtypeStruct((B, S, D_RNN), F32),
        scratch_shapes=[pltpu.VMEM((S + 2 * SUBLANES, LRU_BLOCK), F32),
                        pltpu.VMEM((2, S, LRU_BLOCK), F32),
                        pltpu.VMEM((2, S, LRU_BLOCK), F32),
                        pltpu.VMEM((L + 2 * SUBLANES, LRU_BLOCK), F32),
                        pltpu.VMEM((2, L, LRU_BLOCK), F32),
                        pltpu.VMEM((2, L, LRU_BLOCK), F32)],
        compiler_params=_cparams(("parallel", "parallel")),
        name="lru_scan",
    )(rec, rec_c, conv_w, conv_b, wa, ba, wx, bx, lam)


def _lru_out_kernel(x_ref, mod_ref, gate_ref, s_ref, w_ref, o_ref):
    y = (gate_ref[0] * s_ref[0]).astype(BF16)
    o_ref[0] = x_ref[0] + mod_ref[0][2:3] * _dot(y, w_ref[...])


def _lru_out(x, mod, gate, s, w, tm):
    B, S, D = x.shape
    tok = lambda width: pl.BlockSpec((1, tm, width), lambda b, i: (b, i, 0))
    return pl.pallas_call(
        _lru_out_kernel,
        grid=(B, S // tm),
        in_specs=[tok(D), pl.BlockSpec((1, 6, D), lambda b, i: (b, 0, 0)), tok(D_RNN), tok(D_RNN),
                  pl.BlockSpec((D_RNN, D), lambda b, i: (0, 0))],
        out_specs=tok(D),
        out_shape=jax.ShapeDtypeStruct((B, S, D), F32),
        compiler_params=_cparams(("parallel", "parallel")),
        name="lru_out",
    )(x, mod, gate, s, w)


def _rope_tables(n_tok):
    rows = n_tok // GRID_W
    row = jnp.repeat(jnp.arange(rows, dtype=F32), GRID_W)
    col = jnp.tile(jnp.arange(GRID_W, dtype=F32), rows)
    freqs = ROPE_BASE ** (-jnp.arange(ROPE_FREQS, dtype=F32) / ROPE_FREQS)
    ar, ac = row[:, None] * freqs, col[:, None] * freqs
    cos = jnp.concatenate([jnp.cos(ar), jnp.cos(ar), jnp.cos(ac), jnp.cos(ac)], axis=-1)
    sin = jnp.concatenate([-jnp.sin(ar), jnp.sin(ar), -jnp.sin(ac), jnp.sin(ac)], axis=-1)
    return jnp.tile(cos, (1, LANES // HEAD_DIM)), jnp.tile(sin, (1, LANES // HEAD_DIM))


def kernel(x, c, ctx, c_ctx, ada_w_e, ada_b_e, norm1_e, norm2_e, w_in_e, sgu_w, sgu_b, attn_sink, w_out_e, ffn_w1, ffn_w3, ffn_w2, ada_w_o, ada_b_o, norm1_o, norm2_o, w_in_o, conv_w, conv_b, lru_wa, lru_ba, lru_wx, lru_bx, lru_lambda, w_out_o, router_w, moe_w1, moe_w3, moe_w2, final_norm):
    B, S, D = x.shape
    L = ctx.shape[1]
    cvec = jnp.concatenate([c, c_ctx[None], jnp.zeros((SUBLANES - B - 1, D), F32)], axis=0)
    mod_e = _ada_params(cvec, ada_w_e[0], ada_b_e[0])
    mod_o = _ada_params(cvec, ada_w_o[0], ada_b_o[0])
    lat_e, ctx_e = mod_e[0:B], mod_e[B:B + 1]
    lat_o, ctx_o = mod_o[0:B], mod_o[B:B + 1]
    bf = lambda t: t.astype(BF16)
    row = lambda t: t.reshape(1, -1)

    cos, sin = _rope_tables(S)
    cos_c, sin_c = jnp.ones((L, LANES), F32), jnp.zeros((L, LANES), F32)
    w_in = bf(w_in_e[0])
    n1, n2 = row(norm1_e[0]), row(norm2_e[0])
    uc, vc, qc, kc2, vc2 = _proj_even(ctx, ctx_e, False, n1, w_in, cos_c, sin_c, L)
    u, v, q, k2, v2 = _proj_even(x, lat_e, True, n1, w_in, cos, sin, 512)
    ws, bs_t, wout = bf(sgu_w[0]), sgu_b[0].T, bf(w_out_e[0])
    sink = attn_sink[0] * LOG2E
    x = _mixer_even(x, lat_e, True, u, v, q, k2, v2, kc2, vc2, ws, bs_t, sink, wout, 512, False)
    xc = _mixer_even(ctx, ctx_e, False, uc, vc, qc, kc2, vc2, kc2, vc2, ws, bs_t, sink, wout, L, True)
    w1, w3, w2 = bf(ffn_w1[0]), bf(ffn_w3[0]), bf(ffn_w2[0])
    x = _ffn(x, lat_e, True, n2, w1, w3, w2, 512, 1408)
    xc = _ffn(xc, ctx_e, False, n2, w1, w3, w2, L, 1408)

    w_in = bf(w_in_o[0])
    n1, n2 = row(norm1_o[0]), row(norm2_o[0])
    _, rec_c = _proj_odd(xc, ctx_o, False, n1, w_in, L)
    gate, rec = _proj_odd(x, lat_o, True, n1, w_in, 512)
    s = _lru(rec, rec_c, conv_w[0], row(conv_b[0]), bf(lru_wa[0]), lru_ba[0], bf(lru_wx[0]), lru_bx[0],
             lru_lambda[0], 512)
    x = _lru_out(x, lat_o, gate, s, bf(w_out_o[0]), 512)
    rw = jnp.pad(router_w[0], ((0, 0), (0, LANES - N_EXPERTS)))
    T = B * S
    tg = 512
    cap = T + tg
    hn, meta, pos4, cnt = _route(x, lat_o, n2, rw, 512, cap)
    pos = jnp.transpose(pos4[:, :, 0:2, :], (2, 0, 1, 3)).reshape(2 * T)
    cnt = cnt[0, 0:N_EXPERTS].astype(jnp.int32)
    xs = _dispatch(pos, cnt, hn.reshape(T, D), cap, tg, 512)
    blk, exp, fa, fb, nact = _moe_tiles(cnt, 2 * T // tg + N_EXPERTS, tg, cap)
    y = _moe_group(blk, exp, fa, fb, nact, xs, bf(moe_w1[0]), bf(moe_w3[0]), bf(moe_w2[0]), tg)
    out = _combine(pos, x.reshape(T, D), lat_o, meta.reshape(T, LANES), row(final_norm), y, S, 256)
    return out.reshape(B, S, D)
```

```python
import functools

import jax
import jax.numpy as jnp
from jax import lax
from jax.experimental import pallas as pl
from jax.experimental.pallas import tpu as pltpu

F32 = jnp.float32
BF16 = jnp.bfloat16

D_MODEL = 1024
GRID_W = 64
EPS = 1e-6
NEG_INF = -1e30
CHUNK = 128
SGU_GROUPS = 4
SGU_WIDTH = 512
HEAD_DIM = 64
N_Q_HEADS = 8
N_KV_HEADS = 2
ATTN_WIDTH = 512
KV_WIDTH = 128
WINDOW = 128
ATTN_BLOCK = 128
ATTN_SCALE = HEAD_DIM ** -0.5
LOG2E = 1.4426950408889634
ROW_BLOCK = 32
COMBINE_TILE = 256
COMBINE_CHUNK = 16
ROPE_BASE = 10000.0
ROPE_FREQS = 16
IN_EVEN = 1792
D_RNN = 1280
LRU_BLOCKS = 10
LRU_BLOCK = 128
LRU_C = 8.0
D_FF = 2816
N_EXPERTS = 8
LANES = 128
SUBLANES = 8
VMEM_LIMIT = 56 * 1024 * 1024


def _cparams(sem):
    return pltpu.CompilerParams(dimension_semantics=sem, vmem_limit_bytes=VMEM_LIMIT)


def _dot(a, b):
    return jnp.dot(a, b, preferred_element_type=F32)


def _dot_nt(a, b):
    return lax.dot_general(a, b, (((1,), (1,)), ((), ())), preferred_element_type=F32)


def _gelu(x):
    return 0.5 * x * (1.0 + jnp.tanh(0.7978845608028654 * (x + 0.044715 * (x * x * x))))


def _sigmoid(x):
    return 0.5 * jnp.tanh(0.5 * x) + 0.5


def _rms(x, nw):
    return (x * lax.rsqrt(jnp.mean(x * x, axis=-1, keepdims=True) + EPS)) * nw


def _rms_mod(x, nw, scale, shift):
    return _rms(x, nw) * (1.0 + scale) + shift


def _ada_kernel(c_ref, w_ref, b_ref, o_ref):
    c = c_ref[...]
    act = c * _sigmoid(c)
    o_ref[...] = jnp.dot(act, w_ref[...], precision=lax.Precision.HIGHEST,
                         preferred_element_type=F32) + b_ref[...]


def _ada_params(cvec, w, b):
    n = w.shape[1]
    tn = 1536
    out = pl.pallas_call(
        _ada_kernel,
        grid=(n // tn,),
        in_specs=[pl.BlockSpec((SUBLANES, D_MODEL), lambda j: (0, 0)),
                  pl.BlockSpec((D_MODEL, tn), lambda j: (0, j)),
                  pl.BlockSpec((1, tn), lambda j: (0, j))],
        out_specs=pl.BlockSpec((SUBLANES, tn), lambda j: (0, j)),
        out_shape=jax.ShapeDtypeStruct((SUBLANES, n), F32),
        compiler_params=_cparams(("parallel",)),
        name="ada_params",
    )(cvec, w, b.reshape(1, n))
    return out.reshape(SUBLANES, 6, D_MODEL)


def _proj_even_kernel(x_ref, mod_ref, nw_ref, w_ref, cos_ref, sin_ref,
                      u_ref, v_ref, q_ref, k_ref, val_ref):
    m = mod_ref[0]
    h = _rms_mod(x_ref[0], nw_ref[...], m[1:2], m[0:1]).astype(BF16)
    u_ref[0] = _gelu(_dot(h, w_ref[:, 0:SGU_WIDTH])).astype(u_ref.dtype)
    v_ref[0] = _gelu(_dot(h, w_ref[:, SGU_WIDTH:2 * SGU_WIDTH])).astype(v_ref.dtype)
    cos = cos_ref[...]
    sin = sin_ref[...]
    lane = lax.broadcasted_iota(jnp.int32, cos.shape, 1)
    first_half = (lane % 32) < ROPE_FREQS

    def rope(t):
        partner = jnp.where(first_half, pltpu.roll(t, LANES - ROPE_FREQS, 1), pltpu.roll(t, ROPE_FREQS, 1))
        return t * cos + partner * sin

    q = _dot(h, w_ref[:, 2 * SGU_WIDTH:2 * SGU_WIDTH + ATTN_WIDTH]) * (ATTN_SCALE * LOG2E)
    for g in range(ATTN_WIDTH // LANES):
        q_ref[0, :, g * LANES:(g + 1) * LANES] = rope(q[:, g * LANES:(g + 1) * LANES]).astype(BF16)
    kv = _dot(h, w_ref[:, 2 * SGU_WIDTH + ATTN_WIDTH:IN_EVEN])
    k = rope(kv[:, 0:KV_WIDTH])
    val = kv[:, KV_WIDTH:2 * KV_WIDTH]
    k_ref[0, :, 0:LANES] = k.astype(BF16)
    k_ref[0, :, LANES:2 * LANES] = pltpu.roll(k, HEAD_DIM, 1).astype(BF16)
    val_ref[0, :, 0:LANES] = val.astype(BF16)
    val_ref[0, :, LANES:2 * LANES] = pltpu.roll(val, HEAD_DIM, 1).astype(BF16)


def _proj_even(x, mod, mod_per_batch, nw, w, cos, sin, tm):
    B, S, D = x.shape
    mod_map = (lambda b, i: (b, 0, 0)) if mod_per_batch else (lambda b, i: (0, 0, 0))
    tok = lambda width: pl.BlockSpec((1, tm, width), lambda b, i: (b, i, 0))
    return pl.pallas_call(
        _proj_even_kernel,
        grid=(B, S // tm),
        in_specs=[tok(D),
                  pl.BlockSpec((1, 6, D), mod_map),
                  pl.BlockSpec((1, D), lambda b, i: (0, 0)),
                  pl.BlockSpec((D, IN_EVEN), lambda b, i: (0, 0)),
                  pl.BlockSpec((tm, LANES), lambda b, i: (i, 0)),
                  pl.BlockSpec((tm, LANES), lambda b, i: (i, 0))],
        out_specs=[tok(SGU_WIDTH), tok(SGU_WIDTH), tok(ATTN_WIDTH), tok(2 * KV_WIDTH), tok(2 * KV_WIDTH)],
        out_shape=[jax.ShapeDtypeStruct((B, S, SGU_WIDTH), BF16),
                   jax.ShapeDtypeStruct((B, S, SGU_WIDTH), BF16),
                   jax.ShapeDtypeStruct((B, S, ATTN_WIDTH), BF16),
                   jax.ShapeDtypeStruct((B, S, 2 * KV_WIDTH), BF16),
                   jax.ShapeDtypeStruct((B, S, 2 * KV_WIDTH), BF16)],
        compiler_params=_cparams(("parallel", "parallel")),
        name="proj_even",
    )(x, mod, nw, w, cos, sin)


def _mixer_even_kernel(sink_ref, x_ref, mod_ref, u_ref, v_ref, q_ref, k_ref, val_ref, kc_ref, vc_ref,
                       ws_ref, bs_ref, wout_ref, bias_ref, o_ref, mix_ref, s_ref, p_ref, inv_ref,
                       *, seq_len, is_ctx):
    tq = x_ref.shape[1]
    n_chunks = tq // CHUNK
    i = pl.program_id(1)
    nk = kc_ref.shape[1] + (0 if is_ctx else 3 * ATTN_BLOCK)
    lane = lax.broadcasted_iota(jnp.int32, (1, LANES), 1)
    lo = lane < HEAD_DIM
    zero = jnp.zeros((), BF16)

    def halves(ref_slice, kh):
        nat, swp = ref_slice[:, 0:LANES], ref_slice[:, LANES:2 * LANES]
        if kh == 0:
            return jnp.where(lo, nat, zero), jnp.where(lo, zero, swp)
        return jnp.where(lo, swp, zero), jnp.where(lo, zero, nat)

    kc_all = kc_ref[0]
    vc_all = vc_ref[0]

    def chunk_body(c, carry):
        r0 = pl.multiple_of(c * CHUNK, CHUNK)
        rows = pl.ds(r0, CHUNK)
        vch = v_ref[0, rows, :].astype(F32)
        uch = u_ref[0, rows, :].astype(F32)
        for g in range(SGU_GROUPS):
            cols = slice(g * LANES, (g + 1) * LANES)
            vg = vch[:, cols]
            dev = vg - jnp.mean(vg, axis=-1, keepdims=True)
            vn = dev * lax.rsqrt(jnp.mean(dev * dev, axis=-1, keepdims=True) + EPS)
            mixed = _dot(ws_ref[g], vn.astype(BF16)) + bs_ref[:, g:g + 1]
            mix_ref[rows, cols] = (uch[:, cols] * mixed).astype(BF16)
        qch = q_ref[0, rows, :]
        if not is_ctx:
            blk = i * n_chunks + c
            n_blk = seq_len // ATTN_BLOCK
            start = pl.multiple_of(jnp.clip((blk - 1) * ATTN_BLOCK, 0, seq_len - 3 * ATTN_BLOCK), ATTN_BLOCK)
            k3 = k_ref[0, pl.ds(start, 3 * ATTN_BLOCK), :]
            v3 = val_ref[0, pl.ds(start, 3 * ATTN_BLOCK), :]
            case = jnp.where(blk == 0, 0, jnp.where(blk == n_blk - 1, 2, 1))
        for kh in range(N_KV_HEADS):
            kc_lo, kc_hi = halves(kc_all, kh)
            vc_lo, vc_hi = halves(vc_all, kh)
            if is_ctx:
                k_cat = jnp.concatenate([kc_lo, kc_hi], axis=0)
                v_cat = (vc_lo, vc_hi)
            else:
                k_lo, k_hi = halves(k3, kh)
                v_lo, v_hi = halves(v3, kh)
                k_cat = jnp.concatenate([k_lo, kc_lo, k_hi, kc_hi], axis=0)
                v_cat = (jnp.concatenate([v_lo, vc_lo], axis=0), jnp.concatenate([v_hi, vc_hi], axis=0))
            q2 = jnp.concatenate([qch[:, 2 * kh * LANES:(2 * kh + 1) * LANES],
                                  qch[:, (2 * kh + 1) * LANES:(2 * kh + 2) * LANES]], axis=0)
            s_ref[:, 0:2 * nk] = _dot_nt(q2, k_cat)
            for half in range(2):
                for rb in range(2 * ATTN_BLOCK // ROW_BLOCK):
                    rsl = slice(rb * ROW_BLOCK, (rb + 1) * ROW_BLOCK)
                    snk = sink_ref[2 * (2 * kh + rb * ROW_BLOCK // ATTN_BLOCK) + half]
                    s = s_ref[rsl, half * nk:(half + 1) * nk]
                    if not is_ctx:
                        qoff = (rb * ROW_BLOCK) % ATTN_BLOCK
                        s_loc = s[:, 0:3 * ATTN_BLOCK] + bias_ref[case, qoff:qoff + ROW_BLOCK, :]
                        s = jnp.concatenate([s_loc, s[:, 3 * ATTN_BLOCK:]], axis=1)
                    m = jnp.maximum(jnp.max(s, axis=-1, keepdims=True), snk)
                    p = jnp.exp2(s - m)
                    den = jnp.sum(p, axis=-1, keepdims=True) + jnp.exp2(snk - m)
                    p_ref[rsl, half * nk:(half + 1) * nk] = p.astype(BF16)
                    inv_ref[rsl, half * LANES:(half + 1) * LANES] = jnp.broadcast_to(1.0 / den, (ROW_BLOCK, LANES))
            o_lo = _dot(p_ref[:, 0:nk], v_cat[0])
            o_hi = _dot(p_ref[:, nk:2 * nk], v_cat[1])
            acc = (o_lo * inv_ref[:, 0:LANES] + o_hi * inv_ref[:, LANES:2 * LANES]).astype(BF16)
            for g in range(2):
                col = SGU_WIDTH + (2 * kh + g) * LANES
                mix_ref[rows, col:col + LANES] = acc[g * ATTN_BLOCK:(g + 1) * ATTN_BLOCK]
        return carry

    lax.fori_loop(0, n_chunks, chunk_body, 0)
    y = _dot(mix_ref[...], wout_ref[...])
    o_ref[0] = x_ref[0] + mod_ref[0][2:3] * y


def _window_bias():
    case = jnp.arange(3, dtype=jnp.int32)[:, None, None]
    qi = jnp.arange(ATTN_BLOCK, dtype=jnp.int32)[None, :, None]
    kj = jnp.arange(3 * ATTN_BLOCK, dtype=jnp.int32)[None, None, :]
    return jnp.where(jnp.abs(kj - case * ATTN_BLOCK - qi) <= WINDOW, 0.0, NEG_INF).astype(F32)


def _mixer_even(x, mod, mod_per_batch, u, v, q, k2, v2, kc2, vc2, ws, bs_t, sink, wout, tq, is_ctx):
    B, S, D = x.shape
    Sk = k2.shape[1]
    Lc = kc2.shape[1]
    nk = Lc + (0 if is_ctx else 3 * ATTN_BLOCK)
    mod_map = (lambda b, i: (b, 0, 0)) if mod_per_batch else (lambda b, i: (0, 0, 0))
    tok = lambda width: pl.BlockSpec((1, tq, width), lambda b, i: (b, i, 0))
    per_batch = lambda rows: pl.BlockSpec((1, rows, 2 * KV_WIDTH), lambda b, i: (b, 0, 0))
    return pl.pallas_call(
        functools.partial(_mixer_even_kernel, seq_len=S, is_ctx=is_ctx),
        grid=(B, S // tq),
        in_specs=[pl.BlockSpec(memory_space=pltpu.SMEM),
                  tok(D),
                  pl.BlockSpec((1, 6, D), mod_map),
                  tok(SGU_WIDTH), tok(SGU_WIDTH), tok(ATTN_WIDTH),
                  per_batch(Sk), per_batch(Sk), per_batch(Lc), per_batch(Lc),
                  pl.BlockSpec((SGU_GROUPS, CHUNK, CHUNK), lambda b, i: (0, 0, 0)),
                  pl.BlockSpec((CHUNK, SGU_GROUPS), lambda b, i: (0, 0)),
                  pl.BlockSpec((D, D), lambda b, i: (0, 0)),
                  pl.BlockSpec((3, ATTN_BLOCK, 3 * ATTN_BLOCK), lambda b, i: (0, 0, 0))],
        out_specs=tok(D),
        out_shape=jax.ShapeDtypeStruct((B, S, D), F32),
        scratch_shapes=[pltpu.VMEM((tq, D), BF16),
                        pltpu.VMEM((2 * ATTN_BLOCK, 2 * nk), F32),
                        pltpu.VMEM((2 * ATTN_BLOCK, 2 * nk), BF16),
                        pltpu.VMEM((2 * ATTN_BLOCK, 2 * LANES), F32)],
        compiler_params=_cparams(("parallel", "arbitrary")),
        name="mixer_ctx" if is_ctx else "mixer_even",
    )(sink, x, mod, u, v, q, k2, v2, kc2, vc2, ws, bs_t, wout, _window_bias())


def _route_kernel(x_ref, mod_ref, nw_ref, rw_ref, hn_ref, meta_ref, pos_ref, cnt_ref, tb_ref, base_ref, *, cap):
    @pl.when(jnp.logical_and(pl.program_id(0) == 0, pl.program_id(1) == 0))
    def _():
        base_ref[...] = jnp.zeros_like(base_ref)

    m = mod_ref[0]
    h = _rms_mod(x_ref[0], nw_ref[...], m[4:5], m[3:4])
    hn_ref[0] = h
    w = rw_ref[...]
    w_hi = w.astype(BF16)
    w_lo = (w - w_hi.astype(F32)).astype(BF16)
    h_hi = h.astype(BF16)
    h_lo = (h - h_hi.astype(F32)).astype(BF16)
    logits = _dot(h_hi, w_hi) + (_dot(h_lo, w_hi) + _dot(h_hi, w_lo))
    tm = logits.shape[0]
    lane = lax.broadcasted_iota(jnp.int32, logits.shape, 1)
    lg = jnp.where(lane < N_EXPERTS, logits, -jnp.inf)
    m1 = jnp.max(lg, axis=-1, keepdims=True)
    i1 = jnp.min(jnp.where(lg == m1, lane, LANES), axis=-1, keepdims=True)
    lg2 = jnp.where(lane == i1, -jnp.inf, lg)
    m2 = jnp.max(lg2, axis=-1, keepdims=True)
    i2 = jnp.min(jnp.where(lg2 == m2, lane, LANES), axis=-1, keepdims=True)
    e2 = jnp.exp(m2 - m1)
    den = 1.0 + e2
    hot = jnp.where(jnp.logical_or(lane == i1, lane == i2), 1.0, 0.0)
    r = lax.broadcasted_iota(jnp.int32, (tm, tm), 0)
    c = lax.broadcasted_iota(jnp.int32, (tm, tm), 1)
    before = jnp.where(r > c, 1.0, 0.0).astype(BF16)
    tot = base_ref[...] + _dot(before, hot.astype(BF16))
    rank1 = jnp.sum(jnp.where(lane == i1, tot, 0.0), axis=-1, keepdims=True)
    rank2 = jnp.sum(jnp.where(lane == i2, tot, 0.0), axis=-1, keepdims=True)
    pos1 = i1.astype(F32) * cap + rank1
    pos2 = i2.astype(F32) * cap + rank2
    meta = (jnp.where(lane == 0, pos1, 0.0) + jnp.where(lane == 1, pos2, 0.0)
            + jnp.where(lane == 2, 1.0 / den, 0.0) + jnp.where(lane == 3, e2 / den, 0.0)
            + jnp.where(lane == 4, i1.astype(F32), 0.0) + jnp.where(lane == 5, i2.astype(F32), 0.0))
    meta_ref[0] = meta
    pos_ref[0, 0] = meta.T[0:SUBLANES].astype(jnp.int32)
    sub = lax.broadcasted_iota(jnp.int32, (SUBLANES, LANES), 0)
    tb = jnp.zeros((SUBLANES, LANES), F32)
    for j in range(tm // COMBINE_TILE):
        tb = jnp.where(sub == j, tot[j * COMBINE_TILE:j * COMBINE_TILE + 1, :], tb)
    tb_ref[0, 0] = tb
    base_ref[...] += jnp.sum(hot, axis=0, keepdims=True)
    cnt_ref[...] = jnp.broadcast_to(base_ref[...], cnt_ref.shape)


def _route(x, mod, nw, rw, tm, cap):
    B, S, D = x.shape
    return pl.pallas_call(
        functools.partial(_route_kernel, cap=float(cap)),
        grid=(B, S // tm),
        in_specs=[pl.BlockSpec((1, tm, D), lambda b, i: (b, i, 0)),
                  pl.BlockSpec((1, 6, D), lambda b, i: (b, 0, 0)),
                  pl.BlockSpec((1, D), lambda b, i: (0, 0)),
                  pl.BlockSpec((D, LANES), lambda b, i: (0, 0))],
        out_specs=[pl.BlockSpec((1, tm, D), lambda b, i: (b, i, 0)),
                   pl.BlockSpec((1, tm, LANES), lambda b, i: (b, i, 0)),
                   pl.BlockSpec((1, 1, SUBLANES, tm), lambda b, i: (b, i, 0, 0)),
                   pl.BlockSpec((SUBLANES, LANES), lambda b, i: (0, 0)),
                   pl.BlockSpec((1, 1, SUBLANES, LANES), lambda b, i: (b, i, 0, 0))],
        out_shape=[jax.ShapeDtypeStruct((B, S, D), F32),
                   jax.ShapeDtypeStruct((B, S, LANES), F32),
                   jax.ShapeDtypeStruct((B, S // tm, SUBLANES, tm), jnp.int32),
                   jax.ShapeDtypeStruct((SUBLANES, LANES), F32),
                   jax.ShapeDtypeStruct((B, S // tm, SUBLANES, LANES), F32)],
        scratch_shapes=[pltpu.VMEM((1, LANES), F32)],
        compiler_params=_cparams(("arbitrary", "arbitrary")),
        name="moe_route",
    )(x, mod, nw, rw)


def _dispatch_kernel(pos_ref, cnt_ref, hn_ref, xs_ref, zero_ref, sem, zsem, *, n_tok, cap, tg):
    i = pl.program_id(0)
    td = hn_ref.shape[0]

    @pl.when(i == 0)
    def _():
        zero_ref[...] = jnp.zeros_like(zero_ref)

        def tail_copy(e):
            start = pl.multiple_of(e * cap + (cnt_ref[e] // SUBLANES) * SUBLANES, SUBLANES)
            return pltpu.make_async_copy(zero_ref, xs_ref.at[pl.ds(start, tg)], zsem)

        for e in range(N_EXPERTS):
            tail_copy(e).start()
        for e in range(N_EXPERTS):
            tail_copy(e).wait()

    def tok_body(t, c):
        tok = i * td + t
        pltpu.make_async_copy(hn_ref.at[pl.ds(t, 1)], xs_ref.at[pl.ds(pos_ref[tok], 1)], sem).start()
        pltpu.make_async_copy(hn_ref.at[pl.ds(t, 1)], xs_ref.at[pl.ds(pos_ref[n_tok + tok], 1)], sem).start()
        return c

    lax.fori_loop(0, td, tok_body, 0, unroll=8)
    for _ in range(2):
        pltpu.make_async_copy(hn_ref, hn_ref, sem).wait()


def _dispatch(pos, cnt, hn, cap, tg, td):
    T, D = hn.shape
    return pl.pallas_call(
        functools.partial(_dispatch_kernel, n_tok=T, cap=cap, tg=tg),
        grid_spec=pltpu.PrefetchScalarGridSpec(
            num_scalar_prefetch=2,
            grid=(T // td,),
            in_specs=[pl.BlockSpec((td, D), lambda i, pos, cnt: (i, 0))],
            out_specs=pl.BlockSpec(memory_space=pl.ANY),
            scratch_shapes=[pltpu.VMEM((tg, D), F32), pltpu.SemaphoreType.DMA, pltpu.SemaphoreType.DMA]),
        out_shape=jax.ShapeDtypeStruct((N_EXPERTS * cap, D), F32),
        compiler_params=_cparams(("arbitrary",)),
        name="moe_dispatch",
    )(pos, cnt, hn)


def _moe_group_kernel(blk_ref, exp_ref, fa_ref, fb_ref, nact_ref, x_ref, w1_ref, w3_ref, w2_ref, o_ref,
                      h_ref, acc_ref):
    t = pl.program_id(0)
    f = pl.program_id(1)

    @pl.when(t < nact_ref[0])
    def _():
        @pl.when(f == 0)
        def _():
            h_ref[...] = x_ref[...].astype(BF16)

        h = h_ref[...]
        a = _dot(h, w1_ref[0])
        b = _dot(h, w3_ref[0])
        y = _dot(((a * _sigmoid(a)) * b).astype(BF16), w2_ref[0])

        @pl.when(f == 0)
        def _():
            acc_ref[...] = y

        @pl.when(f == 1)
        def _():
            o_ref[...] = (acc_ref[...] + y).astype(o_ref.dtype)


def _moe_group(blk, exp, fa, fb, nact, xs, w1, w3, w2, tg):
    R, D = xs.shape
    F = w1.shape[2]
    tf = F // 2
    fsel = lambda f, fa, fb, t: jnp.where(f == 0, fa[t], fb[t])
    return pl.pallas_call(
        _moe_group_kernel,
        grid_spec=pltpu.PrefetchScalarGridSpec(
            num_scalar_prefetch=5,
            grid=(blk.shape[0], 2),
            in_specs=[pl.BlockSpec((tg, D), lambda t, f, blk, exp, fa, fb, na: (blk[t], 0)),
                      pl.BlockSpec((1, D, tf), lambda t, f, blk, exp, fa, fb, na: (exp[t], 0, fsel(f, fa, fb, t))),
                      pl.BlockSpec((1, D, tf), lambda t, f, blk, exp, fa, fb, na: (exp[t], 0, fsel(f, fa, fb, t))),
                      pl.BlockSpec((1, tf, D), lambda t, f, blk, exp, fa, fb, na: (exp[t], fsel(f, fa, fb, t), 0))],
            out_specs=pl.BlockSpec((tg, D), lambda t, f, blk, exp, fa, fb, na: (blk[t], 0)),
            scratch_shapes=[pltpu.VMEM((tg, D), BF16), pltpu.VMEM((tg, D), F32)]),
        out_shape=jax.ShapeDtypeStruct((R, D), BF16),
        compiler_params=_cparams(("arbitrary", "arbitrary")),
        name="moe_experts",
    )(blk, exp, fa, fb, nact, xs, w1, w3, w2)


def _combine_kernel(tb_ref, x_ref, mod_ref, meta_ref, fn_ref, y_ref, o_ref, buf_ref, sem, *, cap):
    i = pl.program_id(0)
    n = pl.num_programs(0)
    tc = x_ref.shape[0]
    n_rows = buf_ref.shape[1]
    shift = COMBINE_CHUNK.bit_length() - 1

    def segments(tile):
        segs, off = [], 0
        for e in range(N_EXPERTS):
            first = tb_ref[tile * N_EXPERTS + e]
            count = tb_ref[(tile + 1) * N_EXPERTS + e] - first
            lead = first & (COMBINE_CHUNK - 1)
            n_chunks = jnp.where(count > 0, lax.shift_right_logical(lead + count + COMBINE_CHUNK - 1, shift), 0)
            segs.append((e * cap + first - lead, n_chunks, off))
            off = off + n_chunks * COMBINE_CHUNK
        return segs

    def chunk_copy(src_row, dst_row, slot):
        return pltpu.make_async_copy(y_ref.at[pl.ds(pl.multiple_of(src_row, COMBINE_CHUNK), COMBINE_CHUNK)],
                                     buf_ref.at[slot, pl.ds(pl.multiple_of(dst_row, COMBINE_CHUNK), COMBINE_CHUNK)],
                                     sem.at[slot])

    def for_each_chunk(tile, slot, action):
        for src, n_chunks, off in segments(tile):
            def body(c, carry):
                action(chunk_copy(src + c * COMBINE_CHUNK, off + c * COMBINE_CHUNK, slot))
                return carry

            lax.fori_loop(0, n_chunks, body, 0)

    @pl.when(i == 0)
    def _():
        buf_ref[...] = jnp.zeros_like(buf_ref)
        for_each_chunk(0, 0, lambda cp: cp.start())

    slot = i % 2

    @pl.when(i + 1 < n)
    def _():
        for_each_chunk(i + 1, 1 - slot, lambda cp: cp.start())

    for_each_chunk(i, slot, lambda cp: cp.wait())
    meta = meta_ref[...]
    segs = segments(i)

    def one_hot(pos, exp):
        delta = jnp.zeros_like(pos)
        for e, (src, _, off) in enumerate(segs):
            delta = jnp.where(exp == e, off - src, delta)
        col = lax.broadcasted_iota(jnp.int32, (tc, n_rows), 1)
        return jnp.where(col == pos + delta, 1.0, 0.0).astype(BF16)

    as_int = lambda lane: meta[:, lane:lane + 1].astype(jnp.int32)
    rows = buf_ref[slot]
    y1 = _dot(one_hot(as_int(0), as_int(4)), rows)
    y2 = _dot(one_hot(as_int(1), as_int(5)), rows)
    mix = meta[:, 2:3] * y1 + meta[:, 3:4] * y2
    o_ref[...] = _rms(x_ref[...] + mod_ref[0][5:6] * mix, fn_ref[...])


def _combine(tile_base, x, mod, meta, fin, y, seq_len, cap):
    T, D = x.shape
    tc = COMBINE_TILE
    per_batch = seq_len // tc
    n_rows = -(-(2 * tc + 2 * N_EXPERTS * (COMBINE_CHUNK - 1)) // LANES) * LANES
    return pl.pallas_call(
        functools.partial(_combine_kernel, cap=cap),
        grid_spec=pltpu.PrefetchScalarGridSpec(
            num_scalar_prefetch=1,
            grid=(T // tc,),
            in_specs=[pl.BlockSpec((tc, D), lambda i, tb: (i, 0)),
                      pl.BlockSpec((1, 6, D), lambda i, tb: (i // per_batch, 0, 0)),
                      pl.BlockSpec((tc, LANES), lambda i, tb: (i, 0)),
                      pl.BlockSpec((1, D), lambda i, tb: (0, 0)),
                      pl.BlockSpec(memory_space=pl.ANY)],
            out_specs=pl.BlockSpec((tc, D), lambda i, tb: (i, 0)),
            scratch_shapes=[pltpu.VMEM((2, n_rows, D), BF16), pltpu.SemaphoreType.DMA((2,))]),
        out_shape=jax.ShapeDtypeStruct((T, D), F32),
        compiler_params=_cparams(("arbitrary",)),
        name="moe_combine",
    )(tile_base, x, mod, meta, fin, y)


def _moe_tiles(cnt, n_tiles, tg, cap):
    per = (cnt + tg - 1) // tg
    cum = jnp.cumsum(per)
    nact = cum[-1]
    t = jnp.arange(n_tiles, dtype=jnp.int32)
    tt = jnp.minimum(t, nact - 1)
    exp = jnp.minimum(jnp.sum((tt[:, None] >= cum[None, :]).astype(jnp.int32), axis=1), N_EXPERTS - 1)
    blk = exp * (cap // tg) + tt - (cum - per)[exp]
    odd = tt % 2
    fa = jnp.where(t < nact, odd, 1 - odd)
    fb = 1 - odd
    i32 = lambda v: v.astype(jnp.int32)
    return i32(blk), i32(exp), i32(fa), i32(fb), i32(nact.reshape(1))


def _ffn_kernel(x_ref, mod_ref, nw_ref, w1_ref, w3_ref, w2_ref, o_ref, h_ref, acc_ref):
    f = pl.program_id(2)

    @pl.when(f == 0)
    def _():
        m = mod_ref[0]
        h_ref[...] = _rms_mod(x_ref[0], nw_ref[...], m[4:5], m[3:4]).astype(BF16)
        acc_ref[...] = jnp.zeros_like(acc_ref)

    h = h_ref[...]
    a = _dot(h, w1_ref[...])
    b = _dot(h, w3_ref[...])
    acc_ref[...] += _dot(((a * _sigmoid(a)) * b).astype(BF16), w2_ref[...])

    @pl.when(f == pl.num_programs(2) - 1)
    def _():
        o_ref[0] = x_ref[0] + mod_ref[0][5:6] * acc_ref[...]


def _ffn(x, mod, mod_per_batch, nw, w1, w3, w2, tm, tf):
    B, S, D = x.shape
    F = w1.shape[1]
    mod_map = (lambda b, i, f: (b, 0, 0)) if mod_per_batch else (lambda b, i, f: (0, 0, 0))
    return pl.pallas_call(
        _ffn_kernel,
        grid=(B, S // tm, F // tf),
        in_specs=[pl.BlockSpec((1, tm, D), lambda b, i, f: (b, i, 0)),
                  pl.BlockSpec((1, 6, D), mod_map),
                  pl.BlockSpec((1, D), lambda b, i, f: (0, 0)),
                  pl.BlockSpec((D, tf), lambda b, i, f: (0, f)),
                  pl.BlockSpec((D, tf), lambda b, i, f: (0, f)),
                  pl.BlockSpec((tf, D), lambda b, i, f: (f, 0))],
        out_specs=pl.BlockSpec((1, tm, D), lambda b, i, f: (b, i, 0)),
        out_shape=jax.ShapeDtypeStruct((B, S, D), F32),
        scratch_shapes=[pltpu.VMEM((tm, D), BF16), pltpu.VMEM((tm, D), F32)],
        compiler_params=_cparams(("parallel", "parallel", "arbitrary")),
        name="ffn",
    )(x, mod, nw, w1, w3, w2)


def _proj_odd_kernel(x_ref, mod_ref, nw_ref, w_ref, gate_ref, rec_ref):
    m = mod_ref[0]
    h = _rms_mod(x_ref[0], nw_ref[...], m[1:2], m[0:1]).astype(BF16)
    gate_ref[0] = _gelu(_dot(h, w_ref[:, 0:D_RNN])).astype(gate_ref.dtype)
    rec_ref[0] = _dot(h, w_ref[:, D_RNN:2 * D_RNN])


def _proj_odd(x, mod, mod_per_batch, nw, w, tm):
    B, S, D = x.shape
    mod_map = (lambda b, i: (b, 0, 0)) if mod_per_batch else (lambda b, i: (0, 0, 0))
    tok = lambda width: pl.BlockSpec((1, tm, width), lambda b, i: (b, i, 0))
    return pl.pallas_call(
        _proj_odd_kernel,
        grid=(B, S // tm),
        in_specs=[tok(D),
                  pl.BlockSpec((1, 6, D), mod_map),
                  pl.BlockSpec((1, D), lambda b, i: (0, 0)),
                  pl.BlockSpec((D, 2 * D_RNN), lambda b, i: (0, 0))],
        out_specs=[tok(D_RNN), tok(D_RNN)],
        out_shape=[jax.ShapeDtypeStruct((B, S, D_RNN), BF16), jax.ShapeDtypeStruct((B, S, D_RNN), F32)],
        compiler_params=_cparams(("parallel", "parallel")),
        name="proj_odd",
    )(x, mod, nw, w)


def _scan8(a, b, h, row, reverse):
    for s in (1, 2, 4):
        if reverse:
            a_s, b_s, live = pltpu.roll(a, SUBLANES - s, 0), pltpu.roll(b, SUBLANES - s, 0), row < SUBLANES - s
        else:
            a_s, b_s, live = pltpu.roll(a, s, 0), pltpu.roll(b, s, 0), row >= s
        b = jnp.where(live, a * b_s + b, b)
        a = jnp.where(live, a * a_s, a)
    hr = a * h + b
    return hr, (hr[0:1] if reverse else hr[SUBLANES - 1:SUBLANES])


def _lru_kernel(rec_ref, recc_ref, cw_ref, cb_ref, wa_ref, ba_ref, wx_ref, bx_ref, lam_ref,
                s_ref, pad_ref, a_ref, b_ref, cpad_ref, ca_ref, cbb_ref, *, tile):
    S = rec_ref.shape[1]
    L = recc_ref.shape[1]
    cw = cw_ref[...]
    cb = cb_ref[...]
    lam = lam_ref[...]
    sp = jnp.maximum(-lam, 0.0) + jnp.log1p(jnp.exp(-jnp.abs(lam)))
    zeros8 = jnp.zeros((SUBLANES, LANES), F32)

    def coefficients(src_ref, dst_a, dst_b, n_rows, t):
        pad = cpad_ref if src_ref is recc_ref else pad_ref
        pad[0:SUBLANES, :] = zeros8
        pad[SUBLANES + n_rows:2 * SUBLANES + n_rows, :] = zeros8

        def copy(j, carry):
            r0 = pl.multiple_of(j * t, t)
            pad[pl.ds(SUBLANES + r0, t), :] = src_ref[0, pl.ds(r0, t), :]
            return carry

        lax.fori_loop(0, n_rows // t, copy, 0)

        def body(j, carry):
            r0 = pl.multiple_of(j * t, t)
            ext = pad[pl.ds(r0, t + 2 * SUBLANES), :]
            conv = cb
            for tap in range(4):
                conv = conv + cw[tap:tap + 1] * ext[SUBLANES - 2 + tap:SUBLANES - 2 + tap + t]
            cbf = conv.astype(BF16)
            for d in range(2):
                r = _sigmoid(_dot(cbf, wa_ref[d, 0]) + ba_ref[d:d + 1])
                gi = _sigmoid(_dot(cbf, wx_ref[d, 0]) + bx_ref[d:d + 1])
                log_a = -LRU_C * r * sp[d:d + 1]
                a = jnp.exp(log_a)
                y = -jnp.tanh(0.5 * log_a)
                root = jnp.where(y > 0.0, y * lax.rsqrt(y), 0.0)
                dst_a[d, pl.ds(r0, t), :] = a
                dst_b[d, pl.ds(r0, t), :] = (root * (1.0 + a)) * (gi * conv)
            return carry

        lax.fori_loop(0, n_rows // t, body, 0)

    row = lax.broadcasted_iota(jnp.int32, (SUBLANES, LANES), 0)
    h_zero = jnp.zeros((1, LANES), F32)

    coefficients(recc_ref, ca_ref, cbb_ref, L, L)
    nc = L // SUBLANES

    def ctx_body(j, carry):
        hf, hb = carry
        rf = pl.multiple_of(j * SUBLANES, SUBLANES)
        rb = pl.multiple_of((nc - 1 - j) * SUBLANES, SUBLANES)
        _, hf = _scan8(ca_ref[0, pl.ds(rf, SUBLANES), :], cbb_ref[0, pl.ds(rf, SUBLANES), :], hf, row, False)
        _, hb = _scan8(ca_ref[1, pl.ds(rb, SUBLANES), :], cbb_ref[1, pl.ds(rb, SUBLANES), :], hb, row, True)
        return hf, hb

    h0f, h0b = lax.fori_loop(0, nc, ctx_body, (h_zero, h_zero))

    coefficients(rec_ref, a_ref, b_ref, S, tile)
    n = S // SUBLANES

    def lat_body(accumulate):
        def body(j, carry):
            hf, hb = carry
            rf = pl.multiple_of(j * SUBLANES, SUBLANES)
            rb = pl.multiple_of((n - 1 - j) * SUBLANES, SUBLANES)
            of, hf = _scan8(a_ref[0, pl.ds(rf, SUBLANES), :], b_ref[0, pl.ds(rf, SUBLANES), :], hf, row, False)
            ob, hb = _scan8(a_ref[1, pl.ds(rb, SUBLANES), :], b_ref[1, pl.ds(rb, SUBLANES), :], hb, row, True)
            if accumulate:
                s_ref[0, pl.ds(rf, SUBLANES), :] += of
                s_ref[0, pl.ds(rb, SUBLANES), :] += ob
            else:
                s_ref[0, pl.ds(rf, SUBLANES), :] = of
                s_ref[0, pl.ds(rb, SUBLANES), :] = ob
            return hf, hb
        return body

    mid = lax.fori_loop(0, n // 2, lat_body(False), (h0f, h0b), unroll=4)
    lax.fori_loop(n // 2, n, lat_body(True), mid, unroll=4)


def _lru(rec, rec_c, conv_w, conv_b, wa, ba, wx, bx, lam, tile):
    B, S, _ = rec.shape
    L = rec_c.shape[1]
    blk = lambda rows: pl.BlockSpec((1, rows, LRU_BLOCK), lambda b, j: (b, 0, j))
    vec = lambda rows: pl.BlockSpec((rows, LRU_BLOCK), lambda b, j: (0, j))
    wspec = pl.BlockSpec((2, 1, LRU_BLOCK, LRU_BLOCK), lambda b, j: (0, j, 0, 0))
    return pl.pallas_call(
        functools.partial(_lru_kernel, tile=tile),
        grid=(B, LRU_BLOCKS),
        in_specs=[blk(S), blk(L), vec(4), vec(1), wspec, vec(2), wspec, vec(2), vec(2)],
        out_specs=blk(S),
        out_shape=jax.ShapeDtypeStruct((B, S, D_RNN), F32),
        scratch_shapes=[pltpu.VMEM((S + 2 * SUBLANES, LRU_BLOCK), F32),
                        pltpu.VMEM((2, S, LRU_BLOCK), F32),
                        pltpu.VMEM((2, S, LRU_BLOCK), F32),
                        pltpu.VMEM((L + 2 * SUBLANES, LRU_BLOCK), F32),
                        pltpu.VMEM((2, L, LRU_BLOCK), F32),
                        pltpu.VMEM((2, L, LRU_BLOCK), F32)],
        compiler_params=_cparams(("parallel", "parallel")),
        name="lru_scan",
    )(rec, rec_c, conv_w, conv_b, wa, ba, wx, bx, lam)


def _lru_out_kernel(x_ref, mod_ref, gate_ref, s_ref, w_ref, o_ref):
    y = (gate_ref[0].astype(F32) * s_ref[0]).astype(BF16)
    o_ref[0] = x_ref[0] + mod_ref[0][2:3] * _dot(y, w_ref[...])


def _lru_out(x, mod, gate, s, w, tm):
    B, S, D = x.shape
    tok = lambda width: pl.BlockSpec((1, tm, width), lambda b, i: (b, i, 0))
    return pl.pallas_call(
        _lru_out_kernel,
        grid=(B, S // tm),
        in_specs=[tok(D), pl.BlockSpec((1, 6, D), lambda b, i: (b, 0, 0)), tok(D_RNN), tok(D_RNN),
                  pl.BlockSpec((D_RNN, D), lambda b, i: (0, 0))],
        out_specs=tok(D),
        out_shape=jax.ShapeDtypeStruct((B, S, D), F32),
        compiler_params=_cparams(("parallel", "parallel")),
        name="lru_out",
    )(x, mod, gate, s, w)


def _rope_tables(n_tok):
    rows = n_tok // GRID_W
    row = jnp.repeat(jnp.arange(rows, dtype=F32), GRID_W)
    col = jnp.tile(jnp.arange(GRID_W, dtype=F32), rows)
    freqs = ROPE_BASE ** (-jnp.arange(ROPE_FREQS, dtype=F32) / ROPE_FREQS)
    ar, ac = row[:, None] * freqs, col[:, None] * freqs
    cos = jnp.concatenate([jnp.cos(ar), jnp.cos(ar), jnp.cos(ac), jnp.cos(ac)], axis=-1)
    sin = jnp.concatenate([-jnp.sin(ar), jnp.sin(ar), -jnp.sin(ac), jnp.sin(ac)], axis=-1)
    return jnp.tile(cos, (1, LANES // HEAD_DIM)), jnp.tile(sin, (1, LANES // HEAD_DIM))


def kernel(x, c, ctx, c_ctx, ada_w_e, ada_b_e, norm1_e, norm2_e, w_in_e, sgu_w, sgu_b, attn_sink, w_out_e, ffn_w1, ffn_w3, ffn_w2, ada_w_o, ada_b_o, norm1_o, norm2_o, w_in_o, conv_w, conv_b, lru_wa, lru_ba, lru_wx, lru_bx, lru_lambda, w_out_o, router_w, moe_w1, moe_w3, moe_w2, final_norm):
    B, S, D = x.shape
    L = ctx.shape[1]
    cvec = jnp.concatenate([c, c_ctx[None], jnp.zeros((SUBLANES - B - 1, D), F32)], axis=0)
    mod_e = _ada_params(cvec, ada_w_e[0], ada_b_e[0])
    mod_o = _ada_params(cvec, ada_w_o[0], ada_b_o[0])
    lat_e, ctx_e = mod_e[0:B], mod_e[B:B + 1]
    lat_o, ctx_o = mod_o[0:B], mod_o[B:B + 1]
    bf = lambda t: t.astype(BF16)
    row = lambda t: t.reshape(1, -1)

    cos, sin = _rope_tables(S)
    cos_c, sin_c = jnp.ones((L, LANES), F32), jnp.zeros((L, LANES), F32)
    w_in = bf(w_in_e[0])
    n1, n2 = row(norm1_e[0]), row(norm2_e[0])
    uc, vc, qc, kc2, vc2 = _proj_even(ctx, ctx_e, False, n1, w_in, cos_c, sin_c, L)
    u, v, q, k2, v2 = _proj_even(x, lat_e, True, n1, w_in, cos, sin, 512)
    ws, bs_t, wout = bf(sgu_w[0]), sgu_b[0].T, bf(w_out_e[0])
    sink = attn_sink[0] * LOG2E
    x = _mixer_even(x, lat_e, True, u, v, q, k2, v2, kc2, vc2, ws, bs_t, sink, wout, 512, False)
    xc = _mixer_even(ctx, ctx_e, False, uc, vc, qc, kc2, vc2, kc2, vc2, ws, bs_t, sink, wout, L, True)
    w1, w3, w2 = bf(ffn_w1[0]), bf(ffn_w3[0]), bf(ffn_w2[0])
    x = _ffn(x, lat_e, True, n2, w1, w3, w2, 512, 1408)
    xc = _ffn(xc, ctx_e, False, n2, w1, w3, w2, L, 1408)

    w_in = bf(w_in_o[0])
    n1, n2 = row(norm1_o[0]), row(norm2_o[0])
    _, rec_c = _proj_odd(xc, ctx_o, False, n1, w_in, L)
    gate, rec = _proj_odd(x, lat_o, True, n1, w_in, 512)
    s = _lru(rec, rec_c, conv_w[0], row(conv_b[0]), bf(lru_wa[0]), lru_ba[0], bf(lru_wx[0]), lru_bx[0],
             lru_lambda[0], 512)
    x = _lru_out(x, lat_o, gate, s, bf(w_out_o[0]), 512)
    rw = jnp.pad(router_w[0], ((0, 0), (0, LANES - N_EXPERTS)))
    T = B * S
    tg = 512
    cap = T + tg
    tm = 512
    hn, meta, pos4, cnt, tb4 = _route(x, lat_o, n2, rw, tm, cap)
    pos = jnp.transpose(pos4[:, :, 0:2, :], (2, 0, 1, 3)).reshape(2 * T)
    cnt = cnt[0, 0:N_EXPERTS].astype(jnp.int32)
    tile_base = jnp.concatenate([tb4[:, :, 0:tm // COMBINE_TILE, 0:N_EXPERTS].astype(jnp.int32).reshape(-1), cnt])
    xs = _dispatch(pos, cnt, hn.reshape(T, D), cap, tg, 512)
    blk, exp, fa, fb, nact = _moe_tiles(cnt, 2 * T // tg + N_EXPERTS, tg, cap)
    y = _moe_group(blk, exp, fa, fb, nact, xs, bf(moe_w1[0]), bf(moe_w3[0]), bf(moe_w2[0]), tg)
    out = _combine(tile_base, x.reshape(T, D), lat_o, meta.reshape(T, LANES), row(final_norm), y, S, cap)
    return out.reshape(B, S, D)
```

```python
import functools

import jax
import jax.numpy as jnp
from jax import lax
from jax.experimental import pallas as pl
from jax.experimental.pallas import tpu as pltpu

F32 = jnp.float32
BF16 = jnp.bfloat16

D_MODEL = 1024
GRID_W = 64
EPS = 1e-6
NEG_INF = -1e30
CHUNK = 128
SGU_GROUPS = 4
SGU_WIDTH = 512
HEAD_DIM = 64
N_Q_HEADS = 8
N_KV_HEADS = 2
ATTN_WIDTH = 512
KV_WIDTH = 128
WINDOW = 128
ATTN_BLOCK = 128
ATTN_SCALE = HEAD_DIM ** -0.5
LOG2E = 1.4426950408889634
ROW_BLOCK = 32
COMBINE_TILE = 256
COMBINE_CHUNK = 16
ROPE_BASE = 10000.0
ROPE_FREQS = 16
IN_EVEN = 1792
D_RNN = 1280
LRU_BLOCKS = 10
LRU_BLOCK = 128
LRU_C = 8.0
D_FF = 2816
N_EXPERTS = 8
LANES = 128
SUBLANES = 8
VMEM_LIMIT = 56 * 1024 * 1024


def _cparams(sem):
    return pltpu.CompilerParams(dimension_semantics=sem, vmem_limit_bytes=VMEM_LIMIT)


def _dot(a, b):
    return jnp.dot(a, b, preferred_element_type=F32)


def _dot_nt(a, b):
    return lax.dot_general(a, b, (((1,), (1,)), ((), ())), preferred_element_type=F32)


def _gelu(x):
    return 0.5 * x * (1.0 + jnp.tanh(0.7978845608028654 * (x + 0.044715 * (x * x * x))))


def _sigmoid(x):
    return 0.5 * jnp.tanh(0.5 * x) + 0.5


def _rms(x, nw):
    return (x * lax.rsqrt(jnp.mean(x * x, axis=-1, keepdims=True) + EPS)) * nw


def _rms_mod(x, nw, scale, shift):
    return _rms(x, nw) * (1.0 + scale) + shift


def _ada_kernel(c_ref, w_ref, b_ref, o_ref):
    c = c_ref[...]
    act = c * _sigmoid(c)
    o_ref[...] = jnp.dot(act, w_ref[...], precision=lax.Precision.HIGHEST,
                         preferred_element_type=F32) + b_ref[...]


def _ada_params(cvec, w, b):
    n = w.shape[1]
    tn = 1536
    out = pl.pallas_call(
        _ada_kernel,
        grid=(n // tn,),
        in_specs=[pl.BlockSpec((SUBLANES, D_MODEL), lambda j: (0, 0)),
                  pl.BlockSpec((D_MODEL, tn), lambda j: (0, j)),
                  pl.BlockSpec((1, tn), lambda j: (0, j))],
        out_specs=pl.BlockSpec((SUBLANES, tn), lambda j: (0, j)),
        out_shape=jax.ShapeDtypeStruct((SUBLANES, n), F32),
        compiler_params=_cparams(("parallel",)),
        name="ada_params",
    )(cvec, w, b.reshape(1, n))
    return out.reshape(SUBLANES, 6, D_MODEL)


def _proj_even_kernel(x_ref, mod_ref, nw_ref, w_ref, cos_ref, sin_ref,
                      u_ref, v_ref, q_ref, k_ref, val_ref):
    m = mod_ref[0]
    h = _rms_mod(x_ref[0], nw_ref[...], m[1:2], m[0:1]).astype(BF16)
    u_ref[0] = _gelu(_dot(h, w_ref[:, 0:SGU_WIDTH])).astype(u_ref.dtype)
    v_ref[0] = _gelu(_dot(h, w_ref[:, SGU_WIDTH:2 * SGU_WIDTH])).astype(v_ref.dtype)
    cos = cos_ref[...]
    sin = sin_ref[...]
    lane = lax.broadcasted_iota(jnp.int32, cos.shape, 1)
    first_half = (lane % 32) < ROPE_FREQS

    def rope(t):
        partner = jnp.where(first_half, pltpu.roll(t, LANES - ROPE_FREQS, 1), pltpu.roll(t, ROPE_FREQS, 1))
        return t * cos + partner * sin

    q = _dot(h, w_ref[:, 2 * SGU_WIDTH:2 * SGU_WIDTH + ATTN_WIDTH]) * (ATTN_SCALE * LOG2E)
    for g in range(ATTN_WIDTH // LANES):
        q_ref[0, :, g * LANES:(g + 1) * LANES] = rope(q[:, g * LANES:(g + 1) * LANES]).astype(BF16)
    kv = _dot(h, w_ref[:, 2 * SGU_WIDTH + ATTN_WIDTH:IN_EVEN])
    k = rope(kv[:, 0:KV_WIDTH])
    val = kv[:, KV_WIDTH:2 * KV_WIDTH]
    k_ref[0, :, 0:LANES] = k.astype(BF16)
    k_ref[0, :, LANES:2 * LANES] = pltpu.roll(k, HEAD_DIM, 1).astype(BF16)
    val_ref[0, :, 0:LANES] = val.astype(BF16)
    val_ref[0, :, LANES:2 * LANES] = pltpu.roll(val, HEAD_DIM, 1).astype(BF16)


def _proj_even(x, mod, mod_per_batch, nw, w, cos, sin, tm):
    B, S, D = x.shape
    mod_map = (lambda b, i: (b, 0, 0)) if mod_per_batch else (lambda b, i: (0, 0, 0))
    tok = lambda width: pl.BlockSpec((1, tm, width), lambda b, i: (b, i, 0))
    return pl.pallas_call(
        _proj_even_kernel,
        grid=(B, S // tm),
        in_specs=[tok(D),
                  pl.BlockSpec((1, 6, D), mod_map),
                  pl.BlockSpec((1, D), lambda b, i: (0, 0)),
                  pl.BlockSpec((D, IN_EVEN), lambda b, i: (0, 0)),
                  pl.BlockSpec((tm, LANES), lambda b, i: (i, 0)),
                  pl.BlockSpec((tm, LANES), lambda b, i: (i, 0))],
        out_specs=[tok(SGU_WIDTH), tok(SGU_WIDTH), tok(ATTN_WIDTH), tok(2 * KV_WIDTH), tok(2 * KV_WIDTH)],
        out_shape=[jax.ShapeDtypeStruct((B, S, SGU_WIDTH), BF16),
                   jax.ShapeDtypeStruct((B, S, SGU_WIDTH), BF16),
                   jax.ShapeDtypeStruct((B, S, ATTN_WIDTH), BF16),
                   jax.ShapeDtypeStruct((B, S, 2 * KV_WIDTH), BF16),
                   jax.ShapeDtypeStruct((B, S, 2 * KV_WIDTH), BF16)],
        compiler_params=_cparams(("parallel", "parallel")),
        name="proj_even",
    )(x, mod, nw, w, cos, sin)


def _mixer_even_kernel(sink_ref, x_ref, mod_ref, u_ref, v_ref, q_ref, k_ref, val_ref, kc_ref, vc_ref,
                       ws_ref, bs_ref, wout_ref, bias_ref, o_ref, mix_ref, s_ref, p_ref, inv_ref,
                       *, seq_len, is_ctx):
    tq = x_ref.shape[1]
    n_chunks = tq // CHUNK
    i = pl.program_id(1)
    nk = kc_ref.shape[1] + (0 if is_ctx else 3 * ATTN_BLOCK)
    lane = lax.broadcasted_iota(jnp.int32, (1, LANES), 1)
    lo = lane < HEAD_DIM
    zero = jnp.zeros((), BF16)

    def halves(ref_slice, kh):
        nat, swp = ref_slice[:, 0:LANES], ref_slice[:, LANES:2 * LANES]
        if kh == 0:
            return jnp.where(lo, nat, zero), jnp.where(lo, zero, swp)
        return jnp.where(lo, swp, zero), jnp.where(lo, zero, nat)

    kc_all = kc_ref[0]
    vc_all = vc_ref[0]

    def chunk_body(c, carry):
        r0 = pl.multiple_of(c * CHUNK, CHUNK)
        rows = pl.ds(r0, CHUNK)
        vch = v_ref[0, rows, :].astype(F32)
        uch = u_ref[0, rows, :].astype(F32)
        for g in range(SGU_GROUPS):
            cols = slice(g * LANES, (g + 1) * LANES)
            vg = vch[:, cols]
            dev = vg - jnp.mean(vg, axis=-1, keepdims=True)
            vn = dev * lax.rsqrt(jnp.mean(dev * dev, axis=-1, keepdims=True) + EPS)
            mixed = _dot(ws_ref[g], vn.astype(BF16)) + bs_ref[:, g:g + 1]
            mix_ref[rows, cols] = (uch[:, cols] * mixed).astype(BF16)
        qch = q_ref[0, rows, :]
        if not is_ctx:
            blk = i * n_chunks + c
            n_blk = seq_len // ATTN_BLOCK
            start = pl.multiple_of(jnp.clip((blk - 1) * ATTN_BLOCK, 0, seq_len - 3 * ATTN_BLOCK), ATTN_BLOCK)
            k3 = k_ref[0, pl.ds(start, 3 * ATTN_BLOCK), :]
            v3 = val_ref[0, pl.ds(start, 3 * ATTN_BLOCK), :]
            case = jnp.where(blk == 0, 0, jnp.where(blk == n_blk - 1, 2, 1))
        for kh in range(N_KV_HEADS):
            kc_lo, kc_hi = halves(kc_all, kh)
            vc_lo, vc_hi = halves(vc_all, kh)
            if is_ctx:
                k_cat = jnp.concatenate([kc_lo, kc_hi], axis=0)
                v_cat = (vc_lo, vc_hi)
            else:
                k_lo, k_hi = halves(k3, kh)
                v_lo, v_hi = halves(v3, kh)
                k_cat = jnp.concatenate([k_lo, kc_lo, k_hi, kc_hi], axis=0)
                v_cat = (jnp.concatenate([v_lo, vc_lo], axis=0), jnp.concatenate([v_hi, vc_hi], axis=0))
            q2 = jnp.concatenate([qch[:, 2 * kh * LANES:(2 * kh + 1) * LANES],
                                  qch[:, (2 * kh + 1) * LANES:(2 * kh + 2) * LANES]], axis=0)
            s_ref[:, 0:2 * nk] = _dot_nt(q2, k_cat)
            for half in range(2):
                for rb in range(2 * ATTN_BLOCK // ROW_BLOCK):
                    rsl = slice(rb * ROW_BLOCK, (rb + 1) * ROW_BLOCK)
                    snk = sink_ref[2 * (2 * kh + rb * ROW_BLOCK // ATTN_BLOCK) + half]
                    s = s_ref[rsl, half * nk:(half + 1) * nk]
                    if not is_ctx:
                        qoff = (rb * ROW_BLOCK) % ATTN_BLOCK
                        s_loc = s[:, 0:3 * ATTN_BLOCK] + bias_ref[case, qoff:qoff + ROW_BLOCK, :]
                        s = jnp.concatenate([s_loc, s[:, 3 * ATTN_BLOCK:]], axis=1)
                    m = jnp.maximum(jnp.max(s, axis=-1, keepdims=True), snk)
                    p = jnp.exp2(s - m)
                    den = jnp.sum(p, axis=-1, keepdims=True) + jnp.exp2(snk - m)
                    p_ref[rsl, half * nk:(half + 1) * nk] = p.astype(BF16)
                    inv_ref[rsl, half * LANES:(half + 1) * LANES] = jnp.broadcast_to(1.0 / den, (ROW_BLOCK, LANES))
            o_lo = _dot(p_ref[:, 0:nk], v_cat[0])
            o_hi = _dot(p_ref[:, nk:2 * nk], v_cat[1])
            acc = (o_lo * inv_ref[:, 0:LANES] + o_hi * inv_ref[:, LANES:2 * LANES]).astype(BF16)
            for g in range(2):
                col = SGU_WIDTH + (2 * kh + g) * LANES
                mix_ref[rows, col:col + LANES] = acc[g * ATTN_BLOCK:(g + 1) * ATTN_BLOCK]
        return carry

    lax.fori_loop(0, n_chunks, chunk_body, 0)
    y = _dot(mix_ref[...], wout_ref[...])
    o_ref[0] = x_ref[0] + mod_ref[0][2:3] * y


def _window_bias():
    case = jnp.arange(3, dtype=jnp.int32)[:, None, None]
    qi = jnp.arange(ATTN_BLOCK, dtype=jnp.int32)[None, :, None]
    kj = jnp.arange(3 * ATTN_BLOCK, dtype=jnp.int32)[None, None, :]
    return jnp.where(jnp.abs(kj - case * ATTN_BLOCK - qi) <= WINDOW, 0.0, NEG_INF).astype(F32)


def _mixer_even(x, mod, mod_per_batch, u, v, q, k2, v2, kc2, vc2, ws, bs_t, sink, wout, tq, is_ctx):
    B, S, D = x.shape
    Sk = k2.shape[1]
    Lc = kc2.shape[1]
    nk = Lc + (0 if is_ctx else 3 * ATTN_BLOCK)
    mod_map = (lambda b, i: (b, 0, 0)) if mod_per_batch else (lambda b, i: (0, 0, 0))
    tok = lambda width: pl.BlockSpec((1, tq, width), lambda b, i: (b, i, 0))
    per_batch = lambda rows: pl.BlockSpec((1, rows, 2 * KV_WIDTH), lambda b, i: (b, 0, 0))
    return pl.pallas_call(
        functools.partial(_mixer_even_kernel, seq_len=S, is_ctx=is_ctx),
        grid=(B, S // tq),
        in_specs=[pl.BlockSpec(memory_space=pltpu.SMEM),
                  tok(D),
                  pl.BlockSpec((1, 6, D), mod_map),
                  tok(SGU_WIDTH), tok(SGU_WIDTH), tok(ATTN_WIDTH),
                  per_batch(Sk), per_batch(Sk), per_batch(Lc), per_batch(Lc),
                  pl.BlockSpec((SGU_GROUPS, CHUNK, CHUNK), lambda b, i: (0, 0, 0)),
                  pl.BlockSpec((CHUNK, SGU_GROUPS), lambda b, i: (0, 0)),
                  pl.BlockSpec((D, D), lambda b, i: (0, 0)),
                  pl.BlockSpec((3, ATTN_BLOCK, 3 * ATTN_BLOCK), lambda b, i: (0, 0, 0))],
        out_specs=tok(D),
        out_shape=jax.ShapeDtypeStruct((B, S, D), F32),
        scratch_shapes=[pltpu.VMEM((tq, D), BF16),
                        pltpu.VMEM((2 * ATTN_BLOCK, 2 * nk), F32),
                        pltpu.VMEM((2 * ATTN_BLOCK, 2 * nk), BF16),
                        pltpu.VMEM((2 * ATTN_BLOCK, 2 * LANES), F32)],
        compiler_params=_cparams(("parallel", "arbitrary")),
        name="mixer_ctx" if is_ctx else "mixer_even",
    )(sink, x, mod, u, v, q, k2, v2, kc2, vc2, ws, bs_t, wout, _window_bias())


def _route_kernel(x_ref, mod_ref, nw_ref, rw_ref, hn_ref, meta_ref, pos_ref, cnt_ref, tb_ref, base_ref, *, cap):
    @pl.when(jnp.logical_and(pl.program_id(0) == 0, pl.program_id(1) == 0))
    def _():
        base_ref[...] = jnp.zeros_like(base_ref)

    m = mod_ref[0]
    h = _rms_mod(x_ref[0], nw_ref[...], m[4:5], m[3:4])
    hn_ref[0] = h
    w = rw_ref[...]
    w_hi = w.astype(BF16)
    w_lo = (w - w_hi.astype(F32)).astype(BF16)
    h_hi = h.astype(BF16)
    h_lo = (h - h_hi.astype(F32)).astype(BF16)
    logits = _dot(h_hi, w_hi) + (_dot(h_lo, w_hi) + _dot(h_hi, w_lo))
    tm = logits.shape[0]
    lane = lax.broadcasted_iota(jnp.int32, logits.shape, 1)
    lg = jnp.where(lane < N_EXPERTS, logits, -jnp.inf)
    m1 = jnp.max(lg, axis=-1, keepdims=True)
    i1 = jnp.min(jnp.where(lg == m1, lane, LANES), axis=-1, keepdims=True)
    lg2 = jnp.where(lane == i1, -jnp.inf, lg)
    m2 = jnp.max(lg2, axis=-1, keepdims=True)
    i2 = jnp.min(jnp.where(lg2 == m2, lane, LANES), axis=-1, keepdims=True)
    e2 = jnp.exp(m2 - m1)
    den = 1.0 + e2
    hot = jnp.where(jnp.logical_or(lane == i1, lane == i2), 1.0, 0.0)
    r = lax.broadcasted_iota(jnp.int32, (tm, tm), 0)
    c = lax.broadcasted_iota(jnp.int32, (tm, tm), 1)
    before = jnp.where(r > c, 1.0, 0.0).astype(BF16)
    tot = base_ref[...] + _dot(before, hot.astype(BF16))
    rank1 = jnp.sum(jnp.where(lane == i1, tot, 0.0), axis=-1, keepdims=True)
    rank2 = jnp.sum(jnp.where(lane == i2, tot, 0.0), axis=-1, keepdims=True)
    pos1 = i1.astype(F32) * cap + rank1
    pos2 = i2.astype(F32) * cap + rank2
    meta = (jnp.where(lane == 0, pos1, 0.0) + jnp.where(lane == 1, pos2, 0.0)
            + jnp.where(lane == 2, 1.0 / den, 0.0) + jnp.where(lane == 3, e2 / den, 0.0)
            + jnp.where(lane == 4, i1.astype(F32), 0.0) + jnp.where(lane == 5, i2.astype(F32), 0.0))
    meta_ref[0] = meta
    pos_ref[0, 0] = meta.T[0:SUBLANES].astype(jnp.int32)
    sub = lax.broadcasted_iota(jnp.int32, (SUBLANES, LANES), 0)
    tb = jnp.zeros((SUBLANES, LANES), F32)
    for j in range(tm // COMBINE_TILE):
        tb = jnp.where(sub == j, tot[j * COMBINE_TILE:j * COMBINE_TILE + 1, :], tb)
    tb_ref[0, 0] = tb
    base_ref[...] += jnp.sum(hot, axis=0, keepdims=True)
    cnt_ref[...] = jnp.broadcast_to(base_ref[...], cnt_ref.shape)


def _route(x, mod, nw, rw, tm, cap):
    B, S, D = x.shape
    return pl.pallas_call(
        functools.partial(_route_kernel, cap=float(cap)),
        grid=(B, S // tm),
        in_specs=[pl.BlockSpec((1, tm, D), lambda b, i: (b, i, 0)),
                  pl.BlockSpec((1, 6, D), lambda b, i: (b, 0, 0)),
                  pl.BlockSpec((1, D), lambda b, i: (0, 0)),
                  pl.BlockSpec((D, LANES), lambda b, i: (0, 0))],
        out_specs=[pl.BlockSpec((1, tm, D), lambda b, i: (b, i, 0)),
                   pl.BlockSpec((1, tm, LANES), lambda b, i: (b, i, 0)),
                   pl.BlockSpec((1, 1, SUBLANES, tm), lambda b, i: (b, i, 0, 0)),
                   pl.BlockSpec((SUBLANES, LANES), lambda b, i: (0, 0)),
                   pl.BlockSpec((1, 1, SUBLANES, LANES), lambda b, i: (b, i, 0, 0))],
        out_shape=[jax.ShapeDtypeStruct((B, S, D), F32),
                   jax.ShapeDtypeStruct((B, S, LANES), F32),
                   jax.ShapeDtypeStruct((B, S // tm, SUBLANES, tm), jnp.int32),
                   jax.ShapeDtypeStruct((SUBLANES, LANES), F32),
                   jax.ShapeDtypeStruct((B, S // tm, SUBLANES, LANES), F32)],
        scratch_shapes=[pltpu.VMEM((1, LANES), F32)],
        compiler_params=_cparams(("arbitrary", "arbitrary")),
        name="moe_route",
    )(x, mod, nw, rw)


def _dispatch_kernel(pos_ref, cnt_ref, hn_ref, xs_ref, zero_ref, sem, zsem, *, n_tok, cap, tg):
    i = pl.program_id(0)
    td = hn_ref.shape[0]

    @pl.when(i == 0)
    def _():
        zero_ref[...] = jnp.zeros_like(zero_ref)

        def tail_copy(e):
            start = pl.multiple_of(e * cap + (cnt_ref[e] // SUBLANES) * SUBLANES, SUBLANES)
            return pltpu.make_async_copy(zero_ref, xs_ref.at[pl.ds(start, tg)], zsem)

        for e in range(N_EXPERTS):
            tail_copy(e).start()
        for e in range(N_EXPERTS):
            tail_copy(e).wait()

    def tok_body(t, c):
        tok = i * td + t
        pltpu.make_async_copy(hn_ref.at[pl.ds(t, 1)], xs_ref.at[pl.ds(pos_ref[tok], 1)], sem).start()
        pltpu.make_async_copy(hn_ref.at[pl.ds(t, 1)], xs_ref.at[pl.ds(pos_ref[n_tok + tok], 1)], sem).start()
        return c

    lax.fori_loop(0, td, tok_body, 0, unroll=8)
    for _ in range(2):
        pltpu.make_async_copy(hn_ref, hn_ref, sem).wait()


def _dispatch(pos, cnt, hn, cap, tg, td):
    T, D = hn.shape
    return pl.pallas_call(
        functools.partial(_dispatch_kernel, n_tok=T, cap=cap, tg=tg),
        grid_spec=pltpu.PrefetchScalarGridSpec(
            num_scalar_prefetch=2,
            grid=(T // td,),
            in_specs=[pl.BlockSpec((td, D), lambda i, pos, cnt: (i, 0))],
            out_specs=pl.BlockSpec(memory_space=pl.ANY),
            scratch_shapes=[pltpu.VMEM((tg, D), F32), pltpu.SemaphoreType.DMA, pltpu.SemaphoreType.DMA]),
        out_shape=jax.ShapeDtypeStruct((N_EXPERTS * cap, D), F32),
        compiler_params=_cparams(("arbitrary",)),
        name="moe_dispatch",
    )(pos, cnt, hn)


def _moe_group_kernel(blk_ref, exp_ref, fa_ref, fb_ref, nact_ref, x_ref, w1_ref, w3_ref, w2_ref, o_ref,
                      h_ref, acc_ref):
    t = pl.program_id(0)
    f = pl.program_id(1)

    @pl.when(t < nact_ref[0])
    def _():
        @pl.when(f == 0)
        def _():
            h_ref[...] = x_ref[...].astype(BF16)

        h = h_ref[...]
        a = _dot(h, w1_ref[0])
        b = _dot(h, w3_ref[0])
        y = _dot(((a * _sigmoid(a)) * b).astype(BF16), w2_ref[0])

        @pl.when(f == 0)
        def _():
            acc_ref[...] = y

        @pl.when(f == 1)
        def _():
            o_ref[...] = (acc_ref[...] + y).astype(o_ref.dtype)


def _moe_group(blk, exp, fa, fb, nact, xs, w1, w3, w2, tg):
    R, D = xs.shape
    F = w1.shape[2]
    tf = F // 2
    fsel = lambda f, fa, fb, t: jnp.where(f == 0, fa[t], fb[t])
    return pl.pallas_call(
        _moe_group_kernel,
        grid_spec=pltpu.PrefetchScalarGridSpec(
            num_scalar_prefetch=5,
            grid=(blk.shape[0], 2),
            in_specs=[pl.BlockSpec((tg, D), lambda t, f, blk, exp, fa, fb, na: (blk[t], 0)),
                      pl.BlockSpec((1, D, tf), lambda t, f, blk, exp, fa, fb, na: (exp[t], 0, fsel(f, fa, fb, t))),
                      pl.BlockSpec((1, D, tf), lambda t, f, blk, exp, fa, fb, na: (exp[t], 0, fsel(f, fa, fb, t))),
                      pl.BlockSpec((1, tf, D), lambda t, f, blk, exp, fa, fb, na: (exp[t], fsel(f, fa, fb, t), 0))],
            out_specs=pl.BlockSpec((tg, D), lambda t, f, blk, exp, fa, fb, na: (blk[t], 0)),
            scratch_shapes=[pltpu.VMEM((tg, D), BF16), pltpu.VMEM((tg, D), F32)]),
        out_shape=jax.ShapeDtypeStruct((R, D), BF16),
        compiler_params=_cparams(("arbitrary", "arbitrary")),
        name="moe_experts",
    )(blk, exp, fa, fb, nact, xs, w1, w3, w2)


def _combine_kernel(tb_ref, x_ref, mod_ref, meta_ref, fn_ref, y_ref, o_ref, buf_ref, sem, *, cap):
    i = pl.program_id(0)
    n = pl.num_programs(0)
    tc = x_ref.shape[0]
    n_rows = buf_ref.shape[1]
    shift = COMBINE_CHUNK.bit_length() - 1

    def segments(tile):
        segs, off = [], 0
        for e in range(N_EXPERTS):
            first = tb_ref[tile * N_EXPERTS + e]
            count = tb_ref[(tile + 1) * N_EXPERTS + e] - first
            lead = first & (COMBINE_CHUNK - 1)
            n_chunks = jnp.where(count > 0, lax.shift_right_logical(lead + count + COMBINE_CHUNK - 1, shift), 0)
            segs.append((e * cap + first - lead, n_chunks, off))
            off = off + n_chunks * COMBINE_CHUNK
        return segs

    def chunk_copy(src_row, dst_row, slot):
        return pltpu.make_async_copy(y_ref.at[pl.ds(pl.multiple_of(src_row, COMBINE_CHUNK), COMBINE_CHUNK)],
                                     buf_ref.at[slot, pl.ds(pl.multiple_of(dst_row, COMBINE_CHUNK), COMBINE_CHUNK)],
                                     sem.at[slot])

    def for_each_chunk(tile, slot, action):
        for src, n_chunks, off in segments(tile):
            def body(c, carry):
                action(chunk_copy(src + c * COMBINE_CHUNK, off + c * COMBINE_CHUNK, slot))
                return carry

            lax.fori_loop(0, n_chunks, body, 0)

    @pl.when(i == 0)
    def _():
        buf_ref[...] = jnp.zeros_like(buf_ref)
        for_each_chunk(0, 0, lambda cp: cp.start())

    slot = i % 2

    @pl.when(i + 1 < n)
    def _():
        for_each_chunk(i + 1, 1 - slot, lambda cp: cp.start())

    for_each_chunk(i, slot, lambda cp: cp.wait())
    meta = meta_ref[...]
    segs = segments(i)

    def one_hot(pos, exp):
        delta = jnp.zeros_like(pos)
        for e, (src, _, off) in enumerate(segs):
            delta = jnp.where(exp == e, off - src, delta)
        col = lax.broadcasted_iota(jnp.int32, (tc, n_rows), 1)
        return jnp.where(col == pos + delta, 1.0, 0.0).astype(BF16)

    as_int = lambda lane: meta[:, lane:lane + 1].astype(jnp.int32)
    picks = jnp.concatenate([one_hot(as_int(0), as_int(4)), one_hot(as_int(1), as_int(5))], axis=0)
    y12 = _dot(picks, buf_ref[slot])
    mix = meta[:, 2:3] * y12[0:tc] + meta[:, 3:4] * y12[tc:2 * tc]
    o_ref[...] = _rms(x_ref[...] + mod_ref[0][5:6] * mix, fn_ref[...])


def _combine(tile_base, x, mod, meta, fin, y, seq_len, cap):
    T, D = x.shape
    tc = COMBINE_TILE
    per_batch = seq_len // tc
    n_rows = -(-(2 * tc + 2 * N_EXPERTS * (COMBINE_CHUNK - 1)) // LANES) * LANES
    return pl.pallas_call(
        functools.partial(_combine_kernel, cap=cap),
        grid_spec=pltpu.PrefetchScalarGridSpec(
            num_scalar_prefetch=1,
            grid=(T // tc,),
            in_specs=[pl.BlockSpec((tc, D), lambda i, tb: (i, 0)),
                      pl.BlockSpec((1, 6, D), lambda i, tb: (i // per_batch, 0, 0)),
                      pl.BlockSpec((tc, LANES), lambda i, tb: (i, 0)),
                      pl.BlockSpec((1, D), lambda i, tb: (0, 0)),
                      pl.BlockSpec(memory_space=pl.ANY)],
            out_specs=pl.BlockSpec((tc, D), lambda i, tb: (i, 0)),
            scratch_shapes=[pltpu.VMEM((2, n_rows, D), BF16), pltpu.SemaphoreType.DMA((2,))]),
        out_shape=jax.ShapeDtypeStruct((T, D), F32),
        compiler_params=_cparams(("arbitrary",)),
        name="moe_combine",
    )(tile_base, x, mod, meta, fin, y)


def _moe_tiles(cnt, n_tiles, tg, cap):
    per = (cnt + tg - 1) // tg
    cum = jnp.cumsum(per)
    nact = cum[-1]
    t = jnp.arange(n_tiles, dtype=jnp.int32)
    tt = jnp.minimum(t, nact - 1)
    exp = jnp.minimum(jnp.sum((tt[:, None] >= cum[None, :]).astype(jnp.int32), axis=1), N_EXPERTS - 1)
    blk = exp * (cap // tg) + tt - (cum - per)[exp]
    odd = tt % 2
    fa = jnp.where(t < nact, odd, 1 - odd)
    fb = 1 - odd
    i32 = lambda v: v.astype(jnp.int32)
    return i32(blk), i32(exp), i32(fa), i32(fb), i32(nact.reshape(1))


def _ffn_kernel(*refs, n_cast):
    x_ref, mod_ref, nw_ref, w1_ref, w3_ref, w2_ref = refs[0:6]
    cast_in = refs[6:6 + n_cast]
    o_ref = refs[6 + n_cast]
    cast_out = refs[7 + n_cast:7 + 2 * n_cast]
    h_ref, acc_ref = refs[7 + 2 * n_cast:]
    f = pl.program_id(2)
    for src, dst in zip(cast_in, cast_out):
        dst[...] = src[...].astype(dst.dtype)

    @pl.when(f == 0)
    def _():
        m = mod_ref[0]
        h_ref[...] = _rms_mod(x_ref[0], nw_ref[...], m[4:5], m[3:4]).astype(BF16)
        acc_ref[...] = jnp.zeros_like(acc_ref)

    h = h_ref[...]
    a = _dot(h, w1_ref[...])
    b = _dot(h, w3_ref[...])
    acc_ref[...] += _dot(((a * _sigmoid(a)) * b).astype(BF16), w2_ref[...])

    @pl.when(f == pl.num_programs(2) - 1)
    def _():
        o_ref[0] = x_ref[0] + mod_ref[0][5:6] * acc_ref[...]


def _ffn(x, mod, mod_per_batch, nw, w1, w3, w2, tm, tf, cast=()):
    B, S, D = x.shape
    F = w1.shape[1]
    n_i, n_f = S // tm, F // tf
    n_steps = B * n_i * n_f
    mod_map = (lambda b, i, f: (b, 0, 0)) if mod_per_batch else (lambda b, i, f: (0, 0, 0))
    cast_specs, cast_shapes = [], []
    for arr in cast:
        E, R, C = arr.shape
        per = n_steps // E
        assert per * E == n_steps and R % (per * 2 * SUBLANES) == 0, (arr.shape, n_steps)

        def cast_map(b, i, f, per=per):
            step = (b * n_i + i) * n_f + f
            return step // per, step % per, 0

        cast_specs.append(pl.BlockSpec((1, R // per, C), cast_map))
        cast_shapes.append(jax.ShapeDtypeStruct(arr.shape, BF16))
    tok = pl.BlockSpec((1, tm, D), lambda b, i, f: (b, i, 0))
    outs = pl.pallas_call(
        functools.partial(_ffn_kernel, n_cast=len(cast)),
        grid=(B, n_i, n_f),
        in_specs=[tok,
                  pl.BlockSpec((1, 6, D), mod_map),
                  pl.BlockSpec((1, D), lambda b, i, f: (0, 0)),
                  pl.BlockSpec((D, tf), lambda b, i, f: (0, f)),
                  pl.BlockSpec((D, tf), lambda b, i, f: (0, f)),
                  pl.BlockSpec((tf, D), lambda b, i, f: (f, 0))] + cast_specs,
        out_specs=[tok] + cast_specs,
        out_shape=[jax.ShapeDtypeStruct((B, S, D), F32)] + cast_shapes,
        scratch_shapes=[pltpu.VMEM((tm, D), BF16), pltpu.VMEM((tm, D), F32)],
        compiler_params=_cparams(("parallel", "parallel", "arbitrary")),
        name="ffn",
    )(x, mod, nw, w1, w3, w2, *cast)
    return outs[0], tuple(outs[1:])


def _proj_odd_kernel(x_ref, mod_ref, nw_ref, w_ref, gate_ref, rec_ref):
    m = mod_ref[0]
    h = _rms_mod(x_ref[0], nw_ref[...], m[1:2], m[0:1]).astype(BF16)
    gate_ref[0] = _gelu(_dot(h, w_ref[:, 0:D_RNN])).astype(gate_ref.dtype)
    rec_ref[0] = _dot(h, w_ref[:, D_RNN:2 * D_RNN])


def _proj_odd(x, mod, mod_per_batch, nw, w, tm):
    B, S, D = x.shape
    mod_map = (lambda b, i: (b, 0, 0)) if mod_per_batch else (lambda b, i: (0, 0, 0))
    tok = lambda width: pl.BlockSpec((1, tm, width), lambda b, i: (b, i, 0))
    return pl.pallas_call(
        _proj_odd_kernel,
        grid=(B, S // tm),
        in_specs=[tok(D),
                  pl.BlockSpec((1, 6, D), mod_map),
                  pl.BlockSpec((1, D), lambda b, i: (0, 0)),
                  pl.BlockSpec((D, 2 * D_RNN), lambda b, i: (0, 0))],
        out_specs=[tok(D_RNN), tok(D_RNN)],
        out_shape=[jax.ShapeDtypeStruct((B, S, D_RNN), BF16), jax.ShapeDtypeStruct((B, S, D_RNN), F32)],
        compiler_params=_cparams(("parallel", "parallel")),
        name="proj_odd",
    )(x, mod, nw, w)


def _scan8(a, b, h, row, reverse):
    for s in (1, 2, 4):
        if reverse:
            a_s, b_s, live = pltpu.roll(a, SUBLANES - s, 0), pltpu.roll(b, SUBLANES - s, 0), row < SUBLANES - s
        else:
            a_s, b_s, live = pltpu.roll(a, s, 0), pltpu.roll(b, s, 0), row >= s
        b = jnp.where(live, a * b_s + b, b)
        a = jnp.where(live, a * a_s, a)
    hr = a * h + b
    return hr, (hr[0:1] if reverse else hr[SUBLANES - 1:SUBLANES])


def _lru_kernel(rec_ref, recc_ref, cw_ref, cb_ref, wa_ref, ba_ref, wx_ref, bx_ref, lam_ref,
                s_ref, pad_ref, a_ref, b_ref, cpad_ref, ca_ref, cbb_ref, *, tile):
    S = rec_ref.shape[1]
    L = recc_ref.shape[1]
    cw = cw_ref[...]
    cb = cb_ref[...]
    lam = lam_ref[...]
    sp = jnp.maximum(-lam, 0.0) + jnp.log1p(jnp.exp(-jnp.abs(lam)))
    zeros8 = jnp.zeros((SUBLANES, LANES), F32)

    def coefficients(src_ref, dst_a, dst_b, n_rows, t):
        pad = cpad_ref if src_ref is recc_ref else pad_ref
        pad[0:SUBLANES, :] = zeros8
        pad[SUBLANES + n_rows:2 * SUBLANES + n_rows, :] = zeros8

        def copy(j, carry):
            r0 = pl.multiple_of(j * t, t)
            pad[pl.ds(SUBLANES + r0, t), :] = src_ref[0, pl.ds(r0, t), :]
            return carry

        lax.fori_loop(0, n_rows // t, copy, 0)

        def body(j, carry):
            r0 = pl.multiple_of(j * t, t)
            ext = pad[pl.ds(r0, t + 2 * SUBLANES), :]
            conv = cb
            for tap in range(4):
                conv = conv + cw[tap:tap + 1] * ext[SUBLANES - 2 + tap:SUBLANES - 2 + tap + t]
            cbf = conv.astype(BF16)
            for d in range(2):
                r = _sigmoid(_dot(cbf, wa_ref[d, 0]) + ba_ref[d:d + 1])
                gi = _sigmoid(_dot(cbf, wx_ref[d, 0]) + bx_ref[d:d + 1])
                log_a = -LRU_C * r * sp[d:d + 1]
                a = jnp.exp(log_a)
                y = -jnp.tanh(0.5 * log_a)
                root = jnp.where(y > 0.0, y * lax.rsqrt(y), 0.0)
                dst_a[d, pl.ds(r0, t), :] = a
                dst_b[d, pl.ds(r0, t), :] = (root * (1.0 + a)) * (gi * conv)
            return carry

        lax.fori_loop(0, n_rows // t, body, 0)

    row = lax.broadcasted_iota(jnp.int32, (SUBLANES, LANES), 0)
    h_zero = jnp.zeros((1, LANES), F32)

    coefficients(recc_ref, ca_ref, cbb_ref, L, L)
    nc = L // SUBLANES

    def ctx_body(j, carry):
        hf, hb = carry
        rf = pl.multiple_of(j * SUBLANES, SUBLANES)
        rb = pl.multiple_of((nc - 1 - j) * SUBLANES, SUBLANES)
        _, hf = _scan8(ca_ref[0, pl.ds(rf, SUBLANES), :], cbb_ref[0, pl.ds(rf, SUBLANES), :], hf, row, False)
        _, hb = _scan8(ca_ref[1, pl.ds(rb, SUBLANES), :], cbb_ref[1, pl.ds(rb, SUBLANES), :], hb, row, True)
        return hf, hb

    h0f, h0b = lax.fori_loop(0, nc, ctx_body, (h_zero, h_zero))

    coefficients(rec_ref, a_ref, b_ref, S, tile)
    n = S // SUBLANES

    def lat_body(accumulate):
        def body(j, carry):
            hf, hb = carry
            rf = pl.multiple_of(j * SUBLANES, SUBLANES)
            rb = pl.multiple_of((n - 1 - j) * SUBLANES, SUBLANES)
            of, hf = _scan8(a_ref[0, pl.ds(rf, SUBLANES), :], b_ref[0, pl.ds(rf, SUBLANES), :], hf, row, False)
            ob, hb = _scan8(a_ref[1, pl.ds(rb, SUBLANES), :], b_ref[1, pl.ds(rb, SUBLANES), :], hb, row, True)
            if accumulate:
                s_ref[0, pl.ds(rf, SUBLANES), :] += of
                s_ref[0, pl.ds(rb, SUBLANES), :] += ob
            else:
                s_ref[0, pl.ds(rf, SUBLANES), :] = of
                s_ref[0, pl.ds(rb, SUBLANES), :] = ob
            return hf, hb
        return body

    mid = lax.fori_loop(0, n // 2, lat_body(False), (h0f, h0b), unroll=8)
    lax.fori_loop(n // 2, n, lat_body(True), mid, unroll=8)


def _lru(rec, rec_c, conv_w, conv_b, wa, ba, wx, bx, lam, tile):
    B, S, _ = rec.shape
    L = rec_c.shape[1]
    blk = lambda rows: pl.BlockSpec((1, rows, LRU_BLOCK), lambda b, j: (b, 0, j))
    vec = lambda rows: pl.BlockSpec((rows, LRU_BLOCK), lambda b, j: (0, j))
    wspec = pl.BlockSpec((2, 1, LRU_BLOCK, LRU_BLOCK), lambda b, j: (0, j, 0, 0))
    return pl.pallas_call(
        functools.partial(_lru_kernel, tile=tile),
        grid=(B, LRU_BLOCKS),
        in_specs=[blk(S), blk(L), vec(4), vec(1), wspec, vec(2), wspec, vec(2), vec(2)],
        out_specs=blk(S),
        out_shape=jax.ShapeDtypeStruct((B, S, D_RNN), F32),
        scratch_shapes=[pltpu.VMEM((S + 2 * SUBLANES, LRU_BLOCK), F32),
                        pltpu.VMEM((2, S, LRU_BLOCK), F32),
                        pltpu.VMEM((2, S, LRU_BLOCK), F32),
                        pltpu.VMEM((L + 2 * SUBLANES, LRU_BLOCK), F32),
                        pltpu.VMEM((2, L, LRU_BLOCK), F32),
                        pltpu.VMEM((2, L, LRU_BLOCK), F32)],
        compiler_params=_cparams(("parallel", "parallel")),
        name="lru_scan",
    )(rec, rec_c, conv_w, conv_b, wa, ba, wx, bx, lam)


def _lru_out_kernel(x_ref, mod_ref, gate_ref, s_ref, w_ref, o_ref):
    y = (gate_ref[0].astype(F32) * s_ref[0]).astype(BF16)
    o_ref[0] = x_ref[0] + mod_ref[0][2:3] * _dot(y, w_ref[...])


def _lru_out(x, mod, gate, s, w, tm):
    B, S, D = x.shape
    tok = lambda width: pl.BlockSpec((1, tm, width), lambda b, i: (b, i, 0))
    return pl.pallas_call(
        _lru_out_kernel,
        grid=(B, S // tm),
        in_specs=[tok(D), pl.BlockSpec((1, 6, D), lambda b, i: (b, 0, 0)), tok(D_RNN), tok(D_RNN),
                  pl.BlockSpec((D_RNN, D), lambda b, i: (0, 0))],
        out_specs=tok(D),
        out_shape=jax.ShapeDtypeStruct((B, S, D), F32),
        compiler_params=_cparams(("parallel", "parallel")),
        name="lru_out",
    )(x, mod, gate, s, w)


def _rope_tables(n_tok):
    rows = n_tok // GRID_W
    row = jnp.repeat(jnp.arange(rows, dtype=F32), GRID_W)
    col = jnp.tile(jnp.arange(GRID_W, dtype=F32), rows)
    freqs = ROPE_BASE ** (-jnp.arange(ROPE_FREQS, dtype=F32) / ROPE_FREQS)
    ar, ac = row[:, None] * freqs, col[:, None] * freqs
    cos = jnp.concatenate([jnp.cos(ar), jnp.cos(ar), jnp.cos(ac), jnp.cos(ac)], axis=-1)
    sin = jnp.concatenate([-jnp.sin(ar), jnp.sin(ar), -jnp.sin(ac), jnp.sin(ac)], axis=-1)
    return jnp.tile(cos, (1, LANES // HEAD_DIM)), jnp.tile(sin, (1, LANES // HEAD_DIM))


def kernel(x, c, ctx, c_ctx, ada_w_e, ada_b_e, norm1_e, norm2_e, w_in_e, sgu_w, sgu_b, attn_sink, w_out_e, ffn_w1, ffn_w3, ffn_w2, ada_w_o, ada_b_o, norm1_o, norm2_o, w_in_o, conv_w, conv_b, lru_wa, lru_ba, lru_wx, lru_bx, lru_lambda, w_out_o, router_w, moe_w1, moe_w3, moe_w2, final_norm):
    B, S, D = x.shape
    L = ctx.shape[1]
    cvec = jnp.concatenate([c, c_ctx[None], jnp.zeros((SUBLANES - B - 1, D), F32)], axis=0)
    mod_e = _ada_params(cvec, ada_w_e[0], ada_b_e[0])
    mod_o = _ada_params(cvec, ada_w_o[0], ada_b_o[0])
    lat_e, ctx_e = mod_e[0:B], mod_e[B:B + 1]
    lat_o, ctx_o = mod_o[0:B], mod_o[B:B + 1]
    bf = lambda t: t.astype(BF16)
    row = lambda t: t.reshape(1, -1)

    cos, sin = _rope_tables(S)
    cos_c, sin_c = jnp.ones((L, LANES), F32), jnp.zeros((L, LANES), F32)
    w_in = bf(w_in_e[0])
    n1, n2 = row(norm1_e[0]), row(norm2_e[0])
    uc, vc, qc, kc2, vc2 = _proj_even(ctx, ctx_e, False, n1, w_in, cos_c, sin_c, L)
    u, v, q, k2, v2 = _proj_even(x, lat_e, True, n1, w_in, cos, sin, 1024)
    ws, bs_t, wout = bf(sgu_w[0]), sgu_b[0].T, bf(w_out_e[0])
    sink = attn_sink[0] * LOG2E
    x = _mixer_even(x, lat_e, True, u, v, q, k2, v2, kc2, vc2, ws, bs_t, sink, wout, 512, False)
    xc = _mixer_even(ctx, ctx_e, False, uc, vc, qc, kc2, vc2, kc2, vc2, ws, bs_t, sink, wout, L, True)
    w1, w3, w2 = bf(ffn_w1[0]), bf(ffn_w3[0]), bf(ffn_w2[0])
    x, moe_w = _ffn(x, lat_e, True, n2, w1, w3, w2, 512, 1408, cast=(moe_w1[0], moe_w3[0], moe_w2[0]))
    xc, _ = _ffn(xc, ctx_e, False, n2, w1, w3, w2, L, 1408)

    w_in = bf(w_in_o[0])
    n1, n2 = row(norm1_o[0]), row(norm2_o[0])
    _, rec_c = _proj_odd(xc, ctx_o, False, n1, w_in, L)
    gate, rec = _proj_odd(x, lat_o, True, n1, w_in, 1024)
    s = _lru(rec, rec_c, conv_w[0], row(conv_b[0]), bf(lru_wa[0]), lru_ba[0], bf(lru_wx[0]), lru_bx[0],
             lru_lambda[0], 512)
    x = _lru_out(x, lat_o, gate, s, bf(w_out_o[0]), 1024)
    rw = jnp.pad(router_w[0], ((0, 0), (0, LANES - N_EXPERTS)))
    T = B * S
    tg = 512
    cap = T + tg
    tm = 512
    hn, meta, pos4, cnt, tb4 = _route(x, lat_o, n2, rw, tm, cap)
    pos = jnp.transpose(pos4[:, :, 0:2, :], (2, 0, 1, 3)).reshape(2 * T)
    cnt = cnt[0, 0:N_EXPERTS].astype(jnp.int32)
    tile_base = jnp.concatenate([tb4[:, :, 0:tm // COMBINE_TILE, 0:N_EXPERTS].astype(jnp.int32).reshape(-1), cnt])
    xs = _dispatch(pos, cnt, hn.reshape(T, D), cap, tg, 512)
    blk, exp, fa, fb, nact = _moe_tiles(cnt, 2 * T // tg + N_EXPERTS, tg, cap)
    y = _moe_group(blk, exp, fa, fb, nact, xs, *moe_w, tg)
    out = _combine(tile_base, x.reshape(T, D), lat_o, meta.reshape(T, LANES), row(final_norm), y, S, cap)
    return out.reshape(B, S, D)
```

```python
import functools

import jax
import jax.numpy as jnp
from jax import lax
from jax.experimental import pallas as pl
from jax.experimental.pallas import tpu as pltpu

F32 = jnp.float32
BF16 = jnp.bfloat16

D_MODEL = 1024
GRID_W = 64
EPS = 1e-6
NEG_INF = -1e30
CHUNK = 128
SGU_GROUPS = 4
SGU_WIDTH = 512
HEAD_DIM = 64
N_Q_HEADS = 8
N_KV_HEADS = 2
ATTN_WIDTH = 512
KV_WIDTH = 128
WINDOW = 128
ATTN_BLOCK = 128
ATTN_SCALE = HEAD_DIM ** -0.5
LOG2E = 1.4426950408889634
ROW_BLOCK = 32
COMBINE_TILE = 256
COMBINE_CHUNK = 16
ROPE_BASE = 10000.0
ROPE_FREQS = 16
IN_EVEN = 1792
D_RNN = 1280
LRU_BLOCKS = 10
LRU_BLOCK = 128
LRU_C = 8.0
D_FF = 2816
N_EXPERTS = 8
LANES = 128
SUBLANES = 8
VMEM_LIMIT = 56 * 1024 * 1024


def _cparams(sem):
    return pltpu.CompilerParams(dimension_semantics=sem, vmem_limit_bytes=VMEM_LIMIT)


def _dot(a, b):
    return jnp.dot(a, b, preferred_element_type=F32)


def _dot_nt(a, b):
    return lax.dot_general(a, b, (((1,), (1,)), ((), ())), preferred_element_type=F32)


def _gelu(x):
    return 0.5 * x * (1.0 + jnp.tanh(0.7978845608028654 * (x + 0.044715 * (x * x * x))))


def _sigmoid(x):
    return 0.5 * jnp.tanh(0.5 * x) + 0.5


def _rms(x, nw):
    return (x * lax.rsqrt(jnp.mean(x * x, axis=-1, keepdims=True) + EPS)) * nw


def _rms_mod(x, nw, scale, shift):
    return _rms(x, nw) * (1.0 + scale) + shift


def _ada_kernel(c_ref, w_ref, b_ref, o_ref):
    c = c_ref[...]
    act = c * _sigmoid(c)
    o_ref[...] = jnp.dot(act, w_ref[...], precision=lax.Precision.HIGHEST,
                         preferred_element_type=F32) + b_ref[...]


def _ada_params(cvec, w, b):
    n = w.shape[1]
    tn = 1536
    out = pl.pallas_call(
        _ada_kernel,
        grid=(n // tn,),
        in_specs=[pl.BlockSpec((SUBLANES, D_MODEL), lambda j: (0, 0)),
                  pl.BlockSpec((D_MODEL, tn), lambda j: (0, j)),
                  pl.BlockSpec((1, tn), lambda j: (0, j))],
        out_specs=pl.BlockSpec((SUBLANES, tn), lambda j: (0, j)),
        out_shape=jax.ShapeDtypeStruct((SUBLANES, n), F32),
        compiler_params=_cparams(("parallel",)),
        name="ada_params",
    )(cvec, w, b.reshape(1, n))
    return out.reshape(SUBLANES, 6, D_MODEL)


def _proj_even_kernel(x_ref, mod_ref, nw_ref, w_ref, cos_ref, sin_ref,
                      u_ref, v_ref, q_ref, k_ref, val_ref):
    m = mod_ref[0]
    h = _rms_mod(x_ref[0], nw_ref[...], m[1:2], m[0:1]).astype(BF16)
    u_ref[0] = _gelu(_dot(h, w_ref[:, 0:SGU_WIDTH])).astype(u_ref.dtype)
    v_ref[0] = _gelu(_dot(h, w_ref[:, SGU_WIDTH:2 * SGU_WIDTH])).astype(v_ref.dtype)
    cos = cos_ref[...]
    sin = sin_ref[...]
    lane = lax.broadcasted_iota(jnp.int32, cos.shape, 1)
    first_half = (lane % 32) < ROPE_FREQS

    def rope(t):
        partner = jnp.where(first_half, pltpu.roll(t, LANES - ROPE_FREQS, 1), pltpu.roll(t, ROPE_FREQS, 1))
        return t * cos + partner * sin

    q = _dot(h, w_ref[:, 2 * SGU_WIDTH:2 * SGU_WIDTH + ATTN_WIDTH]) * (ATTN_SCALE * LOG2E)
    for g in range(ATTN_WIDTH // LANES):
        q_ref[0, :, g * LANES:(g + 1) * LANES] = rope(q[:, g * LANES:(g + 1) * LANES]).astype(BF16)
    kv = _dot(h, w_ref[:, 2 * SGU_WIDTH + ATTN_WIDTH:IN_EVEN])
    k = rope(kv[:, 0:KV_WIDTH])
    val = kv[:, KV_WIDTH:2 * KV_WIDTH]
    k_ref[0, :, 0:LANES] = k.astype(BF16)
    k_ref[0, :, LANES:2 * LANES] = pltpu.roll(k, HEAD_DIM, 1).astype(BF16)
    val_ref[0, :, 0:LANES] = val.astype(BF16)
    val_ref[0, :, LANES:2 * LANES] = pltpu.roll(val, HEAD_DIM, 1).astype(BF16)


def _proj_even(x, mod, mod_per_batch, nw, w, cos, sin, tm):
    B, S, D = x.shape
    mod_map = (lambda b, i: (b, 0, 0)) if mod_per_batch else (lambda b, i: (0, 0, 0))
    tok = lambda width: pl.BlockSpec((1, tm, width), lambda b, i: (b, i, 0))
    return pl.pallas_call(
        _proj_even_kernel,
        grid=(B, S // tm),
        in_specs=[tok(D),
                  pl.BlockSpec((1, 6, D), mod_map),
                  pl.BlockSpec((1, D), lambda b, i: (0, 0)),
                  pl.BlockSpec((D, IN_EVEN), lambda b, i: (0, 0)),
                  pl.BlockSpec((tm, LANES), lambda b, i: (i, 0)),
                  pl.BlockSpec((tm, LANES), lambda b, i: (i, 0))],
        out_specs=[tok(SGU_WIDTH), tok(SGU_WIDTH), tok(ATTN_WIDTH), tok(2 * KV_WIDTH), tok(2 * KV_WIDTH)],
        out_shape=[jax.ShapeDtypeStruct((B, S, SGU_WIDTH), BF16),
                   jax.ShapeDtypeStruct((B, S, SGU_WIDTH), BF16),
                   jax.ShapeDtypeStruct((B, S, ATTN_WIDTH), BF16),
                   jax.ShapeDtypeStruct((B, S, 2 * KV_WIDTH), BF16),
                   jax.ShapeDtypeStruct((B, S, 2 * KV_WIDTH), BF16)],
        compiler_params=_cparams(("parallel", "parallel")),
        name="proj_even",
    )(x, mod, nw, w, cos, sin)


def _mixer_even_kernel(sink_ref, x_ref, mod_ref, u_ref, v_ref, q_ref, k_ref, val_ref, kc_ref, vc_ref,
                       ws_ref, bs_ref, wout_ref, bias_ref, o_ref, mix_ref, s_ref, p_ref, inv_ref,
                       *, seq_len, is_ctx):
    tq = x_ref.shape[1]
    n_chunks = tq // CHUNK
    i = pl.program_id(1)
    nk = kc_ref.shape[1] + (0 if is_ctx else 3 * ATTN_BLOCK)
    lane = lax.broadcasted_iota(jnp.int32, (1, LANES), 1)
    lo = lane < HEAD_DIM
    zero = jnp.zeros((), BF16)

    def halves(ref_slice, kh):
        nat, swp = ref_slice[:, 0:LANES], ref_slice[:, LANES:2 * LANES]
        if kh == 0:
            return jnp.where(lo, nat, zero), jnp.where(lo, zero, swp)
        return jnp.where(lo, swp, zero), jnp.where(lo, zero, nat)

    kc_all = kc_ref[0]
    vc_all = vc_ref[0]

    def chunk_body(c, carry):
        r0 = pl.multiple_of(c * CHUNK, CHUNK)
        rows = pl.ds(r0, CHUNK)
        vch = v_ref[0, rows, :].astype(F32)
        uch = u_ref[0, rows, :].astype(F32)
        for g in range(SGU_GROUPS):
            cols = slice(g * LANES, (g + 1) * LANES)
            vg = vch[:, cols]
            dev = vg - jnp.mean(vg, axis=-1, keepdims=True)
            vn = dev * lax.rsqrt(jnp.mean(dev * dev, axis=-1, keepdims=True) + EPS)
            mixed = _dot(ws_ref[g], vn.astype(BF16)) + bs_ref[:, g:g + 1]
            mix_ref[rows, cols] = (uch[:, cols] * mixed).astype(BF16)
        qch = q_ref[0, rows, :]
        if not is_ctx:
            blk = i * n_chunks + c
            n_blk = seq_len // ATTN_BLOCK
            start = pl.multiple_of(jnp.clip((blk - 1) * ATTN_BLOCK, 0, seq_len - 3 * ATTN_BLOCK), ATTN_BLOCK)
            k3 = k_ref[0, pl.ds(start, 3 * ATTN_BLOCK), :]
            v3 = val_ref[0, pl.ds(start, 3 * ATTN_BLOCK), :]
            case = jnp.where(blk == 0, 0, jnp.where(blk == n_blk - 1, 2, 1))
        for kh in range(N_KV_HEADS):
            kc_lo, kc_hi = halves(kc_all, kh)
            vc_lo, vc_hi = halves(vc_all, kh)
            if is_ctx:
                k_cat = jnp.concatenate([kc_lo, kc_hi], axis=0)
                v_cat = (vc_lo, vc_hi)
            else:
                k_lo, k_hi = halves(k3, kh)
                v_lo, v_hi = halves(v3, kh)
                k_cat = jnp.concatenate([k_lo, kc_lo, k_hi, kc_hi], axis=0)
                v_cat = (jnp.concatenate([v_lo, vc_lo], axis=0), jnp.concatenate([v_hi, vc_hi], axis=0))
            q2 = jnp.concatenate([qch[:, 2 * kh * LANES:(2 * kh + 1) * LANES],
                                  qch[:, (2 * kh + 1) * LANES:(2 * kh + 2) * LANES]], axis=0)
            s_ref[:, 0:2 * nk] = _dot_nt(q2, k_cat)
            for half in range(2):
                for rb in range(2 * ATTN_BLOCK // ROW_BLOCK):
                    rsl = slice(rb * ROW_BLOCK, (rb + 1) * ROW_BLOCK)
                    snk = sink_ref[2 * (2 * kh + rb * ROW_BLOCK // ATTN_BLOCK) + half]
                    s = s_ref[rsl, half * nk:(half + 1) * nk]
                    if not is_ctx:
                        qoff = (rb * ROW_BLOCK) % ATTN_BLOCK
                        s_loc = s[:, 0:3 * ATTN_BLOCK] + bias_ref[case, qoff:qoff + ROW_BLOCK, :]
                        s = jnp.concatenate([s_loc, s[:, 3 * ATTN_BLOCK:]], axis=1)
                    m = jnp.maximum(jnp.max(s, axis=-1, keepdims=True), snk)
                    p = jnp.exp2(s - m)
                    den = jnp.sum(p, axis=-1, keepdims=True) + jnp.exp2(snk - m)
                    p_ref[rsl, half * nk:(half + 1) * nk] = p.astype(BF16)
                    inv_ref[rsl, half * LANES:(half + 1) * LANES] = jnp.broadcast_to(1.0 / den, (ROW_BLOCK, LANES))
            o_lo = _dot(p_ref[:, 0:nk], v_cat[0])
            o_hi = _dot(p_ref[:, nk:2 * nk], v_cat[1])
            acc = (o_lo * inv_ref[:, 0:LANES] + o_hi * inv_ref[:, LANES:2 * LANES]).astype(BF16)
            for g in range(2):
                col = SGU_WIDTH + (2 * kh + g) * LANES
                mix_ref[rows, col:col + LANES] = acc[g * ATTN_BLOCK:(g + 1) * ATTN_BLOCK]
        return carry

    lax.fori_loop(0, n_chunks, chunk_body, 0)
    y = _dot(mix_ref[...], wout_ref[...])
    o_ref[0] = x_ref[0] + mod_ref[0][2:3] * y


def _window_bias():
    case = jnp.arange(3, dtype=jnp.int32)[:, None, None]
    qi = jnp.arange(ATTN_BLOCK, dtype=jnp.int32)[None, :, None]
    kj = jnp.arange(3 * ATTN_BLOCK, dtype=jnp.int32)[None, None, :]
    return jnp.where(jnp.abs(kj - case * ATTN_BLOCK - qi) <= WINDOW, 0.0, NEG_INF).astype(F32)


def _mixer_even(x, mod, mod_per_batch, u, v, q, k2, v2, kc2, vc2, ws, bs_t, sink, wout, tq, is_ctx):
    B, S, D = x.shape
    Sk = k2.shape[1]
    Lc = kc2.shape[1]
    nk = Lc + (0 if is_ctx else 3 * ATTN_BLOCK)
    mod_map = (lambda b, i: (b, 0, 0)) if mod_per_batch else (lambda b, i: (0, 0, 0))
    tok = lambda width: pl.BlockSpec((1, tq, width), lambda b, i: (b, i, 0))
    per_batch = lambda rows: pl.BlockSpec((1, rows, 2 * KV_WIDTH), lambda b, i: (b, 0, 0))
    return pl.pallas_call(
        functools.partial(_mixer_even_kernel, seq_len=S, is_ctx=is_ctx),
        grid=(B, S // tq),
        in_specs=[pl.BlockSpec(memory_space=pltpu.SMEM),
                  tok(D),
                  pl.BlockSpec((1, 6, D), mod_map),
                  tok(SGU_WIDTH), tok(SGU_WIDTH), tok(ATTN_WIDTH),
                  per_batch(Sk), per_batch(Sk), per_batch(Lc), per_batch(Lc),
                  pl.BlockSpec((SGU_GROUPS, CHUNK, CHUNK), lambda b, i: (0, 0, 0)),
                  pl.BlockSpec((CHUNK, SGU_GROUPS), lambda b, i: (0, 0)),
                  pl.BlockSpec((D, D), lambda b, i: (0, 0)),
                  pl.BlockSpec((3, ATTN_BLOCK, 3 * ATTN_BLOCK), lambda b, i: (0, 0, 0))],
        out_specs=tok(D),
        out_shape=jax.ShapeDtypeStruct((B, S, D), F32),
        scratch_shapes=[pltpu.VMEM((tq, D), BF16),
                        pltpu.VMEM((2 * ATTN_BLOCK, 2 * nk), F32),
                        pltpu.VMEM((2 * ATTN_BLOCK, 2 * nk), BF16),
                        pltpu.VMEM((2 * ATTN_BLOCK, 2 * LANES), F32)],
        compiler_params=_cparams(("parallel", "arbitrary")),
        name="mixer_ctx" if is_ctx else "mixer_even",
    )(sink, x, mod, u, v, q, k2, v2, kc2, vc2, ws, bs_t, wout, _window_bias())


def _route_kernel(x_ref, mod_ref, nw_ref, rw_ref, hn_ref, meta_ref, pos_ref, cnt_ref, tb_ref, base_ref, *, cap):
    @pl.when(jnp.logical_and(pl.program_id(0) == 0, pl.program_id(1) == 0))
    def _():
        base_ref[...] = jnp.zeros_like(base_ref)

    m = mod_ref[0]
    h = _rms_mod(x_ref[0], nw_ref[...], m[4:5], m[3:4])
    hn_ref[0] = h
    w = rw_ref[...]
    w_hi = w.astype(BF16)
    w_lo = (w - w_hi.astype(F32)).astype(BF16)
    h_hi = h.astype(BF16)
    h_lo = (h - h_hi.astype(F32)).astype(BF16)
    logits = _dot(h_hi, w_hi) + (_dot(h_lo, w_hi) + _dot(h_hi, w_lo))
    tm = logits.shape[0]
    lane = lax.broadcasted_iota(jnp.int32, logits.shape, 1)
    lg = jnp.where(lane < N_EXPERTS, logits, -jnp.inf)
    m1 = jnp.max(lg, axis=-1, keepdims=True)
    i1 = jnp.min(jnp.where(lg == m1, lane, LANES), axis=-1, keepdims=True)
    lg2 = jnp.where(lane == i1, -jnp.inf, lg)
    m2 = jnp.max(lg2, axis=-1, keepdims=True)
    i2 = jnp.min(jnp.where(lg2 == m2, lane, LANES), axis=-1, keepdims=True)
    e2 = jnp.exp(m2 - m1)
    den = 1.0 + e2
    hot = jnp.where(jnp.logical_or(lane == i1, lane == i2), 1.0, 0.0)
    r = lax.broadcasted_iota(jnp.int32, (tm, tm), 0)
    c = lax.broadcasted_iota(jnp.int32, (tm, tm), 1)
    before = jnp.where(r > c, 1.0, 0.0).astype(BF16)
    tot = base_ref[...] + _dot(before, hot.astype(BF16))
    rank1 = jnp.sum(jnp.where(lane == i1, tot, 0.0), axis=-1, keepdims=True)
    rank2 = jnp.sum(jnp.where(lane == i2, tot, 0.0), axis=-1, keepdims=True)
    pos1 = i1.astype(F32) * cap + rank1
    pos2 = i2.astype(F32) * cap + rank2
    meta = (jnp.where(lane == 0, pos1, 0.0) + jnp.where(lane == 1, pos2, 0.0)
            + jnp.where(lane == 2, 1.0 / den, 0.0) + jnp.where(lane == 3, e2 / den, 0.0)
            + jnp.where(lane == 4, i1.astype(F32), 0.0) + jnp.where(lane == 5, i2.astype(F32), 0.0))
    meta_ref[0] = meta
    pos_ref[0, 0] = meta.T[0:SUBLANES].astype(jnp.int32)
    sub = lax.broadcasted_iota(jnp.int32, (SUBLANES, LANES), 0)
    tb = jnp.zeros((SUBLANES, LANES), F32)
    for j in range(tm // COMBINE_TILE):
        tb = jnp.where(sub == j, tot[j * COMBINE_TILE:j * COMBINE_TILE + 1, :], tb)
    tb_ref[0, 0] = tb
    base_ref[...] += jnp.sum(hot, axis=0, keepdims=True)
    cnt_ref[...] = jnp.broadcast_to(base_ref[...], cnt_ref.shape)


def _route(x, mod, nw, rw, tm, cap):
    B, S, D = x.shape
    return pl.pallas_call(
        functools.partial(_route_kernel, cap=float(cap)),
        grid=(B, S // tm),
        in_specs=[pl.BlockSpec((1, tm, D), lambda b, i: (b, i, 0)),
                  pl.BlockSpec((1, 6, D), lambda b, i: (b, 0, 0)),
                  pl.BlockSpec((1, D), lambda b, i: (0, 0)),
                  pl.BlockSpec((D, LANES), lambda b, i: (0, 0))],
        out_specs=[pl.BlockSpec((1, tm, D), lambda b, i: (b, i, 0)),
                   pl.BlockSpec((1, tm, LANES), lambda b, i: (b, i, 0)),
                   pl.BlockSpec((1, 1, SUBLANES, tm), lambda b, i: (b, i, 0, 0)),
                   pl.BlockSpec((SUBLANES, LANES), lambda b, i: (0, 0)),
                   pl.BlockSpec((1, 1, SUBLANES, LANES), lambda b, i: (b, i, 0, 0))],
        out_shape=[jax.ShapeDtypeStruct((B, S, D), F32),
                   jax.ShapeDtypeStruct((B, S, LANES), F32),
                   jax.ShapeDtypeStruct((B, S // tm, SUBLANES, tm), jnp.int32),
                   jax.ShapeDtypeStruct((SUBLANES, LANES), F32),
                   jax.ShapeDtypeStruct((B, S // tm, SUBLANES, LANES), F32)],
        scratch_shapes=[pltpu.VMEM((1, LANES), F32)],
        compiler_params=_cparams(("arbitrary", "arbitrary")),
        name="moe_route",
    )(x, mod, nw, rw)


def _dispatch_kernel(pos_ref, cnt_ref, hn_ref, xs_ref, zero_ref, sem, zsem, *, n_tok, cap, tg):
    i = pl.program_id(0)
    td = hn_ref.shape[0]

    @pl.when(i == 0)
    def _():
        zero_ref[...] = jnp.zeros_like(zero_ref)

        def tail_copy(e):
            start = pl.multiple_of(e * cap + (cnt_ref[e] // SUBLANES) * SUBLANES, SUBLANES)
            return pltpu.make_async_copy(zero_ref, xs_ref.at[pl.ds(start, tg)], zsem)

        for e in range(N_EXPERTS):
            tail_copy(e).start()
        for e in range(N_EXPERTS):
            tail_copy(e).wait()

    def tok_body(t, c):
        tok = i * td + t
        pltpu.make_async_copy(hn_ref.at[pl.ds(t, 1)], xs_ref.at[pl.ds(pos_ref[tok], 1)], sem).start()
        pltpu.make_async_copy(hn_ref.at[pl.ds(t, 1)], xs_ref.at[pl.ds(pos_ref[n_tok + tok], 1)],
                              sem).start(priority=1)
        return c

    lax.fori_loop(0, td, tok_body, 0, unroll=8)
    for _ in range(2):
        pltpu.make_async_copy(hn_ref, hn_ref, sem).wait()


def _dispatch(pos, cnt, hn, cap, tg, td):
    T, D = hn.shape
    return pl.pallas_call(
        functools.partial(_dispatch_kernel, n_tok=T, cap=cap, tg=tg),
        grid_spec=pltpu.PrefetchScalarGridSpec(
            num_scalar_prefetch=2,
            grid=(T // td,),
            in_specs=[pl.BlockSpec((td, D), lambda i, pos, cnt: (i, 0))],
            out_specs=pl.BlockSpec(memory_space=pl.ANY),
            scratch_shapes=[pltpu.VMEM((tg, D), F32), pltpu.SemaphoreType.DMA, pltpu.SemaphoreType.DMA]),
        out_shape=jax.ShapeDtypeStruct((N_EXPERTS * cap, D), F32),
        compiler_params=_cparams(("arbitrary",)),
        name="moe_dispatch",
    )(pos, cnt, hn)


def _moe_group_kernel(blk_ref, exp_ref, fa_ref, fb_ref, nact_ref, x_ref, w1_ref, w3_ref, w2_ref, o_ref, acc_ref):
    t = pl.program_id(0)
    f = pl.program_id(1)

    @pl.when(jnp.logical_and(t == 0, f == 0))
    def _():
        acc_ref[...] = jnp.zeros_like(acc_ref)

    @pl.when(t < nact_ref[0])
    def _():
        h = x_ref[...].astype(BF16)
        a = _dot(h, w1_ref[0])
        b = _dot(h, w3_ref[0])
        y = _dot(((a * _sigmoid(a)) * b).astype(BF16), w2_ref[0])
        total = jnp.where(f == 0, 0.0, acc_ref[...]) + y
        acc_ref[...] = total
        o_ref[...] = total.astype(o_ref.dtype)


def _moe_group(blk, exp, fa, fb, nact, xs, w1, w3, w2, tg):
    R, D = xs.shape
    F = w1.shape[2]
    tf = F // 2
    fsel = lambda f, fa, fb, t: jnp.where(f == 0, fa[t], fb[t])
    return pl.pallas_call(
        _moe_group_kernel,
        grid_spec=pltpu.PrefetchScalarGridSpec(
            num_scalar_prefetch=5,
            grid=(blk.shape[0], 2),
            in_specs=[pl.BlockSpec((tg, D), lambda t, f, blk, exp, fa, fb, na: (blk[t], 0)),
                      pl.BlockSpec((1, D, tf), lambda t, f, blk, exp, fa, fb, na: (exp[t], 0, fsel(f, fa, fb, t))),
                      pl.BlockSpec((1, D, tf), lambda t, f, blk, exp, fa, fb, na: (exp[t], 0, fsel(f, fa, fb, t))),
                      pl.BlockSpec((1, tf, D), lambda t, f, blk, exp, fa, fb, na: (exp[t], fsel(f, fa, fb, t), 0))],
            out_specs=pl.BlockSpec((tg, D), lambda t, f, blk, exp, fa, fb, na: (blk[t], 0)),
            scratch_shapes=[pltpu.VMEM((tg, D), F32)]),
        out_shape=jax.ShapeDtypeStruct((R, D), BF16),
        compiler_params=_cparams(("arbitrary", "arbitrary")),
        name="moe_experts",
    )(blk, exp, fa, fb, nact, xs, w1, w3, w2)


def _combine_kernel(tb_ref, x_ref, mod_ref, meta_ref, fn_ref, y_ref, o_ref, buf_ref, sem, *, cap):
    i = pl.program_id(0)
    n = pl.num_programs(0)
    tc = x_ref.shape[0]
    n_rows = buf_ref.shape[1]
    shift = COMBINE_CHUNK.bit_length() - 1

    def segments(tile):
        segs, off = [], 0
        for e in range(N_EXPERTS):
            first = tb_ref[tile * N_EXPERTS + e]
            count = tb_ref[(tile + 1) * N_EXPERTS + e] - first
            lead = first & (COMBINE_CHUNK - 1)
            n_chunks = jnp.where(count > 0, lax.shift_right_logical(lead + count + COMBINE_CHUNK - 1, shift), 0)
            segs.append((e * cap + first - lead, n_chunks, off))
            off = off + n_chunks * COMBINE_CHUNK
        return segs

    def chunk_copy(src_row, dst_row, slot):
        return pltpu.make_async_copy(y_ref.at[pl.ds(pl.multiple_of(src_row, COMBINE_CHUNK), COMBINE_CHUNK)],
                                     buf_ref.at[slot, pl.ds(pl.multiple_of(dst_row, COMBINE_CHUNK), COMBINE_CHUNK)],
                                     sem.at[slot])

    def for_each_chunk(tile, slot, action):
        for src, n_chunks, off in segments(tile):
            def body(c, carry):
                action(chunk_copy(src + c * COMBINE_CHUNK, off + c * COMBINE_CHUNK, slot))
                return carry

            lax.fori_loop(0, n_chunks, body, 0)

    @pl.when(i == 0)
    def _():
        buf_ref[...] = jnp.zeros_like(buf_ref)
        for_each_chunk(0, 0, lambda cp: cp.start())

    slot = i % 2

    @pl.when(i + 1 < n)
    def _():
        for_each_chunk(i + 1, 1 - slot, lambda cp: cp.start())

    for_each_chunk(i, slot, lambda cp: cp.wait())
    meta = meta_ref[...]
    segs = segments(i)

    def one_hot(pos, exp):
        delta = jnp.zeros_like(pos)
        for e, (src, _, off) in enumerate(segs):
            delta = jnp.where(exp == e, off - src, delta)
        col = lax.broadcasted_iota(jnp.int32, (tc, n_rows), 1)
        return jnp.where(col == pos + delta, 1.0, 0.0).astype(BF16)

    as_int = lambda lane: meta[:, lane:lane + 1].astype(jnp.int32)
    picks = jnp.concatenate([one_hot(as_int(0), as_int(4)), one_hot(as_int(1), as_int(5))], axis=0)
    y12 = _dot(picks, buf_ref[slot])
    mix = meta[:, 2:3] * y12[0:tc] + meta[:, 3:4] * y12[tc:2 * tc]
    o_ref[...] = _rms(x_ref[...] + mod_ref[0][5:6] * mix, fn_ref[...])


def _combine(tile_base, x, mod, meta, fin, y, seq_len, cap):
    T, D = x.shape
    tc = COMBINE_TILE
    per_batch = seq_len // tc
    n_rows = -(-(2 * tc + 2 * N_EXPERTS * (COMBINE_CHUNK - 1)) // LANES) * LANES
    return pl.pallas_call(
        functools.partial(_combine_kernel, cap=cap),
        grid_spec=pltpu.PrefetchScalarGridSpec(
            num_scalar_prefetch=1,
            grid=(T // tc,),
            in_specs=[pl.BlockSpec((tc, D), lambda i, tb: (i, 0)),
                      pl.BlockSpec((1, 6, D), lambda i, tb: (i // per_batch, 0, 0)),
                      pl.BlockSpec((tc, LANES), lambda i, tb: (i, 0)),
                      pl.BlockSpec((1, D), lambda i, tb: (0, 0)),
                      pl.BlockSpec(memory_space=pl.ANY)],
            out_specs=pl.BlockSpec((tc, D), lambda i, tb: (i, 0)),
            scratch_shapes=[pltpu.VMEM((2, n_rows, D), BF16), pltpu.SemaphoreType.DMA((2,))]),
        out_shape=jax.ShapeDtypeStruct((T, D), F32),
        compiler_params=_cparams(("arbitrary",)),
        name="moe_combine",
    )(tile_base, x, mod, meta, fin, y)


def _moe_tiles(cnt, n_tiles, tg, cap):
    per = (cnt + tg - 1) // tg
    cum = jnp.cumsum(per)
    nact = cum[-1]
    t = jnp.arange(n_tiles, dtype=jnp.int32)
    tt = jnp.minimum(t, nact - 1)
    exp = jnp.minimum(jnp.sum((tt[:, None] >= cum[None, :]).astype(jnp.int32), axis=1), N_EXPERTS - 1)
    blk = exp * (cap // tg) + tt - (cum - per)[exp]
    odd = tt % 2
    fa = jnp.where(t < nact, odd, 1 - odd)
    fb = 1 - odd
    i32 = lambda v: v.astype(jnp.int32)
    return i32(blk), i32(exp), i32(fa), i32(fb), i32(nact.reshape(1))


def _ffn_kernel(*refs, n_cast):
    x_ref, mod_ref, nw_ref, w1_ref, w3_ref, w2_ref = refs[0:6]
    cast_in = refs[6:6 + n_cast]
    o_ref = refs[6 + n_cast]
    cast_out = refs[7 + n_cast:7 + 2 * n_cast]
    (acc_ref,) = refs[7 + 2 * n_cast:]
    f = pl.program_id(2)

    @pl.when(jnp.logical_and(jnp.logical_and(pl.program_id(0) == 0, pl.program_id(1) == 0), f == 0))
    def _():
        acc_ref[...] = jnp.zeros_like(acc_ref)

    for src, dst in zip(cast_in, cast_out):
        dst[...] = src[...].astype(dst.dtype)
    m = mod_ref[0]
    x = x_ref[0]
    h = _rms_mod(x, nw_ref[...], m[4:5], m[3:4]).astype(BF16)
    a = _dot(h, w1_ref[...])
    b = _dot(h, w3_ref[...])
    total = jnp.where(f == 0, 0.0, acc_ref[...]) + _dot(((a * _sigmoid(a)) * b).astype(BF16), w2_ref[...])
    acc_ref[...] = total
    o_ref[0] = x + m[5:6] * total


def _ffn(x, mod, mod_per_batch, nw, w1, w3, w2, tm, tf, cast=()):
    B, S, D = x.shape
    F = w1.shape[1]
    n_i, n_f = S // tm, F // tf
    n_steps = B * n_i * n_f
    mod_map = (lambda b, i, f: (b, 0, 0)) if mod_per_batch else (lambda b, i, f: (0, 0, 0))
    cast_specs, cast_shapes = [], []
    for arr in cast:
        E, R, C = arr.shape
        per = n_steps // E
        assert per * E == n_steps and R % (per * 2 * SUBLANES) == 0, (arr.shape, n_steps)

        def cast_map(b, i, f, per=per):
            step = (b * n_i + i) * n_f + f
            return step // per, step % per, 0

        cast_specs.append(pl.BlockSpec((1, R // per, C), cast_map))
        cast_shapes.append(jax.ShapeDtypeStruct(arr.shape, BF16))
    tok = pl.BlockSpec((1, tm, D), lambda b, i, f: (b, i, 0))
    outs = pl.pallas_call(
        functools.partial(_ffn_kernel, n_cast=len(cast)),
        grid=(B, n_i, n_f),
        in_specs=[tok,
                  pl.BlockSpec((1, 6, D), mod_map),
                  pl.BlockSpec((1, D), lambda b, i, f: (0, 0)),
                  pl.BlockSpec((D, tf), lambda b, i, f: (0, f)),
                  pl.BlockSpec((D, tf), lambda b, i, f: (0, f)),
                  pl.BlockSpec((tf, D), lambda b, i, f: (f, 0))] + cast_specs,
        out_specs=[tok] + cast_specs,
        out_shape=[jax.ShapeDtypeStruct((B, S, D), F32)] + cast_shapes,
        scratch_shapes=[pltpu.VMEM((tm, D), F32)],
        compiler_params=_cparams(("parallel", "parallel", "arbitrary")),
        name="ffn",
    )(x, mod, nw, w1, w3, w2, *cast)
    return outs[0], tuple(outs[1:])


def _proj_odd_kernel(x_ref, mod_ref, nw_ref, w_ref, gate_ref, rec_ref):
    m = mod_ref[0]
    h = _rms_mod(x_ref[0], nw_ref[...], m[1:2], m[0:1]).astype(BF16)
    gate_ref[0] = _gelu(_dot(h, w_ref[:, 0:D_RNN])).astype(gate_ref.dtype)
    rec_ref[0] = _dot(h, w_ref[:, D_RNN:2 * D_RNN])


def _proj_odd(x, mod, mod_per_batch, nw, w, tm):
    B, S, D = x.shape
    mod_map = (lambda b, i: (b, 0, 0)) if mod_per_batch else (lambda b, i: (0, 0, 0))
    tok = lambda width: pl.BlockSpec((1, tm, width), lambda b, i: (b, i, 0))
    return pl.pallas_call(
        _proj_odd_kernel,
        grid=(B, S // tm),
        in_specs=[tok(D),
                  pl.BlockSpec((1, 6, D), mod_map),
                  pl.BlockSpec((1, D), lambda b, i: (0, 0)),
                  pl.BlockSpec((D, 2 * D_RNN), lambda b, i: (0, 0))],
        out_specs=[tok(D_RNN), tok(D_RNN)],
        out_shape=[jax.ShapeDtypeStruct((B, S, D_RNN), BF16), jax.ShapeDtypeStruct((B, S, D_RNN), F32)],
        compiler_params=_cparams(("parallel", "parallel")),
        name="proj_odd",
    )(x, mod, nw, w)


def _scan8(a, b, h, row, reverse):
    for s in (1, 2, 4):
        if reverse:
            a_s, b_s, live = pltpu.roll(a, SUBLANES - s, 0), pltpu.roll(b, SUBLANES - s, 0), row < SUBLANES - s
        else:
            a_s, b_s, live = pltpu.roll(a, s, 0), pltpu.roll(b, s, 0), row >= s
        b = jnp.where(live, a * b_s + b, b)
        a = jnp.where(live, a * a_s, a)
    hr = a * h + b
    return hr, (hr[0:1] if reverse else hr[SUBLANES - 1:SUBLANES])


def _lru_kernel(rec_ref, recc_ref, cw_ref, cb_ref, wa_ref, ba_ref, wx_ref, bx_ref, lam_ref,
                s_ref, pad_ref, a_ref, b_ref, cpad_ref, ca_ref, cbb_ref, *, tile):
    S = rec_ref.shape[1]
    L = recc_ref.shape[1]
    cw = cw_ref[...]
    cb = cb_ref[...]
    lam = lam_ref[...]
    sp = jnp.maximum(-lam, 0.0) + jnp.log1p(jnp.exp(-jnp.abs(lam)))
    zeros8 = jnp.zeros((SUBLANES, LANES), F32)

    def coefficients(src_ref, dst_a, dst_b, n_rows, t):
        pad = cpad_ref if src_ref is recc_ref else pad_ref
        pad[0:SUBLANES, :] = zeros8
        pad[SUBLANES + n_rows:2 * SUBLANES + n_rows, :] = zeros8

        def copy(j, carry):
            r0 = pl.multiple_of(j * t, t)
            pad[pl.ds(SUBLANES + r0, t), :] = src_ref[0, pl.ds(r0, t), :]
            return carry

        lax.fori_loop(0, n_rows // t, copy, 0)

        def body(j, carry):
            r0 = pl.multiple_of(j * t, t)
            ext = pad[pl.ds(r0, t + 2 * SUBLANES), :]
            conv = cb
            for tap in range(4):
                conv = conv + cw[tap:tap + 1] * ext[SUBLANES - 2 + tap:SUBLANES - 2 + tap + t]
            cbf = conv.astype(BF16)
            for d in range(2):
                r = _sigmoid(_dot(cbf, wa_ref[d, 0]) + ba_ref[d:d + 1])
                gi = _sigmoid(_dot(cbf, wx_ref[d, 0]) + bx_ref[d:d + 1])
                log_a = -LRU_C * r * sp[d:d + 1]
                a = jnp.exp(log_a)
                y = -jnp.tanh(0.5 * log_a)
                root = jnp.where(y > 0.0, y * lax.rsqrt(y), 0.0)
                dst_a[d, pl.ds(r0, t), :] = a
                dst_b[d, pl.ds(r0, t), :] = (root * (1.0 + a)) * (gi * conv)
            return carry

        lax.fori_loop(0, n_rows // t, body, 0)

    row = lax.broadcasted_iota(jnp.int32, (SUBLANES, LANES), 0)
    h_zero = jnp.zeros((1, LANES), F32)

    coefficients(recc_ref, ca_ref, cbb_ref, L, L)
    nc = L // SUBLANES

    def ctx_body(j, carry):
        hf, hb = carry
        rf = pl.multiple_of(j * SUBLANES, SUBLANES)
        rb = pl.multiple_of((nc - 1 - j) * SUBLANES, SUBLANES)
        _, hf = _scan8(ca_ref[0, pl.ds(rf, SUBLANES), :], cbb_ref[0, pl.ds(rf, SUBLANES), :], hf, row, False)
        _, hb = _scan8(ca_ref[1, pl.ds(rb, SUBLANES), :], cbb_ref[1, pl.ds(rb, SUBLANES), :], hb, row, True)
        return hf, hb

    h0f, h0b = lax.fori_loop(0, nc, ctx_body, (h_zero, h_zero))

    coefficients(rec_ref, a_ref, b_ref, S, tile)
    n = S // SUBLANES

    def lat_body(accumulate):
        def body(j, carry):
            hf, hb = carry
            rf = pl.multiple_of(j * SUBLANES, SUBLANES)
            rb = pl.multiple_of((n - 1 - j) * SUBLANES, SUBLANES)
            of, hf = _scan8(a_ref[0, pl.ds(rf, SUBLANES), :], b_ref[0, pl.ds(rf, SUBLANES), :], hf, row, False)
            ob, hb = _scan8(a_ref[1, pl.ds(rb, SUBLANES), :], b_ref[1, pl.ds(rb, SUBLANES), :], hb, row, True)
            if accumulate:
                s_ref[0, pl.ds(rf, SUBLANES), :] += of
                s_ref[0, pl.ds(rb, SUBLANES), :] += ob
            else:
                s_ref[0, pl.ds(rf, SUBLANES), :] = of
                s_ref[0, pl.ds(rb, SUBLANES), :] = ob
            return hf, hb
        return body

    mid = lax.fori_loop(0, n // 2, lat_body(False), (h0f, h0b), unroll=8)
    lax.fori_loop(n // 2, n, lat_body(True), mid, unroll=8)


def _lru(rec, rec_c, conv_w, conv_b, wa, ba, wx, bx, lam, tile):
    B, S, _ = rec.shape
    L = rec_c.shape[1]
    blk = lambda rows: pl.BlockSpec((1, rows, LRU_BLOCK), lambda b, j: (b, 0, j))
    vec = lambda rows: pl.BlockSpec((rows, LRU_BLOCK), lambda b, j: (0, j))
    wspec = pl.BlockSpec((2, 1, LRU_BLOCK, LRU_BLOCK), lambda b, j: (0, j, 0, 0))
    return pl.pallas_call(
        functools.partial(_lru_kernel, tile=tile),
        grid=(B, LRU_BLOCKS),
        in_specs=[blk(S), blk(L), vec(4), vec(1), wspec, vec(2), wspec, vec(2), vec(2)],
        out_specs=blk(S),
        out_shape=jax.ShapeDtypeStruct((B, S, D_RNN), F32),
        scratch_shapes=[pltpu.VMEM((S + 2 * SUBLANES, LRU_BLOCK), F32),
                        pltpu.VMEM((2, S, LRU_BLOCK), F32),
                        pltpu.VMEM((2, S, LRU_BLOCK), F32),
                        pltpu.VMEM((L + 2 * SUBLANES, LRU_BLOCK), F32),
                        pltpu.VMEM((2, L, LRU_BLOCK), F32),
                        pltpu.VMEM((2, L, LRU_BLOCK), F32)],
        compiler_params=_cparams(("parallel", "parallel")),
        name="lru_scan",
    )(rec, rec_c, conv_w, conv_b, wa, ba, wx, bx, lam)


def _lru_out_kernel(x_ref, mod_ref, gate_ref, s_ref, w_ref, o_ref):
    y = (gate_ref[0].astype(F32) * s_ref[0]).astype(BF16)
    o_ref[0] = x_ref[0] + mod_ref[0][2:3] * _dot(y, w_ref[...])


def _lru_out(x, mod, gate, s, w, tm):
    B, S, D = x.shape
    tok = lambda width: pl.BlockSpec((1, tm, width), lambda b, i: (b, i, 0))
    return pl.pallas_call(
        _lru_out_kernel,
        grid=(B, S // tm),
        in_specs=[tok(D), pl.BlockSpec((1, 6, D), lambda b, i: (b, 0, 0)), tok(D_RNN), tok(D_RNN),
                  pl.BlockSpec((D_RNN, D), lambda b, i: (0, 0))],
        out_specs=tok(D),
        out_shape=jax.ShapeDtypeStruct((B, S, D), F32),
        compiler_params=_cparams(("parallel", "parallel")),
        name="lru_out",
    )(x, mod, gate, s, w)


def _rope_tables(n_tok):
    rows = n_tok // GRID_W
    row = jnp.repeat(jnp.arange(rows, dtype=F32), GRID_W)
    col = jnp.tile(jnp.arange(GRID_W, dtype=F32), rows)
    freqs = ROPE_BASE ** (-jnp.arange(ROPE_FREQS, dtype=F32) / ROPE_FREQS)
    ar, ac = row[:, None] * freqs, col[:, None] * freqs
    cos = jnp.concatenate([jnp.cos(ar), jnp.cos(ar), jnp.cos(ac), jnp.cos(ac)], axis=-1)
    sin = jnp.concatenate([-jnp.sin(ar), jnp.sin(ar), -jnp.sin(ac), jnp.sin(ac)], axis=-1)
    return jnp.tile(cos, (1, LANES // HEAD_DIM)), jnp.tile(sin, (1, LANES // HEAD_DIM))


def kernel(x, c, ctx, c_ctx, ada_w_e, ada_b_e, norm1_e, norm2_e, w_in_e, sgu_w, sgu_b, attn_sink, w_out_e, ffn_w1, ffn_w3, ffn_w2, ada_w_o, ada_b_o, norm1_o, norm2_o, w_in_o, conv_w, conv_b, lru_wa, lru_ba, lru_wx, lru_bx, lru_lambda, w_out_o, router_w, moe_w1, moe_w3, moe_w2, final_norm):
    B, S, D = x.shape
    L = ctx.shape[1]
    cvec = jnp.concatenate([c, c_ctx[None], jnp.zeros((SUBLANES - B - 1, D), F32)], axis=0)
    mod_e = _ada_params(cvec, ada_w_e[0], ada_b_e[0])
    mod_o = _ada_params(cvec, ada_w_o[0], ada_b_o[0])
    lat_e, ctx_e = mod_e[0:B], mod_e[B:B + 1]
    lat_o, ctx_o = mod_o[0:B], mod_o[B:B + 1]
    bf = lambda t: t.astype(BF16)
    row = lambda t: t.reshape(1, -1)

    cos, sin = _rope_tables(S)
    cos_c, sin_c = jnp.ones((L, LANES), F32), jnp.zeros((L, LANES), F32)
    w_in = bf(w_in_e[0])
    n1, n2 = row(norm1_e[0]), row(norm2_e[0])
    uc, vc, qc, kc2, vc2 = _proj_even(ctx, ctx_e, False, n1, w_in, cos_c, sin_c, L)
    u, v, q, k2, v2 = _proj_even(x, lat_e, True, n1, w_in, cos, sin, 1024)
    ws, bs_t, wout = bf(sgu_w[0]), sgu_b[0].T, bf(w_out_e[0])
    sink = attn_sink[0] * LOG2E
    x = _mixer_even(x, lat_e, True, u, v, q, k2, v2, kc2, vc2, ws, bs_t, sink, wout, 512, False)
    xc = _mixer_even(ctx, ctx_e, False, uc, vc, qc, kc2, vc2, kc2, vc2, ws, bs_t, sink, wout, L, True)
    w1, w3, w2 = bf(ffn_w1[0]), bf(ffn_w3[0]), bf(ffn_w2[0])
    x, moe_w = _ffn(x, lat_e, True, n2, w1, w3, w2, 512, 1408, cast=(moe_w1[0], moe_w3[0], moe_w2[0]))
    xc, _ = _ffn(xc, ctx_e, False, n2, w1, w3, w2, L, 1408)

    w_in = bf(w_in_o[0])
    n1, n2 = row(norm1_o[0]), row(norm2_o[0])
    _, rec_c = _proj_odd(xc, ctx_o, False, n1, w_in, L)
    gate, rec = _proj_odd(x, lat_o, True, n1, w_in, 1024)
    s = _lru(rec, rec_c, conv_w[0], row(conv_b[0]), bf(lru_wa[0]), lru_ba[0], bf(lru_wx[0]), lru_bx[0],
             lru_lambda[0], 512)
    x = _lru_out(x, lat_o, gate, s, bf(w_out_o[0]), 1024)
    rw = jnp.pad(router_w[0], ((0, 0), (0, LANES - N_EXPERTS)))
    T = B * S
    tg = 512
    cap = T + tg
    tm = 512
    hn, meta, pos4, cnt, tb4 = _route(x, lat_o, n2, rw, tm, cap)
    pos = jnp.transpose(pos4[:, :, 0:2, :], (2, 0, 1, 3)).reshape(2 * T)
    cnt = cnt[0, 0:N_EXPERTS].astype(jnp.int32)
    tile_base = jnp.concatenate([tb4[:, :, 0:tm // COMBINE_TILE, 0:N_EXPERTS].astype(jnp.int32).reshape(-1), cnt])
    xs = _dispatch(pos, cnt, hn.reshape(T, D), cap, tg, 512)
    blk, exp, fa, fb, nact = _moe_tiles(cnt, 2 * T // tg + N_EXPERTS, tg, cap)
    y = _moe_group(blk, exp, fa, fb, nact, xs, *moe_w, tg)
    out = _combine(tile_base, x.reshape(T, D), lat_o, meta.reshape(T, LANES), row(final_norm), y, S, cap)
    return out.reshape(B, S, D)
```

```python
import functools

import jax
import jax.numpy as jnp
from jax import lax
from jax.experimental import pallas as pl
from jax.experimental.pallas import tpu as pltpu

F32 = jnp.float32
BF16 = jnp.bfloat16

D_MODEL = 1024
GRID_W = 64
EPS = 1e-6
NEG_INF = -1e30
CHUNK = 128
SGU_GROUPS = 4
SGU_WIDTH = 512
HEAD_DIM = 64
N_Q_HEADS = 8
N_KV_HEADS = 2
ATTN_WIDTH = 512
KV_WIDTH = 128
WINDOW = 128
ATTN_BLOCK = 128
ATTN_SCALE = HEAD_DIM ** -0.5
LOG2E = 1.4426950408889634
ROW_BLOCK = 32
COMBINE_TILE = 256
COMBINE_CHUNK = 16
DISPATCH_CHUNK = 64
ROPE_BASE = 10000.0
ROPE_FREQS = 16
IN_EVEN = 1792
D_RNN = 1280
LRU_BLOCKS = 10
LRU_BLOCK = 128
LRU_C = 8.0
D_FF = 2816
N_EXPERTS = 8
LANES = 128
SUBLANES = 8
VMEM_LIMIT = 56 * 1024 * 1024


def _cparams(sem):
    return pltpu.CompilerParams(dimension_semantics=sem, vmem_limit_bytes=VMEM_LIMIT)


def _dot(a, b):
    return jnp.dot(a, b, preferred_element_type=F32)


def _dot_nt(a, b):
    return lax.dot_general(a, b, (((1,), (1,)), ((), ())), preferred_element_type=F32)


def _gelu(x):
    return 0.5 * x * (1.0 + jnp.tanh(0.7978845608028654 * (x + 0.044715 * (x * x * x))))


def _sigmoid(x):
    return 0.5 * jnp.tanh(0.5 * x) + 0.5


def _rms(x, nw):
    return (x * lax.rsqrt(jnp.mean(x * x, axis=-1, keepdims=True) + EPS)) * nw


def _rms_mod(x, nw, scale, shift):
    return _rms(x, nw) * (1.0 + scale) + shift


def _ada_kernel(c_ref, w_ref, b_ref, o_ref):
    c = c_ref[...]
    act = c * _sigmoid(c)
    o_ref[...] = _dot(act.astype(BF16), w_ref[...].astype(BF16)) + b_ref[...]


def _ada_params(cvec, w, b):
    n = w.shape[1]
    tn = 1536
    out = pl.pallas_call(
        _ada_kernel,
        grid=(n // tn,),
        in_specs=[pl.BlockSpec((SUBLANES, D_MODEL), lambda j: (0, 0)),
                  pl.BlockSpec((D_MODEL, tn), lambda j: (0, j)),
                  pl.BlockSpec((1, tn), lambda j: (0, j))],
        out_specs=pl.BlockSpec((SUBLANES, tn), lambda j: (0, j)),
        out_shape=jax.ShapeDtypeStruct((SUBLANES, n), F32),
        compiler_params=_cparams(("parallel",)),
        name="ada_params",
    )(cvec, w, b.reshape(1, n))
    return out.reshape(SUBLANES, 6, D_MODEL)


def _proj_even_kernel(x_ref, mod_ref, nw_ref, w_ref, cos_ref, sin_ref,
                      u_ref, v_ref, q_ref, k_ref, val_ref):
    m = mod_ref[0]
    h = _rms_mod(x_ref[0], nw_ref[...], m[1:2], m[0:1]).astype(BF16)
    u_ref[0] = _gelu(_dot(h, w_ref[:, 0:SGU_WIDTH])).astype(u_ref.dtype)
    v_ref[0] = _gelu(_dot(h, w_ref[:, SGU_WIDTH:2 * SGU_WIDTH])).astype(v_ref.dtype)
    cos = cos_ref[...]
    sin = sin_ref[...]
    lane = lax.broadcasted_iota(jnp.int32, cos.shape, 1)
    first_half = (lane % 32) < ROPE_FREQS

    def rope(t):
        partner = jnp.where(first_half, pltpu.roll(t, LANES - ROPE_FREQS, 1), pltpu.roll(t, ROPE_FREQS, 1))
        return t * cos + partner * sin

    q = _dot(h, w_ref[:, 2 * SGU_WIDTH:2 * SGU_WIDTH + ATTN_WIDTH]) * (ATTN_SCALE * LOG2E)
    for g in range(ATTN_WIDTH // LANES):
        q_ref[0, :, g * LANES:(g + 1) * LANES] = rope(q[:, g * LANES:(g + 1) * LANES]).astype(BF16)
    kv = _dot(h, w_ref[:, 2 * SGU_WIDTH + ATTN_WIDTH:IN_EVEN])
    k = rope(kv[:, 0:KV_WIDTH])
    val = kv[:, KV_WIDTH:2 * KV_WIDTH]
    k_ref[0, :, 0:LANES] = k.astype(BF16)
    k_ref[0, :, LANES:2 * LANES] = pltpu.roll(k, HEAD_DIM, 1).astype(BF16)
    val_ref[0, :, 0:LANES] = val.astype(BF16)
    val_ref[0, :, LANES:2 * LANES] = pltpu.roll(val, HEAD_DIM, 1).astype(BF16)


def _proj_even(x, mod, mod_per_batch, nw, w, cos, sin, tm):
    B, S, D = x.shape
    mod_map = (lambda b, i: (b, 0, 0)) if mod_per_batch else (lambda b, i: (0, 0, 0))
    tok = lambda width: pl.BlockSpec((1, tm, width), lambda b, i: (b, i, 0))
    return pl.pallas_call(
        _proj_even_kernel,
        grid=(B, S // tm),
        in_specs=[tok(D),
                  pl.BlockSpec((1, 6, D), mod_map),
                  pl.BlockSpec((1, D), lambda b, i: (0, 0)),
                  pl.BlockSpec((D, IN_EVEN), lambda b, i: (0, 0)),
                  pl.BlockSpec((tm, LANES), lambda b, i: (i, 0)),
                  pl.BlockSpec((tm, LANES), lambda b, i: (i, 0))],
        out_specs=[tok(SGU_WIDTH), tok(SGU_WIDTH), tok(ATTN_WIDTH), tok(2 * KV_WIDTH), tok(2 * KV_WIDTH)],
        out_shape=[jax.ShapeDtypeStruct((B, S, SGU_WIDTH), BF16),
                   jax.ShapeDtypeStruct((B, S, SGU_WIDTH), BF16),
                   jax.ShapeDtypeStruct((B, S, ATTN_WIDTH), BF16),
                   jax.ShapeDtypeStruct((B, S, 2 * KV_WIDTH), BF16),
                   jax.ShapeDtypeStruct((B, S, 2 * KV_WIDTH), BF16)],
        compiler_params=_cparams(("parallel", "parallel")),
        name="proj_even",
    )(x, mod, nw, w, cos, sin)


def _mixer_even_kernel(sink_ref, x_ref, mod_ref, u_ref, v_ref, q_ref, k_ref, val_ref, kc_ref, vc_ref,
                       ws_ref, bs_ref, wout_ref, bias_ref, o_ref, mix_ref, s_ref, p_ref, inv_ref,
                       *, seq_len, is_ctx):
    tq = x_ref.shape[1]
    n_chunks = tq // CHUNK
    i = pl.program_id(1)
    nk = kc_ref.shape[1] + (0 if is_ctx else 3 * ATTN_BLOCK)
    lane = lax.broadcasted_iota(jnp.int32, (1, LANES), 1)
    lo = lane < HEAD_DIM
    zero = jnp.zeros((), BF16)

    def halves(ref_slice, kh):
        nat, swp = ref_slice[:, 0:LANES], ref_slice[:, LANES:2 * LANES]
        if kh == 0:
            return jnp.where(lo, nat, zero), jnp.where(lo, zero, swp)
        return jnp.where(lo, swp, zero), jnp.where(lo, zero, nat)

    kc_all = kc_ref[0]
    vc_all = vc_ref[0]

    def chunk_body(c, carry):
        r0 = pl.multiple_of(c * CHUNK, CHUNK)
        rows = pl.ds(r0, CHUNK)
        vch = v_ref[0, rows, :].astype(F32)
        uch = u_ref[0, rows, :].astype(F32)
        for g in range(SGU_GROUPS):
            cols = slice(g * LANES, (g + 1) * LANES)
            vg = vch[:, cols]
            dev = vg - jnp.mean(vg, axis=-1, keepdims=True)
            vn = dev * lax.rsqrt(jnp.mean(dev * dev, axis=-1, keepdims=True) + EPS)
            mixed = _dot(ws_ref[g], vn.astype(BF16)) + bs_ref[:, g:g + 1]
            mix_ref[rows, cols] = (uch[:, cols] * mixed).astype(BF16)
        qch = q_ref[0, rows, :]
        if not is_ctx:
            blk = i * n_chunks + c
            n_blk = seq_len // ATTN_BLOCK
            start = pl.multiple_of(jnp.clip((blk - 1) * ATTN_BLOCK, 0, seq_len - 3 * ATTN_BLOCK), ATTN_BLOCK)
            k3 = k_ref[0, pl.ds(start, 3 * ATTN_BLOCK), :]
            v3 = val_ref[0, pl.ds(start, 3 * ATTN_BLOCK), :]
            case = jnp.where(blk == 0, 0, jnp.where(blk == n_blk - 1, 2, 1))
        for kh in range(N_KV_HEADS):
            kc_lo, kc_hi = halves(kc_all, kh)
            vc_lo, vc_hi = halves(vc_all, kh)
            if is_ctx:
                k_cat = jnp.concatenate([kc_lo, kc_hi], axis=0)
                v_cat = (vc_lo, vc_hi)
            else:
                k_lo, k_hi = halves(k3, kh)
                v_lo, v_hi = halves(v3, kh)
                k_cat = jnp.concatenate([k_lo, kc_lo, k_hi, kc_hi], axis=0)
                v_cat = (jnp.concatenate([v_lo, vc_lo], axis=0), jnp.concatenate([v_hi, vc_hi], axis=0))
            q2 = jnp.concatenate([qch[:, 2 * kh * LANES:(2 * kh + 1) * LANES],
                                  qch[:, (2 * kh + 1) * LANES:(2 * kh + 2) * LANES]], axis=0)
            s_ref[:, 0:2 * nk] = _dot_nt(q2, k_cat)
            for half in range(2):
                for rb in range(2 * ATTN_BLOCK // ROW_BLOCK):
                    rsl = slice(rb * ROW_BLOCK, (rb + 1) * ROW_BLOCK)
                    snk = sink_ref[2 * (2 * kh + rb * ROW_BLOCK // ATTN_BLOCK) + half]
                    s = s_ref[rsl, half * nk:(half + 1) * nk]
                    if not is_ctx:
                        qoff = (rb * ROW_BLOCK) % ATTN_BLOCK
                        s_loc = s[:, 0:3 * ATTN_BLOCK] + bias_ref[case, qoff:qoff + ROW_BLOCK, :]
                        s = jnp.concatenate([s_loc, s[:, 3 * ATTN_BLOCK:]], axis=1)
                    m = jnp.maximum(jnp.max(s, axis=-1, keepdims=True), snk)
                    p = jnp.exp2(s - m)
                    den = jnp.sum(p, axis=-1, keepdims=True) + jnp.exp2(snk - m)
                    p_ref[rsl, half * nk:(half + 1) * nk] = p.astype(BF16)
                    inv_ref[rsl, half * LANES:(half + 1) * LANES] = jnp.broadcast_to(1.0 / den, (ROW_BLOCK, LANES))
            o_lo = _dot(p_ref[:, 0:nk], v_cat[0])
            o_hi = _dot(p_ref[:, nk:2 * nk], v_cat[1])
            acc = (o_lo * inv_ref[:, 0:LANES] + o_hi * inv_ref[:, LANES:2 * LANES]).astype(BF16)
            for g in range(2):
                col = SGU_WIDTH + (2 * kh + g) * LANES
                mix_ref[rows, col:col + LANES] = acc[g * ATTN_BLOCK:(g + 1) * ATTN_BLOCK]
        return carry

    lax.fori_loop(0, n_chunks, chunk_body, 0)
    y = _dot(mix_ref[...], wout_ref[...])
    o_ref[0] = x_ref[0] + mod_ref[0][2:3] * y


def _window_bias():
    case = jnp.arange(3, dtype=jnp.int32)[:, None, None]
    qi = jnp.arange(ATTN_BLOCK, dtype=jnp.int32)[None, :, None]
    kj = jnp.arange(3 * ATTN_BLOCK, dtype=jnp.int32)[None, None, :]
    return jnp.where(jnp.abs(kj - case * ATTN_BLOCK - qi) <= WINDOW, 0.0, NEG_INF).astype(F32)


def _mixer_even(x, mod, mod_per_batch, u, v, q, k2, v2, kc2, vc2, ws, bs_t, sink, wout, tq, is_ctx):
    B, S, D = x.shape
    Sk = k2.shape[1]
    Lc = kc2.shape[1]
    nk = Lc + (0 if is_ctx else 3 * ATTN_BLOCK)
    mod_map = (lambda b, i: (b, 0, 0)) if mod_per_batch else (lambda b, i: (0, 0, 0))
    tok = lambda width: pl.BlockSpec((1, tq, width), lambda b, i: (b, i, 0))
    per_batch = lambda rows: pl.BlockSpec((1, rows, 2 * KV_WIDTH), lambda b, i: (b, 0, 0))
    return pl.pallas_call(
        functools.partial(_mixer_even_kernel, seq_len=S, is_ctx=is_ctx),
        grid=(B, S // tq),
        in_specs=[pl.BlockSpec(memory_space=pltpu.SMEM),
                  tok(D),
                  pl.BlockSpec((1, 6, D), mod_map),
                  tok(SGU_WIDTH), tok(SGU_WIDTH), tok(ATTN_WIDTH),
                  per_batch(Sk), per_batch(Sk), per_batch(Lc), per_batch(Lc),
                  pl.BlockSpec((SGU_GROUPS, CHUNK, CHUNK), lambda b, i: (0, 0, 0)),
                  pl.BlockSpec((CHUNK, SGU_GROUPS), lambda b, i: (0, 0)),
                  pl.BlockSpec((D, D), lambda b, i: (0, 0)),
                  pl.BlockSpec((3, ATTN_BLOCK, 3 * ATTN_BLOCK), lambda b, i: (0, 0, 0))],
        out_specs=tok(D),
        out_shape=jax.ShapeDtypeStruct((B, S, D), F32),
        scratch_shapes=[pltpu.VMEM((tq, D), BF16),
                        pltpu.VMEM((2 * ATTN_BLOCK, 2 * nk), F32),
                        pltpu.VMEM((2 * ATTN_BLOCK, 2 * nk), BF16),
                        pltpu.VMEM((2 * ATTN_BLOCK, 2 * LANES), F32)],
        compiler_params=_cparams(("parallel", "arbitrary")),
        name="mixer_ctx" if is_ctx else "mixer_even",
    )(sink, x, mod, u, v, q, k2, v2, kc2, vc2, ws, bs_t, wout, _window_bias())


def _route_kernel(x_ref, mod_ref, nw_ref, rw_ref, meta_ref, cnt_ref, tb_ref, xs_ref,
                  base_ref, stage_ref, zero_ref, sc_vmem, sc_smem, sem, zsem, ssem, *, cap, tg):
    step = pl.program_id(0) * pl.num_programs(1) + pl.program_id(1)
    n_steps = pl.num_programs(0) * pl.num_programs(1)
    slot = step % 2

    def run_copies(slot_, action):
        for e in range(N_EXPERTS):
            first = e * cap + sc_smem[slot_, 0, e]
            count = sc_smem[slot_, 1, e]
            off = sc_smem[slot_, 2, e]
            n_big = count >> (DISPATCH_CHUNK.bit_length() - 1)
            n_small = (count - n_big * DISPATCH_CHUNK) >> (SUBLANES.bit_length() - 1)

            def copy(row, rows, first=first, off=off):
                return pltpu.make_async_copy(
                    stage_ref.at[slot_, pl.ds(pl.multiple_of(off + row, SUBLANES), rows)],
                    xs_ref.at[pl.ds(pl.multiple_of(first + row, SUBLANES), rows)], sem.at[slot_])

            def big(c, carry, copy=copy):
                action(copy(c * DISPATCH_CHUNK, DISPATCH_CHUNK))
                return carry

            def small(c, carry, copy=copy, n_big=n_big):
                action(copy(n_big * DISPATCH_CHUNK + c * SUBLANES, SUBLANES))
                return carry

            lax.fori_loop(0, n_big, big, 0)
            lax.fori_loop(0, n_small, small, 0)

    @pl.when(step == 0)
    def _():
        base_ref[...] = jnp.zeros_like(base_ref)

    @pl.when(step >= 2)
    def _():
        run_copies(slot, lambda cp: cp.wait())

    m = mod_ref[0]
    h = _rms_mod(x_ref[0], nw_ref[...], m[4:5], m[3:4])
    w = rw_ref[...]
    w_hi = w.astype(BF16)
    w_lo = (w - w_hi.astype(F32)).astype(BF16)
    h_hi = h.astype(BF16)
    h_lo = (h - h_hi.astype(F32)).astype(BF16)
    logits = _dot(h_hi, w_hi) + (_dot(h_lo, w_hi) + _dot(h_hi, w_lo))
    tm = logits.shape[0]
    lane = lax.broadcasted_iota(jnp.int32, logits.shape, 1)
    lg = jnp.where(lane < N_EXPERTS, logits, -jnp.inf)
    m1 = jnp.max(lg, axis=-1, keepdims=True)
    i1 = jnp.min(jnp.where(lg == m1, lane, LANES), axis=-1, keepdims=True)
    lg2 = jnp.where(lane == i1, -jnp.inf, lg)
    m2 = jnp.max(lg2, axis=-1, keepdims=True)
    i2 = jnp.min(jnp.where(lg2 == m2, lane, LANES), axis=-1, keepdims=True)
    e2 = jnp.exp(m2 - m1)
    den = 1.0 + e2
    hot = jnp.where(jnp.logical_or(lane == i1, lane == i2), 1.0, 0.0)
    r = lax.broadcasted_iota(jnp.int32, (tm, tm), 0)
    c = lax.broadcasted_iota(jnp.int32, (tm, tm), 1)
    before = jnp.where(r > c, 1.0, 0.0).astype(BF16)
    excl = _dot(before, hot.astype(BF16))
    base = base_ref[...]
    count = jnp.sum(hot, axis=0, keepdims=True)
    padded = jnp.floor((count + (SUBLANES - 1)) * (1.0 / SUBLANES)) * SUBLANES
    er = lax.broadcasted_iota(jnp.int32, (LANES, LANES), 0)
    ec = lax.broadcasted_iota(jnp.int32, (LANES, LANES), 1)
    lower = jnp.where(er < ec, 1.0, 0.0).astype(BF16)
    groups = jnp.broadcast_to(padded * (1.0 / SUBLANES), (SUBLANES, LANES)).astype(BF16)
    stage_off = _dot(groups, lower)[0:1] * SUBLANES
    pick = lambda idx, table: jnp.sum(jnp.where(lane == idx, table, 0.0), axis=-1, keepdims=True)
    tot = base + excl
    pos1 = i1.astype(F32) * cap + pick(i1, tot)
    pos2 = i2.astype(F32) * cap + pick(i2, tot)
    loc1 = pick(i1, stage_off + excl)
    loc2 = pick(i2, stage_off + excl)
    meta = (jnp.where(lane == 0, pos1, 0.0) + jnp.where(lane == 1, pos2, 0.0)
            + jnp.where(lane == 2, 1.0 / den, 0.0) + jnp.where(lane == 3, e2 / den, 0.0)
            + jnp.where(lane == 4, i1.astype(F32), 0.0) + jnp.where(lane == 5, i2.astype(F32), 0.0)
            + jnp.where(lane == 6, loc1, 0.0) + jnp.where(lane == 7, loc2, 0.0))
    meta_ref[0] = meta
    sub = lax.broadcasted_iota(jnp.int32, (SUBLANES, LANES), 0)
    tb = jnp.zeros((SUBLANES, LANES), F32)
    for j in range(tm // COMBINE_TILE):
        tb = jnp.where(sub == j, tot[j * COMBINE_TILE:j * COMBINE_TILE + 1, :], tb)
    tb_ref[0, 0] = tb
    base_ref[...] = base + padded
    cnt_ref[...] = jnp.broadcast_to(base + padded, cnt_ref.shape)

    meta_t = meta.T
    j = lax.broadcasted_iota(jnp.int32, (stage_ref.shape[1], tm), 0)
    sel = jnp.logical_or(j == meta_t[6:7].astype(jnp.int32), j == meta_t[7:8].astype(jnp.int32))
    stage_ref[slot] = _dot(jnp.where(sel, 1.0, 0.0).astype(BF16), h_hi)
    record = jnp.where(sub == 0, base, jnp.where(sub == 1, padded, jnp.where(sub == 2, stage_off, 0.0)))
    sc_vmem[...] = record.astype(jnp.int32)
    to_smem = pltpu.make_async_copy(sc_vmem, sc_smem.at[slot], ssem)
    to_smem.start()
    to_smem.wait()
    run_copies(slot, lambda cp: cp.start())

    @pl.when(step == n_steps - 1)
    def _():
        zero_ref[...] = jnp.zeros_like(zero_ref)

        def tail_copy(e):
            end = sc_smem[slot, 0, e] + sc_smem[slot, 1, e]
            return pltpu.make_async_copy(zero_ref, xs_ref.at[pl.ds(pl.multiple_of(e * cap + end, SUBLANES), tg)], zsem)

        for e in range(N_EXPERTS):
            tail_copy(e).start()
        for e in range(N_EXPERTS):
            tail_copy(e).wait()
        run_copies(slot, lambda cp: cp.wait())

        @pl.when(n_steps >= 2)
        def _():
            run_copies(1 - slot, lambda cp: cp.wait())


def _route(x, mod, nw, rw, tm, cap, tg):
    B, S, D = x.shape
    n_stage = 2 * tm + N_EXPERTS * SUBLANES
    return pl.pallas_call(
        functools.partial(_route_kernel, cap=cap, tg=tg),
        grid=(B, S // tm),
        in_specs=[pl.BlockSpec((1, tm, D), lambda b, i: (b, i, 0)),
                  pl.BlockSpec((1, 6, D), lambda b, i: (b, 0, 0)),
                  pl.BlockSpec((1, D), lambda b, i: (0, 0)),
                  pl.BlockSpec((D, LANES), lambda b, i: (0, 0))],
        out_specs=[pl.BlockSpec((1, tm, LANES), lambda b, i: (b, i, 0)),
                   pl.BlockSpec((SUBLANES, LANES), lambda b, i: (0, 0)),
                   pl.BlockSpec((1, 1, SUBLANES, LANES), lambda b, i: (b, i, 0, 0)),
                   pl.BlockSpec(memory_space=pl.ANY)],
        out_shape=[jax.ShapeDtypeStruct((B, S, LANES), F32),
                   jax.ShapeDtypeStruct((SUBLANES, LANES), F32),
                   jax.ShapeDtypeStruct((B, S // tm, SUBLANES, LANES), F32),
                   jax.ShapeDtypeStruct((N_EXPERTS * cap, D), F32)],
        scratch_shapes=[pltpu.VMEM((1, LANES), F32),
                        pltpu.VMEM((2, n_stage, D), F32),
                        pltpu.VMEM((tg, D), F32),
                        pltpu.VMEM((SUBLANES, LANES), jnp.int32),
                        pltpu.SMEM((2, SUBLANES, LANES), jnp.int32),
                        pltpu.SemaphoreType.DMA((2,)),
                        pltpu.SemaphoreType.DMA,
                        pltpu.SemaphoreType.DMA],
        compiler_params=_cparams(("arbitrary", "arbitrary")),
        name="moe_route",
    )(x, mod, nw, rw)


def _moe_group_kernel(blk_ref, exp_ref, fa_ref, fb_ref, nact_ref, x_ref, w1_ref, w3_ref, w2_ref, o_ref, acc_ref):
    t = pl.program_id(0)
    f = pl.program_id(1)

    @pl.when(jnp.logical_and(t == 0, f == 0))
    def _():
        acc_ref[...] = jnp.zeros_like(acc_ref)

    @pl.when(t < nact_ref[0])
    def _():
        h = x_ref[...].astype(BF16)
        a = _dot(h, w1_ref[0])
        b = _dot(h, w3_ref[0])
        y = _dot(((a * _sigmoid(a)) * b).astype(BF16), w2_ref[0])
        total = jnp.where(f == 0, 0.0, acc_ref[...]) + y
        acc_ref[...] = total
        o_ref[...] = total.astype(o_ref.dtype)


def _moe_group(blk, exp, fa, fb, nact, xs, w1, w3, w2, tg):
    R, D = xs.shape
    F = w1.shape[2]
    tf = F // 2
    fsel = lambda f, fa, fb, t: jnp.where(f == 0, fa[t], fb[t])
    return pl.pallas_call(
        _moe_group_kernel,
        grid_spec=pltpu.PrefetchScalarGridSpec(
            num_scalar_prefetch=5,
            grid=(blk.shape[0], 2),
            in_specs=[pl.BlockSpec((tg, D), lambda t, f, blk, exp, fa, fb, na: (blk[t], 0)),
                      pl.BlockSpec((1, D, tf), lambda t, f, blk, exp, fa, fb, na: (exp[t], 0, fsel(f, fa, fb, t))),
                      pl.BlockSpec((1, D, tf), lambda t, f, blk, exp, fa, fb, na: (exp[t], 0, fsel(f, fa, fb, t))),
                      pl.BlockSpec((1, tf, D), lambda t, f, blk, exp, fa, fb, na: (exp[t], fsel(f, fa, fb, t), 0))],
            out_specs=pl.BlockSpec((tg, D), lambda t, f, blk, exp, fa, fb, na: (blk[t], 0)),
            scratch_shapes=[pltpu.VMEM((tg, D), F32)]),
        out_shape=jax.ShapeDtypeStruct((R, D), BF16),
        compiler_params=_cparams(("arbitrary", "arbitrary")),
        name="moe_experts",
    )(blk, exp, fa, fb, nact, xs, w1, w3, w2)


def _combine_kernel(tb_ref, x_ref, mod_ref, meta_ref, fn_ref, y_ref, o_ref, buf_ref, sem, *, cap):
    i = pl.program_id(0)
    n = pl.num_programs(0)
    tc = x_ref.shape[0]
    n_rows = buf_ref.shape[1]
    shift = COMBINE_CHUNK.bit_length() - 1

    def segments(tile):
        segs, off = [], 0
        for e in range(N_EXPERTS):
            first = tb_ref[tile * N_EXPERTS + e]
            count = tb_ref[(tile + 1) * N_EXPERTS + e] - first
            lead = first & (COMBINE_CHUNK - 1)
            n_chunks = jnp.where(count > 0, lax.shift_right_logical(lead + count + COMBINE_CHUNK - 1, shift), 0)
            segs.append((e * cap + first - lead, n_chunks, off))
            off = off + n_chunks * COMBINE_CHUNK
        return segs

    def chunk_copy(src_row, dst_row, slot):
        return pltpu.make_async_copy(y_ref.at[pl.ds(pl.multiple_of(src_row, COMBINE_CHUNK), COMBINE_CHUNK)],
                                     buf_ref.at[slot, pl.ds(pl.multiple_of(dst_row, COMBINE_CHUNK), COMBINE_CHUNK)],
                                     sem.at[slot])

    def for_each_chunk(tile, slot, action):
        for src, n_chunks, off in segments(tile):
            def body(c, carry):
                action(chunk_copy(src + c * COMBINE_CHUNK, off + c * COMBINE_CHUNK, slot))
                return carry

            lax.fori_loop(0, n_chunks, body, 0)

    @pl.when(i == 0)
    def _():
        buf_ref[...] = jnp.zeros_like(buf_ref)
        for_each_chunk(0, 0, lambda cp: cp.start())

    slot = i % 2

    @pl.when(i + 1 < n)
    def _():
        for_each_chunk(i + 1, 1 - slot, lambda cp: cp.start())

    for_each_chunk(i, slot, lambda cp: cp.wait())
    meta = meta_ref[...]
    segs = segments(i)

    def one_hot(pos, exp):
        delta = jnp.zeros_like(pos)
        for e, (src, _, off) in enumerate(segs):
            delta = jnp.where(exp == e, off - src, delta)
        col = lax.broadcasted_iota(jnp.int32, (tc, n_rows), 1)
        return jnp.where(col == pos + delta, 1.0, 0.0).astype(BF16)

    as_int = lambda lane: meta[:, lane:lane + 1].astype(jnp.int32)
    picks = jnp.concatenate([one_hot(as_int(0), as_int(4)), one_hot(as_int(1), as_int(5))], axis=0)
    y12 = _dot(picks, buf_ref[slot])
    mix = meta[:, 2:3] * y12[0:tc] + meta[:, 3:4] * y12[tc:2 * tc]
    o_ref[...] = _rms(x_ref[...] + mod_ref[0][5:6] * mix, fn_ref[...])


def _combine(tile_base, x, mod, meta, fin, y, seq_len, cap):
    T, D = x.shape
    tc = COMBINE_TILE
    per_batch = seq_len // tc
    n_rows = -(-(2 * tc + N_EXPERTS * (2 * (COMBINE_CHUNK - 1) + SUBLANES - 1)) // LANES) * LANES
    return pl.pallas_call(
        functools.partial(_combine_kernel, cap=cap),
        grid_spec=pltpu.PrefetchScalarGridSpec(
            num_scalar_prefetch=1,
            grid=(T // tc,),
            in_specs=[pl.BlockSpec((tc, D), lambda i, tb: (i, 0)),
                      pl.BlockSpec((1, 6, D), lambda i, tb: (i // per_batch, 0, 0)),
                      pl.BlockSpec((tc, LANES), lambda i, tb: (i, 0)),
                      pl.BlockSpec((1, D), lambda i, tb: (0, 0)),
                      pl.BlockSpec(memory_space=pl.ANY)],
            out_specs=pl.BlockSpec((tc, D), lambda i, tb: (i, 0)),
            scratch_shapes=[pltpu.VMEM((2, n_rows, D), BF16), pltpu.SemaphoreType.DMA((2,))]),
        out_shape=jax.ShapeDtypeStruct((T, D), F32),
        compiler_params=_cparams(("arbitrary",)),
        name="moe_combine",
    )(tile_base, x, mod, meta, fin, y)


def _moe_tiles(cnt, n_tiles, tg, cap):
    per = (cnt + tg - 1) // tg
    cum = jnp.cumsum(per)
    nact = cum[-1]
    t = jnp.arange(n_tiles, dtype=jnp.int32)
    tt = jnp.minimum(t, nact - 1)
    exp = jnp.minimum(jnp.sum((tt[:, None] >= cum[None, :]).astype(jnp.int32), axis=1), N_EXPERTS - 1)
    blk = exp * (cap // tg) + tt - (cum - per)[exp]
    odd = tt % 2
    fa = jnp.where(t < nact, odd, 1 - odd)
    fb = 1 - odd
    i32 = lambda v: v.astype(jnp.int32)
    return i32(blk), i32(exp), i32(fa), i32(fb), i32(nact.reshape(1))


def _ffn_kernel(*refs, n_cast):
    x_ref, mod_ref, nw_ref, w1_ref, w3_ref, w2_ref = refs[0:6]
    cast_in = refs[6:6 + n_cast]
    o_ref = refs[6 + n_cast]
    cast_out = refs[7 + n_cast:7 + 2 * n_cast]
    (acc_ref,) = refs[7 + 2 * n_cast:]
    f = pl.program_id(2)

    @pl.when(jnp.logical_and(jnp.logical_and(pl.program_id(0) == 0, pl.program_id(1) == 0), f == 0))
    def _():
        acc_ref[...] = jnp.zeros_like(acc_ref)

    for src, dst in zip(cast_in, cast_out):
        dst[...] = src[...].astype(dst.dtype)
    m = mod_ref[0]
    x = x_ref[0]
    h = _rms_mod(x, nw_ref[...], m[4:5], m[3:4]).astype(BF16)
    a = _dot(h, w1_ref[...])
    b = _dot(h, w3_ref[...])
    total = jnp.where(f == 0, 0.0, acc_ref[...]) + _dot(((a * _sigmoid(a)) * b).astype(BF16), w2_ref[...])
    acc_ref[...] = total
    o_ref[0] = x + m[5:6] * total


def _ffn(x, mod, mod_per_batch, nw, w1, w3, w2, tm, tf, cast=()):
    B, S, D = x.shape
    F = w1.shape[1]
    n_i, n_f = S // tm, F // tf
    n_steps = B * n_i * n_f
    mod_map = (lambda b, i, f: (b, 0, 0)) if mod_per_batch else (lambda b, i, f: (0, 0, 0))
    cast_specs, cast_shapes = [], []
    for arr in cast:
        E, R, C = arr.shape
        per = n_steps // E
        assert per * E == n_steps and R % (per * 2 * SUBLANES) == 0, (arr.shape, n_steps)

        def cast_map(b, i, f, per=per):
            step = (b * n_i + i) * n_f + f
            return step // per, step % per, 0

        cast_specs.append(pl.BlockSpec((1, R // per, C), cast_map))
        cast_shapes.append(jax.ShapeDtypeStruct(arr.shape, BF16))
    tok = pl.BlockSpec((1, tm, D), lambda b, i, f: (b, i, 0))
    outs = pl.pallas_call(
        functools.partial(_ffn_kernel, n_cast=len(cast)),
        grid=(B, n_i, n_f),
        in_specs=[tok,
                  pl.BlockSpec((1, 6, D), mod_map),
                  pl.BlockSpec((1, D), lambda b, i, f: (0, 0)),
                  pl.BlockSpec((D, tf), lambda b, i, f: (0, f)),
                  pl.BlockSpec((D, tf), lambda b, i, f: (0, f)),
                  pl.BlockSpec((tf, D), lambda b, i, f: (f, 0))] + cast_specs,
        out_specs=[tok] + cast_specs,
        out_shape=[jax.ShapeDtypeStruct((B, S, D), F32)] + cast_shapes,
        scratch_shapes=[pltpu.VMEM((tm, D), F32)],
        compiler_params=_cparams(("parallel", "parallel", "arbitrary")),
        name="ffn",
    )(x, mod, nw, w1, w3, w2, *cast)
    return outs[0], tuple(outs[1:])


def _proj_odd_kernel(x_ref, mod_ref, nw_ref, w_ref, gate_ref, rec_ref):
    m = mod_ref[0]
    h = _rms_mod(x_ref[0], nw_ref[...], m[1:2], m[0:1]).astype(BF16)
    gate_ref[0] = _gelu(_dot(h, w_ref[:, 0:D_RNN])).astype(gate_ref.dtype)
    rec_ref[0] = _dot(h, w_ref[:, D_RNN:2 * D_RNN])


def _proj_odd(x, mod, mod_per_batch, nw, w, tm):
    B, S, D = x.shape
    mod_map = (lambda b, i: (b, 0, 0)) if mod_per_batch else (lambda b, i: (0, 0, 0))
    tok = lambda width: pl.BlockSpec((1, tm, width), lambda b, i: (b, i, 0))
    return pl.pallas_call(
        _proj_odd_kernel,
        grid=(B, S // tm),
        in_specs=[tok(D),
                  pl.BlockSpec((1, 6, D), mod_map),
                  pl.BlockSpec((1, D), lambda b, i: (0, 0)),
                  pl.BlockSpec((D, 2 * D_RNN), lambda b, i: (0, 0))],
        out_specs=[tok(D_RNN), tok(D_RNN)],
        out_shape=[jax.ShapeDtypeStruct((B, S, D_RNN), BF16), jax.ShapeDtypeStruct((B, S, D_RNN), F32)],
        compiler_params=_cparams(("parallel", "parallel")),
        name="proj_odd",
    )(x, mod, nw, w)


def _scan8(a, b, h, row, reverse):
    for s in (1, 2, 4):
        if reverse:
            a_s, b_s, live = pltpu.roll(a, SUBLANES - s, 0), pltpu.roll(b, SUBLANES - s, 0), row < SUBLANES - s
        else:
            a_s, b_s, live = pltpu.roll(a, s, 0), pltpu.roll(b, s, 0), row >= s
        b = jnp.where(live, a * b_s + b, b)
        a = jnp.where(live, a * a_s, a)
    hr = a * h + b
    return hr, (hr[0:1] if reverse else hr[SUBLANES - 1:SUBLANES])


def _lru_kernel(rec_ref, recc_ref, cw_ref, cb_ref, wa_ref, ba_ref, wx_ref, bx_ref, lam_ref,
                s_ref, pad_ref, a_ref, b_ref, cpad_ref, ca_ref, cbb_ref, *, tile):
    S = rec_ref.shape[1]
    L = recc_ref.shape[1]
    cw = cw_ref[...]
    cb = cb_ref[...]
    lam = lam_ref[...]
    sp = jnp.maximum(-lam, 0.0) + jnp.log1p(jnp.exp(-jnp.abs(lam)))
    zeros8 = jnp.zeros((SUBLANES, LANES), F32)

    def coefficients(src_ref, dst_a, dst_b, n_rows, t):
        pad = cpad_ref if src_ref is recc_ref else pad_ref
        pad[0:SUBLANES, :] = zeros8
        pad[SUBLANES + n_rows:2 * SUBLANES + n_rows, :] = zeros8

        def copy(j, carry):
            r0 = pl.multiple_of(j * t, t)
            pad[pl.ds(SUBLANES + r0, t), :] = src_ref[0, pl.ds(r0, t), :]
            return carry

        lax.fori_loop(0, n_rows // t, copy, 0)

        def body(j, carry):
            r0 = pl.multiple_of(j * t, t)
            ext = pad[pl.ds(r0, t + 2 * SUBLANES), :]
            conv = cb
            for tap in range(4):
                conv = conv + cw[tap:tap + 1] * ext[SUBLANES - 2 + tap:SUBLANES - 2 + tap + t]
            cbf = conv.astype(BF16)
            for d in range(2):
                r = _sigmoid(_dot(cbf, wa_ref[d, 0]) + ba_ref[d:d + 1])
                gi = _sigmoid(_dot(cbf, wx_ref[d, 0]) + bx_ref[d:d + 1])
                log_a = -LRU_C * r * sp[d:d + 1]
                a = jnp.exp(log_a)
                y = -jnp.tanh(0.5 * log_a)
                root = jnp.where(y > 0.0, y * lax.rsqrt(y), 0.0)
                dst_a[d, pl.ds(r0, t), :] = a
                dst_b[d, pl.ds(r0, t), :] = (root * (1.0 + a)) * (gi * conv)
            return carry

        lax.fori_loop(0, n_rows // t, body, 0)

    row = lax.broadcasted_iota(jnp.int32, (SUBLANES, LANES), 0)
    h_zero = jnp.zeros((1, LANES), F32)

    coefficients(recc_ref, ca_ref, cbb_ref, L, L)
    nc = L // SUBLANES

    def ctx_body(j, carry):
        hf, hb = carry
        rf = pl.multiple_of(j * SUBLANES, SUBLANES)
        rb = pl.multiple_of((nc - 1 - j) * SUBLANES, SUBLANES)
        _, hf = _scan8(ca_ref[0, pl.ds(rf, SUBLANES), :], cbb_ref[0, pl.ds(rf, SUBLANES), :], hf, row, False)
        _, hb = _scan8(ca_ref[1, pl.ds(rb, SUBLANES), :], cbb_ref[1, pl.ds(rb, SUBLANES), :], hb, row, True)
        return hf, hb

    h0f, h0b = lax.fori_loop(0, nc, ctx_body, (h_zero, h_zero))

    coefficients(rec_ref, a_ref, b_ref, S, tile)
    n = S // SUBLANES

    def lat_body(accumulate):
        def body(j, carry):
            hf, hb = carry
            rf = pl.multiple_of(j * SUBLANES, SUBLANES)
            rb = pl.multiple_of((n - 1 - j) * SUBLANES, SUBLANES)
            of, hf = _scan8(a_ref[0, pl.ds(rf, SUBLANES), :], b_ref[0, pl.ds(rf, SUBLANES), :], hf, row, False)
            ob, hb = _scan8(a_ref[1, pl.ds(rb, SUBLANES), :], b_ref[1, pl.ds(rb, SUBLANES), :], hb, row, True)
            if accumulate:
                s_ref[0, pl.ds(rf, SUBLANES), :] += of
                s_ref[0, pl.ds(rb, SUBLANES), :] += ob
            else:
                s_ref[0, pl.ds(rf, SUBLANES), :] = of
                s_ref[0, pl.ds(rb, SUBLANES), :] = ob
            return hf, hb
        return body

    mid = lax.fori_loop(0, n // 2, lat_body(False), (h0f, h0b), unroll=8)
    lax.fori_loop(n // 2, n, lat_body(True), mid, unroll=8)


def _lru(rec, rec_c, conv_w, conv_b, wa, ba, wx, bx, lam, tile):
    B, S, _ = rec.shape
    L = rec_c.shape[1]
    blk = lambda rows: pl.BlockSpec((1, rows, LRU_BLOCK), lambda b, j: (b, 0, j))
    vec = lambda rows: pl.BlockSpec((rows, LRU_BLOCK), lambda b, j: (0, j))
    wspec = pl.BlockSpec((2, 1, LRU_BLOCK, LRU_BLOCK), lambda b, j: (0, j, 0, 0))
    return pl.pallas_call(
        functools.partial(_lru_kernel, tile=tile),
        grid=(B, LRU_BLOCKS),
        in_specs=[blk(S), blk(L), vec(4), vec(1), wspec, vec(2), wspec, vec(2), vec(2)],
        out_specs=blk(S),
        out_shape=jax.ShapeDtypeStruct((B, S, D_RNN), F32),
        scratch_shapes=[pltpu.VMEM((S + 2 * SUBLANES, LRU_BLOCK), F32),
                        pltpu.VMEM((2, S, LRU_BLOCK), F32),
                        pltpu.VMEM((2, S, LRU_BLOCK), F32),
                        pltpu.VMEM((L + 2 * SUBLANES, LRU_BLOCK), F32),
                        pltpu.VMEM((2, L, LRU_BLOCK), F32),
                        pltpu.VMEM((2, L, LRU_BLOCK), F32)],
        compiler_params=_cparams(("parallel", "parallel")),
        name="lru_scan",
    )(rec, rec_c, conv_w, conv_b, wa, ba, wx, bx, lam)


def _lru_out_kernel(x_ref, mod_ref, gate_ref, s_ref, w_ref, o_ref):
    y = (gate_ref[0].astype(F32) * s_ref[0]).astype(BF16)
    o_ref[0] = x_ref[0] + mod_ref[0][2:3] * _dot(y, w_ref[...])


def _lru_out(x, mod, gate, s, w, tm):
    B, S, D = x.shape
    tok = lambda width: pl.BlockSpec((1, tm, width), lambda b, i: (b, i, 0))
    return pl.pallas_call(
        _lru_out_kernel,
        grid=(B, S // tm),
        in_specs=[tok(D), pl.BlockSpec((1, 6, D), lambda b, i: (b, 0, 0)), tok(D_RNN), tok(D_RNN),
                  pl.BlockSpec((D_RNN, D), lambda b, i: (0, 0))],
        out_specs=tok(D),
        out_shape=jax.ShapeDtypeStruct((B, S, D), F32),
        compiler_params=_cparams(("parallel", "parallel")),
        name="lru_out",
    )(x, mod, gate, s, w)


def _rope_tables(n_tok):
    rows = n_tok // GRID_W
    row = jnp.repeat(jnp.arange(rows, dtype=F32), GRID_W)
    col = jnp.tile(jnp.arange(GRID_W, dtype=F32), rows)
    freqs = ROPE_BASE ** (-jnp.arange(ROPE_FREQS, dtype=F32) / ROPE_FREQS)
    ar, ac = row[:, None] * freqs, col[:, None] * freqs
    cos = jnp.concatenate([jnp.cos(ar), jnp.cos(ar), jnp.cos(ac), jnp.cos(ac)], axis=-1)
    sin = jnp.concatenate([-jnp.sin(ar), jnp.sin(ar), -jnp.sin(ac), jnp.sin(ac)], axis=-1)
    return jnp.tile(cos, (1, LANES // HEAD_DIM)), jnp.tile(sin, (1, LANES // HEAD_DIM))


def kernel(x, c, ctx, c_ctx, ada_w_e, ada_b_e, norm1_e, norm2_e, w_in_e, sgu_w, sgu_b, attn_sink, w_out_e, ffn_w1, ffn_w3, ffn_w2, ada_w_o, ada_b_o, norm1_o, norm2_o, w_in_o, conv_w, conv_b, lru_wa, lru_ba, lru_wx, lru_bx, lru_lambda, w_out_o, router_w, moe_w1, moe_w3, moe_w2, final_norm):
    B, S, D = x.shape
    L = ctx.shape[1]
    cvec = jnp.concatenate([c, c_ctx[None], jnp.zeros((SUBLANES - B - 1, D), F32)], axis=0)
    mod_e = _ada_params(cvec, ada_w_e[0], ada_b_e[0])
    mod_o = _ada_params(cvec, ada_w_o[0], ada_b_o[0])
    lat_e, ctx_e = mod_e[0:B], mod_e[B:B + 1]
    lat_o, ctx_o = mod_o[0:B], mod_o[B:B + 1]
    bf = lambda t: t.astype(BF16)
    row = lambda t: t.reshape(1, -1)

    cos, sin = _rope_tables(S)
    cos_c, sin_c = jnp.ones((L, LANES), F32), jnp.zeros((L, LANES), F32)
    w_in = bf(w_in_e[0])
    n1, n2 = row(norm1_e[0]), row(norm2_e[0])
    uc, vc, qc, kc2, vc2 = _proj_even(ctx, ctx_e, False, n1, w_in, cos_c, sin_c, L)
    u, v, q, k2, v2 = _proj_even(x, lat_e, True, n1, w_in, cos, sin, 1024)
    ws, bs_t, wout = bf(sgu_w[0]), sgu_b[0].T, bf(w_out_e[0])
    sink = attn_sink[0] * LOG2E
    x = _mixer_even(x, lat_e, True, u, v, q, k2, v2, kc2, vc2, ws, bs_t, sink, wout, 512, False)
    xc = _mixer_even(ctx, ctx_e, False, uc, vc, qc, kc2, vc2, kc2, vc2, ws, bs_t, sink, wout, L, True)
    w1, w3, w2 = bf(ffn_w1[0]), bf(ffn_w3[0]), bf(ffn_w2[0])
    x, moe_w = _ffn(x, lat_e, True, n2, w1, w3, w2, 512, 1408, cast=(moe_w1[0], moe_w3[0], moe_w2[0]))
    xc, _ = _ffn(xc, ctx_e, False, n2, w1, w3, w2, L, 1408)

    w_in = bf(w_in_o[0])
    n1, n2 = row(norm1_o[0]), row(norm2_o[0])
    _, rec_c = _proj_odd(xc, ctx_o, False, n1, w_in, L)
    gate, rec = _proj_odd(x, lat_o, True, n1, w_in, 1024)
    s = _lru(rec, rec_c, conv_w[0], row(conv_b[0]), bf(lru_wa[0]), lru_ba[0], bf(lru_wx[0]), lru_bx[0],
             lru_lambda[0], 512)
    x = _lru_out(x, lat_o, gate, s, bf(w_out_o[0]), 1024)
    rw = jnp.pad(router_w[0], ((0, 0), (0, LANES - N_EXPERTS)))
    T = B * S
    tg = 512
    tm = 512
    run_pad = (T // tm) * (SUBLANES - 1)
    cap = -(-(T + run_pad) // tg) * tg + 2 * tg
    meta, cnt, tb4, xs = _route(x, lat_o, n2, rw, tm, cap, tg)
    cnt = cnt[0, 0:N_EXPERTS].astype(jnp.int32)
    tile_base = jnp.concatenate([tb4[:, :, 0:tm // COMBINE_TILE, 0:N_EXPERTS].astype(jnp.int32).reshape(-1), cnt])
    n_tiles = -(-(2 * T + N_EXPERTS * run_pad) // tg) + N_EXPERTS
    blk, exp, fa, fb, nact = _moe_tiles(cnt, n_tiles, tg, cap)
    y = _moe_group(blk, exp, fa, fb, nact, xs, *moe_w, tg)
    out = _combine(tile_base, x.reshape(T, D), lat_o, meta.reshape(T, LANES), row(final_norm), y, S, cap)
    return out.reshape(B, S, D)
```

```python
import functools

import jax
import jax.numpy as jnp
from jax import lax
from jax.experimental import pallas as pl
from jax.experimental.pallas import tpu as pltpu

F32 = jnp.float32
BF16 = jnp.bfloat16

D_MODEL = 1024
GRID_W = 64
EPS = 1e-6
NEG_INF = -1e30
CHUNK = 128
SGU_GROUPS = 4
SGU_WIDTH = 512
HEAD_DIM = 64
N_Q_HEADS = 8
N_KV_HEADS = 2
ATTN_WIDTH = 512
KV_WIDTH = 128
WINDOW = 128
ATTN_BLOCK = 128
ATTN_SCALE = HEAD_DIM ** -0.5
LOG2E = 1.4426950408889634
ROW_BLOCK = 32
COMBINE_TILE = 256
COMBINE_CHUNK = 16
DISPATCH_CHUNK = 64
ROPE_BASE = 10000.0
ROPE_FREQS = 16
IN_EVEN = 1792
D_RNN = 1280
LRU_BLOCKS = 10
LRU_BLOCK = 128
LRU_C = 8.0
D_FF = 2816
N_EXPERTS = 8
LANES = 128
SUBLANES = 8
VMEM_LIMIT = 56 * 1024 * 1024


def _cparams(sem):
    return pltpu.CompilerParams(dimension_semantics=sem, vmem_limit_bytes=VMEM_LIMIT)


def _dot(a, b):
    return jnp.dot(a, b, preferred_element_type=F32)


def _dot_nt(a, b):
    return lax.dot_general(a, b, (((1,), (1,)), ((), ())), preferred_element_type=F32)


def _gelu(x):
    return 0.5 * x * (1.0 + jnp.tanh(0.7978845608028654 * (x + 0.044715 * (x * x * x))))


def _sigmoid(x):
    return 0.5 * jnp.tanh(0.5 * x) + 0.5


def _rms(x, nw):
    return (x * lax.rsqrt(jnp.mean(x * x, axis=-1, keepdims=True) + EPS)) * nw


def _rms_mod(x, nw, scale, shift):
    return _rms(x, nw) * (1.0 + scale) + shift


def _ada_kernel(c_ref, w_ref, b_ref, o_ref):
    c = c_ref[...]
    act = c * _sigmoid(c)
    o_ref[...] = _dot(act.astype(BF16), w_ref[...].astype(BF16)) + b_ref[...]


def _ada_params(cvec, w, b):
    n = w.shape[1]
    tn = 1536
    out = pl.pallas_call(
        _ada_kernel,
        grid=(n // tn,),
        in_specs=[pl.BlockSpec((SUBLANES, D_MODEL), lambda j: (0, 0)),
                  pl.BlockSpec((D_MODEL, tn), lambda j: (0, j)),
                  pl.BlockSpec((1, tn), lambda j: (0, j))],
        out_specs=pl.BlockSpec((SUBLANES, tn), lambda j: (0, j)),
        out_shape=jax.ShapeDtypeStruct((SUBLANES, n), F32),
        compiler_params=_cparams(("parallel",)),
        name="ada_params",
    )(cvec, w, b.reshape(1, n))
    return out.reshape(SUBLANES, 6, D_MODEL)


def _proj_even_kernel(x_ref, mod_ref, nw_ref, w_ref, cos_ref, sin_ref,
                      u_ref, v_ref, q_ref, k_ref, val_ref):
    m = mod_ref[0]
    h = _rms_mod(x_ref[0], nw_ref[...], m[1:2], m[0:1]).astype(BF16)
    u_ref[0] = _gelu(_dot(h, w_ref[:, 0:SGU_WIDTH])).astype(u_ref.dtype)
    v_ref[0] = _gelu(_dot(h, w_ref[:, SGU_WIDTH:2 * SGU_WIDTH])).astype(v_ref.dtype)
    cos = cos_ref[...]
    sin = sin_ref[...]
    lane = lax.broadcasted_iota(jnp.int32, cos.shape, 1)
    first_half = (lane % 32) < ROPE_FREQS

    def rope(t):
        partner = jnp.where(first_half, pltpu.roll(t, LANES - ROPE_FREQS, 1), pltpu.roll(t, ROPE_FREQS, 1))
        return t * cos + partner * sin

    q = _dot(h, w_ref[:, 2 * SGU_WIDTH:2 * SGU_WIDTH + ATTN_WIDTH]) * (ATTN_SCALE * LOG2E)
    for g in range(ATTN_WIDTH // LANES):
        q_ref[0, :, g * LANES:(g + 1) * LANES] = rope(q[:, g * LANES:(g + 1) * LANES]).astype(BF16)
    kv = _dot(h, w_ref[:, 2 * SGU_WIDTH + ATTN_WIDTH:IN_EVEN])
    k = rope(kv[:, 0:KV_WIDTH])
    val = kv[:, KV_WIDTH:2 * KV_WIDTH]
    k_ref[0, :, 0:LANES] = k.astype(BF16)
    k_ref[0, :, LANES:2 * LANES] = pltpu.roll(k, HEAD_DIM, 1).astype(BF16)
    val_ref[0, :, 0:LANES] = val.astype(BF16)
    val_ref[0, :, LANES:2 * LANES] = pltpu.roll(val, HEAD_DIM, 1).astype(BF16)


def _proj_even(x, mod, mod_per_batch, nw, w, cos, sin, tm):
    B, S, D = x.shape
    mod_map = (lambda b, i: (b, 0, 0)) if mod_per_batch else (lambda b, i: (0, 0, 0))
    tok = lambda width: pl.BlockSpec((1, tm, width), lambda b, i: (b, i, 0))
    return pl.pallas_call(
        _proj_even_kernel,
        grid=(B, S // tm),
        in_specs=[tok(D),
                  pl.BlockSpec((1, 6, D), mod_map),
                  pl.BlockSpec((1, D), lambda b, i: (0, 0)),
                  pl.BlockSpec((D, IN_EVEN), lambda b, i: (0, 0)),
                  pl.BlockSpec((tm, LANES), lambda b, i: (i, 0)),
                  pl.BlockSpec((tm, LANES), lambda b, i: (i, 0))],
        out_specs=[tok(SGU_WIDTH), tok(SGU_WIDTH), tok(ATTN_WIDTH), tok(2 * KV_WIDTH), tok(2 * KV_WIDTH)],
        out_shape=[jax.ShapeDtypeStruct((B, S, SGU_WIDTH), BF16),
                   jax.ShapeDtypeStruct((B, S, SGU_WIDTH), BF16),
                   jax.ShapeDtypeStruct((B, S, ATTN_WIDTH), BF16),
                   jax.ShapeDtypeStruct((B, S, 2 * KV_WIDTH), BF16),
                   jax.ShapeDtypeStruct((B, S, 2 * KV_WIDTH), BF16)],
        compiler_params=_cparams(("parallel", "parallel")),
        name="proj_even",
    )(x, mod, nw, w, cos, sin)


def _mixer_even_kernel(sink_ref, x_ref, mod_ref, u_ref, v_ref, q_ref, k_ref, val_ref, kc_ref, vc_ref,
                       ws_ref, bs_ref, wout_ref, bias_ref, o_ref, mix_ref, s_ref, p_ref, inv_ref,
                       *, seq_len, is_ctx):
    tq = x_ref.shape[1]
    n_chunks = tq // CHUNK
    i = pl.program_id(1)
    nk = kc_ref.shape[1] + (0 if is_ctx else 3 * ATTN_BLOCK)
    lane = lax.broadcasted_iota(jnp.int32, (1, LANES), 1)
    lo = lane < HEAD_DIM
    zero = jnp.zeros((), BF16)

    def halves(ref_slice, kh):
        nat, swp = ref_slice[:, 0:LANES], ref_slice[:, LANES:2 * LANES]
        if kh == 0:
            return jnp.where(lo, nat, zero), jnp.where(lo, zero, swp)
        return jnp.where(lo, swp, zero), jnp.where(lo, zero, nat)

    kc_all = kc_ref[0]
    vc_all = vc_ref[0]

    def chunk_body(c, carry):
        r0 = pl.multiple_of(c * CHUNK, CHUNK)
        rows = pl.ds(r0, CHUNK)
        vch = v_ref[0, rows, :].astype(F32)
        uch = u_ref[0, rows, :].astype(F32)
        for g in range(SGU_GROUPS):
            cols = slice(g * LANES, (g + 1) * LANES)
            vg = vch[:, cols]
            dev = vg - jnp.mean(vg, axis=-1, keepdims=True)
            vn = dev * lax.rsqrt(jnp.mean(dev * dev, axis=-1, keepdims=True) + EPS)
            mixed = _dot(ws_ref[g], vn.astype(BF16)) + bs_ref[:, g:g + 1]
            mix_ref[rows, cols] = (uch[:, cols] * mixed).astype(BF16)
        qch = q_ref[0, rows, :]
        if not is_ctx:
            blk = i * n_chunks + c
            n_blk = seq_len // ATTN_BLOCK
            start = pl.multiple_of(jnp.clip((blk - 1) * ATTN_BLOCK, 0, seq_len - 3 * ATTN_BLOCK), ATTN_BLOCK)
            k3 = k_ref[0, pl.ds(start, 3 * ATTN_BLOCK), :]
            v3 = val_ref[0, pl.ds(start, 3 * ATTN_BLOCK), :]
            case = jnp.where(blk == 0, 0, jnp.where(blk == n_blk - 1, 2, 1))
        for kh in range(N_KV_HEADS):
            kc_lo, kc_hi = halves(kc_all, kh)
            vc_lo, vc_hi = halves(vc_all, kh)
            if is_ctx:
                k_cat = jnp.concatenate([kc_lo, kc_hi], axis=0)
                v_cat = (vc_lo, vc_hi)
            else:
                k_lo, k_hi = halves(k3, kh)
                v_lo, v_hi = halves(v3, kh)
                k_cat = jnp.concatenate([k_lo, kc_lo, k_hi, kc_hi], axis=0)
                v_cat = (jnp.concatenate([v_lo, vc_lo], axis=0), jnp.concatenate([v_hi, vc_hi], axis=0))
            q2 = jnp.concatenate([qch[:, 2 * kh * LANES:(2 * kh + 1) * LANES],
                                  qch[:, (2 * kh + 1) * LANES:(2 * kh + 2) * LANES]], axis=0)
            s_ref[:, 0:2 * nk] = _dot_nt(q2, k_cat)
            for half in range(2):
                for rb in range(2 * ATTN_BLOCK // ROW_BLOCK):
                    rsl = slice(rb * ROW_BLOCK, (rb + 1) * ROW_BLOCK)
                    snk = sink_ref[2 * (2 * kh + rb * ROW_BLOCK // ATTN_BLOCK) + half]
                    s = s_ref[rsl, half * nk:(half + 1) * nk]
                    if not is_ctx:
                        qoff = (rb * ROW_BLOCK) % ATTN_BLOCK
                        s_loc = s[:, 0:3 * ATTN_BLOCK] + bias_ref[case, qoff:qoff + ROW_BLOCK, :]
                        s = jnp.concatenate([s_loc, s[:, 3 * ATTN_BLOCK:]], axis=1)
                    m = jnp.maximum(jnp.max(s, axis=-1, keepdims=True), snk)
                    p = jnp.exp2(s - m)
                    den = jnp.sum(p, axis=-1, keepdims=True) + jnp.exp2(snk - m)
                    p_ref[rsl, half * nk:(half + 1) * nk] = p.astype(BF16)
                    inv_ref[rsl, half * LANES:(half + 1) * LANES] = jnp.broadcast_to(1.0 / den, (ROW_BLOCK, LANES))
            o_lo = _dot(p_ref[:, 0:nk], v_cat[0])
            o_hi = _dot(p_ref[:, nk:2 * nk], v_cat[1])
            acc = (o_lo * inv_ref[:, 0:LANES] + o_hi * inv_ref[:, LANES:2 * LANES]).astype(BF16)
            for g in range(2):
                col = SGU_WIDTH + (2 * kh + g) * LANES
                mix_ref[rows, col:col + LANES] = acc[g * ATTN_BLOCK:(g + 1) * ATTN_BLOCK]
        return carry

    lax.fori_loop(0, n_chunks, chunk_body, 0)
    y = _dot(mix_ref[...], wout_ref[...])
    o_ref[0] = x_ref[0] + mod_ref[0][2:3] * y


def _window_bias():
    case = jnp.arange(3, dtype=jnp.int32)[:, None, None]
    qi = jnp.arange(ATTN_BLOCK, dtype=jnp.int32)[None, :, None]
    kj = jnp.arange(3 * ATTN_BLOCK, dtype=jnp.int32)[None, None, :]
    return jnp.where(jnp.abs(kj - case * ATTN_BLOCK - qi) <= WINDOW, 0.0, NEG_INF).astype(F32)


def _mixer_even(x, mod, mod_per_batch, u, v, q, k2, v2, kc2, vc2, ws, bs_t, sink, wout, tq, is_ctx):
    B, S, D = x.shape
    Sk = k2.shape[1]
    Lc = kc2.shape[1]
    nk = Lc + (0 if is_ctx else 3 * ATTN_BLOCK)
    mod_map = (lambda b, i: (b, 0, 0)) if mod_per_batch else (lambda b, i: (0, 0, 0))
    tok = lambda width: pl.BlockSpec((1, tq, width), lambda b, i: (b, i, 0))
    per_batch = lambda rows: pl.BlockSpec((1, rows, 2 * KV_WIDTH), lambda b, i: (b, 0, 0))
    return pl.pallas_call(
        functools.partial(_mixer_even_kernel, seq_len=S, is_ctx=is_ctx),
        grid=(B, S // tq),
        in_specs=[pl.BlockSpec(memory_space=pltpu.SMEM),
                  tok(D),
                  pl.BlockSpec((1, 6, D), mod_map),
                  tok(SGU_WIDTH), tok(SGU_WIDTH), tok(ATTN_WIDTH),
                  per_batch(Sk), per_batch(Sk), per_batch(Lc), per_batch(Lc),
                  pl.BlockSpec((SGU_GROUPS, CHUNK, CHUNK), lambda b, i: (0, 0, 0)),
                  pl.BlockSpec((CHUNK, SGU_GROUPS), lambda b, i: (0, 0)),
                  pl.BlockSpec((D, D), lambda b, i: (0, 0)),
                  pl.BlockSpec((3, ATTN_BLOCK, 3 * ATTN_BLOCK), lambda b, i: (0, 0, 0))],
        out_specs=tok(D),
        out_shape=jax.ShapeDtypeStruct((B, S, D), F32),
        scratch_shapes=[pltpu.VMEM((tq, D), BF16),
                        pltpu.VMEM((2 * ATTN_BLOCK, 2 * nk), F32),
                        pltpu.VMEM((2 * ATTN_BLOCK, 2 * nk), BF16),
                        pltpu.VMEM((2 * ATTN_BLOCK, 2 * LANES), F32)],
        compiler_params=_cparams(("parallel", "arbitrary")),
        name="mixer_ctx" if is_ctx else "mixer_even",
    )(sink, x, mod, u, v, q, k2, v2, kc2, vc2, ws, bs_t, wout, _window_bias())


def _route_kernel(x_ref, mod_ref, nw_ref, rw_ref, meta_ref, cnt_ref, tb_ref, xs_ref,
                  base_ref, stage_ref, zero_ref, sc_vmem, sc_smem, sem, zsem, ssem, *, cap, tg):
    step = pl.program_id(0) * pl.num_programs(1) + pl.program_id(1)
    n_steps = pl.num_programs(0) * pl.num_programs(1)
    slot = step % 2

    def run_copies(slot_, action):
        for e in range(N_EXPERTS):
            first = e * cap + sc_smem[slot_, 0, e]
            count = sc_smem[slot_, 1, e]
            off = sc_smem[slot_, 2, e]
            n_big = count >> (DISPATCH_CHUNK.bit_length() - 1)
            n_small = (count - n_big * DISPATCH_CHUNK) >> (SUBLANES.bit_length() - 1)

            def copy(row, rows, first=first, off=off):
                return pltpu.make_async_copy(
                    stage_ref.at[slot_, pl.ds(pl.multiple_of(off + row, SUBLANES), rows)],
                    xs_ref.at[pl.ds(pl.multiple_of(first + row, SUBLANES), rows)], sem.at[slot_])

            def big(c, carry, copy=copy):
                action(copy(c * DISPATCH_CHUNK, DISPATCH_CHUNK))
                return carry

            def small(c, carry, copy=copy, n_big=n_big):
                action(copy(n_big * DISPATCH_CHUNK + c * SUBLANES, SUBLANES))
                return carry

            lax.fori_loop(0, n_big, big, 0)
            lax.fori_loop(0, n_small, small, 0)

    @pl.when(step == 0)
    def _():
        base_ref[...] = jnp.zeros_like(base_ref)

    @pl.when(step >= 2)
    def _():
        run_copies(slot, lambda cp: cp.wait())

    m = mod_ref[0]
    h = _rms_mod(x_ref[0], nw_ref[...], m[4:5], m[3:4])
    w = rw_ref[...]
    w_hi = w.astype(BF16)
    w_lo = (w - w_hi.astype(F32)).astype(BF16)
    h_hi = h.astype(BF16)
    h_lo = (h - h_hi.astype(F32)).astype(BF16)
    logits = _dot(h_hi, w_hi) + (_dot(h_lo, w_hi) + _dot(h_hi, w_lo))
    tm = logits.shape[0]
    lane = lax.broadcasted_iota(jnp.int32, logits.shape, 1)
    lg = jnp.where(lane < N_EXPERTS, logits, -jnp.inf)
    m1 = jnp.max(lg, axis=-1, keepdims=True)
    i1 = jnp.min(jnp.where(lg == m1, lane, LANES), axis=-1, keepdims=True)
    lg2 = jnp.where(lane == i1, -jnp.inf, lg)
    m2 = jnp.max(lg2, axis=-1, keepdims=True)
    i2 = jnp.min(jnp.where(lg2 == m2, lane, LANES), axis=-1, keepdims=True)
    e2 = jnp.exp(m2 - m1)
    den = 1.0 + e2
    hot = jnp.where(jnp.logical_or(lane == i1, lane == i2), 1.0, 0.0)
    r = lax.broadcasted_iota(jnp.int32, (tm, tm), 0)
    c = lax.broadcasted_iota(jnp.int32, (tm, tm), 1)
    before = jnp.where(r > c, 1.0, 0.0).astype(BF16)
    excl = _dot(before, hot.astype(BF16))
    base = base_ref[...]
    count = jnp.sum(hot, axis=0, keepdims=True)
    padded = jnp.floor((count + (SUBLANES - 1)) * (1.0 / SUBLANES)) * SUBLANES
    er = lax.broadcasted_iota(jnp.int32, (LANES, LANES), 0)
    ec = lax.broadcasted_iota(jnp.int32, (LANES, LANES), 1)
    lower = jnp.where(er < ec, 1.0, 0.0).astype(BF16)
    groups = jnp.broadcast_to(padded * (1.0 / SUBLANES), (SUBLANES, LANES)).astype(BF16)
    stage_off = _dot(groups, lower)[0:1] * SUBLANES
    pick = lambda idx, table: jnp.sum(jnp.where(lane == idx, table, 0.0), axis=-1, keepdims=True)
    tot = base + excl
    pos1 = i1.astype(F32) * cap + pick(i1, tot)
    pos2 = i2.astype(F32) * cap + pick(i2, tot)
    loc1 = pick(i1, stage_off + excl)
    loc2 = pick(i2, stage_off + excl)
    meta = (jnp.where(lane == 0, pos1, 0.0) + jnp.where(lane == 1, pos2, 0.0)
            + jnp.where(lane == 2, 1.0 / den, 0.0) + jnp.where(lane == 3, e2 / den, 0.0)
            + jnp.where(lane == 4, i1.astype(F32), 0.0) + jnp.where(lane == 5, i2.astype(F32), 0.0)
            + jnp.where(lane == 6, loc1, 0.0) + jnp.where(lane == 7, loc2, 0.0))
    meta_ref[0] = meta
    sub = lax.broadcasted_iota(jnp.int32, (SUBLANES, LANES), 0)
    tb = jnp.zeros((SUBLANES, LANES), F32)
    for j in range(tm // COMBINE_TILE):
        tb = jnp.where(sub == j, tot[j * COMBINE_TILE:j * COMBINE_TILE + 1, :], tb)
    tb_ref[0, 0] = tb
    base_ref[...] = base + padded
    cnt_ref[...] = jnp.broadcast_to(base + padded, cnt_ref.shape)

    meta_t = meta.T
    j = lax.broadcasted_iota(jnp.int32, (stage_ref.shape[1], tm), 0)
    sel = jnp.logical_or(j == meta_t[6:7].astype(jnp.int32), j == meta_t[7:8].astype(jnp.int32))
    stage_ref[slot] = _dot(jnp.where(sel, 1.0, 0.0).astype(BF16), h_hi)
    record = jnp.where(sub == 0, base, jnp.where(sub == 1, padded, jnp.where(sub == 2, stage_off, 0.0)))
    sc_vmem[...] = record.astype(jnp.int32)
    to_smem = pltpu.make_async_copy(sc_vmem, sc_smem.at[slot], ssem)
    to_smem.start()
    to_smem.wait()
    run_copies(slot, lambda cp: cp.start())

    @pl.when(step == n_steps - 1)
    def _():
        zero_ref[...] = jnp.zeros_like(zero_ref)

        def tail_copy(e):
            end = sc_smem[slot, 0, e] + sc_smem[slot, 1, e]
            return pltpu.make_async_copy(zero_ref, xs_ref.at[pl.ds(pl.multiple_of(e * cap + end, SUBLANES), tg)], zsem)

        for e in range(N_EXPERTS):
            tail_copy(e).start()
        for e in range(N_EXPERTS):
            tail_copy(e).wait()
        run_copies(slot, lambda cp: cp.wait())

        @pl.when(n_steps >= 2)
        def _():
            run_copies(1 - slot, lambda cp: cp.wait())


def _route(x, mod, nw, rw, tm, cap, tg):
    B, S, D = x.shape
    n_stage = 2 * tm + N_EXPERTS * SUBLANES
    return pl.pallas_call(
        functools.partial(_route_kernel, cap=cap, tg=tg),
        grid=(B, S // tm),
        in_specs=[pl.BlockSpec((1, tm, D), lambda b, i: (b, i, 0)),
                  pl.BlockSpec((1, 6, D), lambda b, i: (b, 0, 0)),
                  pl.BlockSpec((1, D), lambda b, i: (0, 0)),
                  pl.BlockSpec((D, LANES), lambda b, i: (0, 0))],
        out_specs=[pl.BlockSpec((1, tm, LANES), lambda b, i: (b, i, 0)),
                   pl.BlockSpec((SUBLANES, LANES), lambda b, i: (0, 0)),
                   pl.BlockSpec((1, 1, SUBLANES, LANES), lambda b, i: (b, i, 0, 0)),
                   pl.BlockSpec(memory_space=pl.ANY)],
        out_shape=[jax.ShapeDtypeStruct((B, S, LANES), F32),
                   jax.ShapeDtypeStruct((SUBLANES, LANES), F32),
                   jax.ShapeDtypeStruct((B, S // tm, SUBLANES, LANES), F32),
                   jax.ShapeDtypeStruct((N_EXPERTS * cap, D), F32)],
        scratch_shapes=[pltpu.VMEM((1, LANES), F32),
                        pltpu.VMEM((2, n_stage, D), F32),
                        pltpu.VMEM((tg, D), F32),
                        pltpu.VMEM((SUBLANES, LANES), jnp.int32),
                        pltpu.SMEM((2, SUBLANES, LANES), jnp.int32),
                        pltpu.SemaphoreType.DMA((2,)),
                        pltpu.SemaphoreType.DMA,
                        pltpu.SemaphoreType.DMA],
        compiler_params=_cparams(("arbitrary", "arbitrary")),
        name="moe_route",
    )(x, mod, nw, rw)


def _moe_group_kernel(blk_ref, exp_ref, fa_ref, fb_ref, nact_ref, x_ref, w1_ref, w3_ref, w2_ref, o_ref, acc_ref):
    t = pl.program_id(0)
    f = pl.program_id(1)

    @pl.when(jnp.logical_and(t == 0, f == 0))
    def _():
        acc_ref[...] = jnp.zeros_like(acc_ref)

    @pl.when(t < nact_ref[0])
    def _():
        h = x_ref[...].astype(BF16)
        a = _dot(h, w1_ref[0])
        b = _dot(h, w3_ref[0])
        y = _dot(((a * _sigmoid(a)) * b).astype(BF16), w2_ref[0])
        total = jnp.where(f == 0, 0.0, acc_ref[...]) + y
        acc_ref[...] = total
        o_ref[...] = total.astype(o_ref.dtype)


def _moe_group(blk, exp, fa, fb, nact, xs, w1, w3, w2, tg):
    R, D = xs.shape
    F = w1.shape[2]
    tf = F // 2
    fsel = lambda f, fa, fb, t: jnp.where(f == 0, fa[t], fb[t])
    return pl.pallas_call(
        _moe_group_kernel,
        grid_spec=pltpu.PrefetchScalarGridSpec(
            num_scalar_prefetch=5,
            grid=(blk.shape[0], 2),
            in_specs=[pl.BlockSpec((tg, D), lambda t, f, blk, exp, fa, fb, na: (blk[t], 0)),
                      pl.BlockSpec((1, D, tf), lambda t, f, blk, exp, fa, fb, na: (exp[t], 0, fsel(f, fa, fb, t))),
                      pl.BlockSpec((1, D, tf), lambda t, f, blk, exp, fa, fb, na: (exp[t], 0, fsel(f, fa, fb, t))),
                      pl.BlockSpec((1, tf, D), lambda t, f, blk, exp, fa, fb, na: (exp[t], fsel(f, fa, fb, t), 0))],
            out_specs=pl.BlockSpec((tg, D), lambda t, f, blk, exp, fa, fb, na: (blk[t], 0)),
            scratch_shapes=[pltpu.VMEM((tg, D), F32)]),
        out_shape=jax.ShapeDtypeStruct((R, D), BF16),
        compiler_params=_cparams(("arbitrary", "arbitrary")),
        name="moe_experts",
    )(blk, exp, fa, fb, nact, xs, w1, w3, w2)


def _combine_kernel(tb_ref, x_ref, mod_ref, meta_ref, fn_ref, y_ref, o_ref, buf_ref, sem, *, cap):
    i = pl.program_id(0)
    n = pl.num_programs(0)
    tc = x_ref.shape[0]
    n_rows = buf_ref.shape[1]
    shift = COMBINE_CHUNK.bit_length() - 1

    def segments(tile):
        segs, off = [], 0
        for e in range(N_EXPERTS):
            first = tb_ref[tile * N_EXPERTS + e]
            count = tb_ref[(tile + 1) * N_EXPERTS + e] - first
            lead = first & (COMBINE_CHUNK - 1)
            n_chunks = jnp.where(count > 0, lax.shift_right_logical(lead + count + COMBINE_CHUNK - 1, shift), 0)
            segs.append((e * cap + first - lead, n_chunks, off))
            off = off + n_chunks * COMBINE_CHUNK
        return segs

    def chunk_copy(src_row, dst_row, slot):
        return pltpu.make_async_copy(y_ref.at[pl.ds(pl.multiple_of(src_row, COMBINE_CHUNK), COMBINE_CHUNK)],
                                     buf_ref.at[slot, pl.ds(pl.multiple_of(dst_row, COMBINE_CHUNK), COMBINE_CHUNK)],
                                     sem.at[slot])

    def for_each_chunk(tile, slot, action):
        for src, n_chunks, off in segments(tile):
            def body(c, carry):
                action(chunk_copy(src + c * COMBINE_CHUNK, off + c * COMBINE_CHUNK, slot))
                return carry

            lax.fori_loop(0, n_chunks, body, 0)

    @pl.when(i == 0)
    def _():
        buf_ref[...] = jnp.zeros_like(buf_ref)
        for_each_chunk(0, 0, lambda cp: cp.start())

    slot = i % 2

    @pl.when(i + 1 < n)
    def _():
        for_each_chunk(i + 1, 1 - slot, lambda cp: cp.start())

    for_each_chunk(i, slot, lambda cp: cp.wait())
    meta = meta_ref[...]
    segs = segments(i)

    def one_hot(pos, exp):
        delta = jnp.zeros_like(pos)
        for e, (src, _, off) in enumerate(segs):
            delta = jnp.where(exp == e, off - src, delta)
        col = lax.broadcasted_iota(jnp.int32, (tc, n_rows), 1)
        return jnp.where(col == pos + delta, 1.0, 0.0).astype(BF16)

    as_int = lambda lane: meta[:, lane:lane + 1].astype(jnp.int32)
    picks = jnp.concatenate([one_hot(as_int(0), as_int(4)), one_hot(as_int(1), as_int(5))], axis=0)
    y12 = _dot(picks, buf_ref[slot])
    mix = meta[:, 2:3] * y12[0:tc] + meta[:, 3:4] * y12[tc:2 * tc]
    o_ref[...] = _rms(x_ref[...] + mod_ref[0][5:6] * mix, fn_ref[...])


def _combine(tile_base, x, mod, meta, fin, y, seq_len, cap):
    T, D = x.shape
    tc = COMBINE_TILE
    per_batch = seq_len // tc
    n_rows = -(-(2 * tc + N_EXPERTS * (2 * (COMBINE_CHUNK - 1) + SUBLANES - 1)) // LANES) * LANES
    return pl.pallas_call(
        functools.partial(_combine_kernel, cap=cap),
        grid_spec=pltpu.PrefetchScalarGridSpec(
            num_scalar_prefetch=1,
            grid=(T // tc,),
            in_specs=[pl.BlockSpec((tc, D), lambda i, tb: (i, 0)),
                      pl.BlockSpec((1, 6, D), lambda i, tb: (i // per_batch, 0, 0)),
                      pl.BlockSpec((tc, LANES), lambda i, tb: (i, 0)),
                      pl.BlockSpec((1, D), lambda i, tb: (0, 0)),
                      pl.BlockSpec(memory_space=pl.ANY)],
            out_specs=pl.BlockSpec((tc, D), lambda i, tb: (i, 0)),
            scratch_shapes=[pltpu.VMEM((2, n_rows, D), BF16), pltpu.SemaphoreType.DMA((2,))]),
        out_shape=jax.ShapeDtypeStruct((T, D), F32),
        compiler_params=_cparams(("arbitrary",)),
        name="moe_combine",
    )(tile_base, x, mod, meta, fin, y)


def _moe_tiles(cnt, n_tiles, tg, cap):
    per = (cnt + tg - 1) // tg
    cum = jnp.cumsum(per)
    nact = cum[-1]
    t = jnp.arange(n_tiles, dtype=jnp.int32)
    tt = jnp.minimum(t, nact - 1)
    exp = jnp.minimum(jnp.sum((tt[:, None] >= cum[None, :]).astype(jnp.int32), axis=1), N_EXPERTS - 1)
    blk = exp * (cap // tg) + tt - (cum - per)[exp]
    odd = tt % 2
    fa = jnp.where(t < nact, odd, 1 - odd)
    fb = 1 - odd
    i32 = lambda v: v.astype(jnp.int32)
    return i32(blk), i32(exp), i32(fa), i32(fb), i32(nact.reshape(1))


def _ffn_kernel(*refs, n_cast):
    x_ref, mod_ref, nw_ref, w1_ref, w3_ref, w2_ref = refs[0:6]
    cast_in = refs[6:6 + n_cast]
    o_ref = refs[6 + n_cast]
    cast_out = refs[7 + n_cast:7 + 2 * n_cast]
    (acc_ref,) = refs[7 + 2 * n_cast:]
    f = pl.program_id(2)

    @pl.when(jnp.logical_and(jnp.logical_and(pl.program_id(0) == 0, pl.program_id(1) == 0), f == 0))
    def _():
        acc_ref[...] = jnp.zeros_like(acc_ref)

    for src, dst in zip(cast_in, cast_out):
        dst[...] = src[...].astype(dst.dtype)
    m = mod_ref[0]
    x = x_ref[0]
    h = _rms_mod(x, nw_ref[...], m[4:5], m[3:4]).astype(BF16)
    a = _dot(h, w1_ref[...])
    b = _dot(h, w3_ref[...])
    total = jnp.where(f == 0, 0.0, acc_ref[...]) + _dot(((a * _sigmoid(a)) * b).astype(BF16), w2_ref[...])
    acc_ref[...] = total
    o_ref[0] = x + m[5:6] * total


def _ffn(x, mod, mod_per_batch, nw, w1, w3, w2, tm, tf, cast=()):
    B, S, D = x.shape
    F = w1.shape[1]
    n_i, n_f = S // tm, F // tf
    n_steps = B * n_i * n_f
    mod_map = (lambda b, i, f: (b, 0, 0)) if mod_per_batch else (lambda b, i, f: (0, 0, 0))
    cast_specs, cast_shapes = [], []
    for arr in cast:
        E, R, C = arr.shape
        per = n_steps // E
        assert per * E == n_steps and R % (per * 2 * SUBLANES) == 0, (arr.shape, n_steps)

        def cast_map(b, i, f, per=per):
            step = (b * n_i + i) * n_f + f
            return step // per, step % per, 0

        cast_specs.append(pl.BlockSpec((1, R // per, C), cast_map))
        cast_shapes.append(jax.ShapeDtypeStruct(arr.shape, BF16))
    tok = pl.BlockSpec((1, tm, D), lambda b, i, f: (b, i, 0))
    outs = pl.pallas_call(
        functools.partial(_ffn_kernel, n_cast=len(cast)),
        grid=(B, n_i, n_f),
        in_specs=[tok,
                  pl.BlockSpec((1, 6, D), mod_map),
                  pl.BlockSpec((1, D), lambda b, i, f: (0, 0)),
                  pl.BlockSpec((D, tf), lambda b, i, f: (0, f)),
                  pl.BlockSpec((D, tf), lambda b, i, f: (0, f)),
                  pl.BlockSpec((tf, D), lambda b, i, f: (f, 0))] + cast_specs,
        out_specs=[tok] + cast_specs,
        out_shape=[jax.ShapeDtypeStruct((B, S, D), F32)] + cast_shapes,
        scratch_shapes=[pltpu.VMEM((tm, D), F32)],
        compiler_params=_cparams(("parallel", "parallel", "arbitrary")),
        name="ffn",
    )(x, mod, nw, w1, w3, w2, *cast)
    return outs[0], tuple(outs[1:])


def _proj_odd_kernel(x_ref, mod_ref, nw_ref, w_ref, gate_ref, rec_ref):
    m = mod_ref[0]
    h = _rms_mod(x_ref[0], nw_ref[...], m[1:2], m[0:1]).astype(BF16)
    gate_ref[0] = _gelu(_dot(h, w_ref[:, 0:D_RNN])).astype(gate_ref.dtype)
    rec_ref[0] = _dot(h, w_ref[:, D_RNN:2 * D_RNN])


def _proj_odd(x, mod, mod_per_batch, nw, w, tm):
    B, S, D = x.shape
    mod_map = (lambda b, i: (b, 0, 0)) if mod_per_batch else (lambda b, i: (0, 0, 0))
    tok = lambda width: pl.BlockSpec((1, tm, width), lambda b, i: (b, i, 0))
    return pl.pallas_call(
        _proj_odd_kernel,
        grid=(B, S // tm),
        in_specs=[tok(D),
                  pl.BlockSpec((1, 6, D), mod_map),
                  pl.BlockSpec((1, D), lambda b, i: (0, 0)),
                  pl.BlockSpec((D, 2 * D_RNN), lambda b, i: (0, 0))],
        out_specs=[tok(D_RNN), tok(D_RNN)],
        out_shape=[jax.ShapeDtypeStruct((B, S, D_RNN), BF16), jax.ShapeDtypeStruct((B, S, D_RNN), F32)],
        compiler_params=_cparams(("parallel", "parallel")),
        name="proj_odd",
    )(x, mod, nw, w)


def _scan8(a_ref, b_ref, d, r0, h, row, reverse):
    base = pl.multiple_of(SUBLANES + r0, SUBLANES)
    a = a_ref[d, pl.ds(base, SUBLANES), :]
    b = b_ref[d, pl.ds(base, SUBLANES), :]
    for s in (1, 2, 4):
        live = row < SUBLANES - s if reverse else row >= s
        if s == 1:
            shifted = base + 1 if reverse else base - 1
            a_s = a_ref[d, pl.ds(shifted, SUBLANES), :]
            b_s = b_ref[d, pl.ds(shifted, SUBLANES), :]
        else:
            shift = SUBLANES - s if reverse else s
            a_s, b_s = pltpu.roll(a, shift, 0), pltpu.roll(b, shift, 0)
        b = jnp.where(live, a * b_s + b, b)
        a = jnp.where(live, a * a_s, a)
    hr = a * h + b
    return hr, (hr[0:1] if reverse else hr[SUBLANES - 1:SUBLANES])


def _lru_kernel(rec_ref, recc_ref, cw_ref, cb_ref, wa_ref, ba_ref, wx_ref, bx_ref, lam_ref,
                s_ref, pad_ref, a_ref, b_ref, cpad_ref, ca_ref, cbb_ref, park_ref, *, tile):
    S = rec_ref.shape[1]
    L = recc_ref.shape[1]
    cw = cw_ref[...]
    cb = cb_ref[...]
    lam = lam_ref[...]
    sp = jnp.maximum(-lam, 0.0) + jnp.log1p(jnp.exp(-jnp.abs(lam)))
    z_half = (-0.5 * LRU_C) * sp
    z_half_log2e = z_half * LOG2E
    neg_z_quarter = -0.5 * z_half
    half_ba = 0.5 * ba_ref[...]
    half_bx = 0.5 * bx_ref[...]
    zeros8 = jnp.zeros((SUBLANES, LANES), F32)

    def coefficients(src_ref, dst_a, dst_b, n_rows, t):
        pad = cpad_ref if src_ref is recc_ref else pad_ref
        pad[0:SUBLANES, :] = zeros8
        pad[SUBLANES + n_rows:2 * SUBLANES + n_rows, :] = zeros8
        for d in range(2):
            for dst in (dst_a, dst_b):
                dst[d, 0:SUBLANES, :] = zeros8
                dst[d, SUBLANES + n_rows:2 * SUBLANES + n_rows, :] = zeros8

        def copy(j, carry):
            r0 = pl.multiple_of(j * t, t)
            pad[pl.ds(SUBLANES + r0, t), :] = src_ref[0, pl.ds(r0, t), :]
            return carry

        lax.fori_loop(0, n_rows // t, copy, 0)

        def body(j, carry):
            r0 = pl.multiple_of(j * t, t)
            conv = cb
            for tap in range(4):
                conv = conv + cw[tap:tap + 1] * pad[pl.ds(r0 + (SUBLANES - 2 + tap), t), :]
            cbf = conv.astype(BF16)
            half_conv = 0.5 * conv
            for d in range(2):
                tr = jnp.tanh(_dot(cbf, wa_ref[d, 0]) + half_ba[d:d + 1])
                ti = jnp.tanh(_dot(cbf, wx_ref[d, 0]) + half_bx[d:d + 1])
                a = jnp.exp2(z_half_log2e[d:d + 1] * tr + z_half_log2e[d:d + 1])
                y = jnp.tanh(neg_z_quarter[d:d + 1] * tr + neg_z_quarter[d:d + 1])
                root = jnp.where(y > 0.0, y * lax.rsqrt(y), 0.0)
                dst_a[d, pl.ds(SUBLANES + r0, t), :] = a
                dst_b[d, pl.ds(SUBLANES + r0, t), :] = (root * (1.0 + a)) * ((ti + 1.0) * half_conv)
            return carry

        lax.fori_loop(0, n_rows // t, body, 0)

    row = lax.broadcasted_iota(jnp.int32, (SUBLANES, LANES), 0)
    h_zero = jnp.zeros((1, LANES), F32)

    coefficients(recc_ref, ca_ref, cbb_ref, L, L)
    nc = L // SUBLANES

    def ctx_body(j, carry):
        hf, hb = carry
        rf = pl.multiple_of(j * SUBLANES, SUBLANES)
        rb = pl.multiple_of((nc - 1 - j) * SUBLANES, SUBLANES)
        _, hf = _scan8(ca_ref, cbb_ref, 0, rf, hf, row, False)
        _, hb = _scan8(ca_ref, cbb_ref, 1, rb, hb, row, True)
        return hf, hb

    h0f, h0b = lax.fori_loop(0, nc, ctx_body, (h_zero, h_zero))

    coefficients(rec_ref, a_ref, b_ref, S, tile)
    n = S // SUBLANES

    def lat_body(second_half):
        def body(j, carry):
            hf, hb = carry
            rf = pl.multiple_of(j * SUBLANES, SUBLANES)
            rb = pl.multiple_of((n - 1 - j) * SUBLANES, SUBLANES)
            of, hf = _scan8(a_ref, b_ref, 0, rf, hf, row, False)
            ob, hb = _scan8(a_ref, b_ref, 1, rb, hb, row, True)
            if second_half:
                park_ref[0, pl.ds(rf - half, SUBLANES), :] = of
                park_ref[1, pl.ds(rb, SUBLANES), :] = ob
            else:
                s_ref[0, pl.ds(rf, SUBLANES), :] = of
                s_ref[0, pl.ds(rb, SUBLANES), :] = ob
            return hf, hb
        return body

    half = S // 2
    mid = lax.fori_loop(0, n // 2, lat_body(False), (h0f, h0b), unroll=8)
    lax.fori_loop(n // 2, n, lat_body(True), mid, unroll=8)

    def add_parked(j, carry):
        r0 = pl.multiple_of(j * tile, tile)
        s_ref[0, pl.ds(half + r0, tile), :] += park_ref[0, pl.ds(r0, tile), :]
        s_ref[0, pl.ds(r0, tile), :] += park_ref[1, pl.ds(r0, tile), :]
        return carry

    lax.fori_loop(0, half // tile, add_parked, 0)


def _lru(rec, rec_c, conv_w, conv_b, wa, ba, wx, bx, lam, tile):
    B, S, _ = rec.shape
    L = rec_c.shape[1]
    blk = lambda rows: pl.BlockSpec((1, rows, LRU_BLOCK), lambda b, j: (b, 0, j))
    vec = lambda rows: pl.BlockSpec((rows, LRU_BLOCK), lambda b, j: (0, j))
    wspec = pl.BlockSpec((2, 1, LRU_BLOCK, LRU_BLOCK), lambda b, j: (0, j, 0, 0))
    return pl.pallas_call(
        functools.partial(_lru_kernel, tile=tile),
        grid=(B, LRU_BLOCKS),
        in_specs=[blk(S), blk(L), vec(4), vec(1), wspec, vec(2), wspec, vec(2), vec(2)],
        out_specs=blk(S),
        out_shape=jax.ShapeDtypeStruct((B, S, D_RNN), F32),
        scratch_shapes=[pltpu.VMEM((S + 2 * SUBLANES, LRU_BLOCK), F32),
                        pltpu.VMEM((2, S + 2 * SUBLANES, LRU_BLOCK), F32),
                        pltpu.VMEM((2, S + 2 * SUBLANES, LRU_BLOCK), F32),
                        pltpu.VMEM((L + 2 * SUBLANES, LRU_BLOCK), F32),
                        pltpu.VMEM((2, L + 2 * SUBLANES, LRU_BLOCK), F32),
                        pltpu.VMEM((2, L + 2 * SUBLANES, LRU_BLOCK), F32),
                        pltpu.VMEM((2, S // 2, LRU_BLOCK), F32)],
        compiler_params=_cparams(("parallel", "parallel")),
        name="lru_scan",
    )(rec, rec_c, conv_w, conv_b, wa, ba, wx, bx, lam)


def _lru_out_kernel(x_ref, mod_ref, gate_ref, s_ref, w_ref, o_ref):
    y = (gate_ref[0].astype(F32) * s_ref[0]).astype(BF16)
    o_ref[0] = x_ref[0] + mod_ref[0][2:3] * _dot(y, w_ref[...])


def _lru_out(x, mod, gate, s, w, tm):
    B, S, D = x.shape
    tok = lambda width: pl.BlockSpec((1, tm, width), lambda b, i: (b, i, 0))
    return pl.pallas_call(
        _lru_out_kernel,
        grid=(B, S // tm),
        in_specs=[tok(D), pl.BlockSpec((1, 6, D), lambda b, i: (b, 0, 0)), tok(D_RNN), tok(D_RNN),
                  pl.BlockSpec((D_RNN, D), lambda b, i: (0, 0))],
        out_specs=tok(D),
        out_shape=jax.ShapeDtypeStruct((B, S, D), F32),
        compiler_params=_cparams(("parallel", "parallel")),
        name="lru_out",
    )(x, mod, gate, s, w)


def _rope_tables(n_tok):
    rows = n_tok // GRID_W
    row = jnp.repeat(jnp.arange(rows, dtype=F32), GRID_W)
    col = jnp.tile(jnp.arange(GRID_W, dtype=F32), rows)
    freqs = ROPE_BASE ** (-jnp.arange(ROPE_FREQS, dtype=F32) / ROPE_FREQS)
    ar, ac = row[:, None] * freqs, col[:, None] * freqs
    cos = jnp.concatenate([jnp.cos(ar), jnp.cos(ar), jnp.cos(ac), jnp.cos(ac)], axis=-1)
    sin = jnp.concatenate([-jnp.sin(ar), jnp.sin(ar), -jnp.sin(ac), jnp.sin(ac)], axis=-1)
    return jnp.tile(cos, (1, LANES // HEAD_DIM)), jnp.tile(sin, (1, LANES // HEAD_DIM))


def kernel(x, c, ctx, c_ctx, ada_w_e, ada_b_e, norm1_e, norm2_e, w_in_e, sgu_w, sgu_b, attn_sink, w_out_e, ffn_w1, ffn_w3, ffn_w2, ada_w_o, ada_b_o, norm1_o, norm2_o, w_in_o, conv_w, conv_b, lru_wa, lru_ba, lru_wx, lru_bx, lru_lambda, w_out_o, router_w, moe_w1, moe_w3, moe_w2, final_norm):
    B, S, D = x.shape
    L = ctx.shape[1]
    cvec = jnp.concatenate([c, c_ctx[None], jnp.zeros((SUBLANES - B - 1, D), F32)], axis=0)
    mod_e = _ada_params(cvec, ada_w_e[0], ada_b_e[0])
    mod_o = _ada_params(cvec, ada_w_o[0], ada_b_o[0])
    lat_e, ctx_e = mod_e[0:B], mod_e[B:B + 1]
    lat_o, ctx_o = mod_o[0:B], mod_o[B:B + 1]
    bf = lambda t: t.astype(BF16)
    row = lambda t: t.reshape(1, -1)

    cos, sin = _rope_tables(S)
    cos_c, sin_c = jnp.ones((L, LANES), F32), jnp.zeros((L, LANES), F32)
    w_in = bf(w_in_e[0])
    n1, n2 = row(norm1_e[0]), row(norm2_e[0])
    uc, vc, qc, kc2, vc2 = _proj_even(ctx, ctx_e, False, n1, w_in, cos_c, sin_c, L)
    u, v, q, k2, v2 = _proj_even(x, lat_e, True, n1, w_in, cos, sin, 1024)
    ws, bs_t, wout = bf(sgu_w[0]), sgu_b[0].T, bf(w_out_e[0])
    sink = attn_sink[0] * LOG2E
    x = _mixer_even(x, lat_e, True, u, v, q, k2, v2, kc2, vc2, ws, bs_t, sink, wout, 512, False)
    xc = _mixer_even(ctx, ctx_e, False, uc, vc, qc, kc2, vc2, kc2, vc2, ws, bs_t, sink, wout, L, True)
    w1, w3, w2 = bf(ffn_w1[0]), bf(ffn_w3[0]), bf(ffn_w2[0])
    x, moe_w = _ffn(x, lat_e, True, n2, w1, w3, w2, 512, 1408, cast=(moe_w1[0], moe_w3[0], moe_w2[0]))
    xc, _ = _ffn(xc, ctx_e, False, n2, w1, w3, w2, L, 1408)

    w_in = bf(w_in_o[0])
    n1, n2 = row(norm1_o[0]), row(norm2_o[0])
    _, rec_c = _proj_odd(xc, ctx_o, False, n1, w_in, L)
    gate, rec = _proj_odd(x, lat_o, True, n1, w_in, 1024)
    s = _lru(rec, rec_c, conv_w[0], row(conv_b[0]), bf(0.5 * lru_wa[0]), lru_ba[0], bf(0.5 * lru_wx[0]), lru_bx[0],
             lru_lambda[0], 512)
    x = _lru_out(x, lat_o, gate, s, bf(w_out_o[0]), 1024)
    rw = jnp.pad(router_w[0], ((0, 0), (0, LANES - N_EXPERTS)))
    T = B * S
    tg = 512
    tm = 512
    run_pad = (T // tm) * (SUBLANES - 1)
    cap = -(-(T + run_pad) // tg) * tg + 2 * tg
    meta, cnt, tb4, xs = _route(x, lat_o, n2, rw, tm, cap, tg)
    cnt = cnt[0, 0:N_EXPERTS].astype(jnp.int32)
    tile_base = jnp.concatenate([tb4[:, :, 0:tm // COMBINE_TILE, 0:N_EXPERTS].astype(jnp.int32).reshape(-1), cnt])
    n_tiles = -(-(2 * T + N_EXPERTS * run_pad) // tg) + N_EXPERTS
    blk, exp, fa, fb, nact = _moe_tiles(cnt, n_tiles, tg, cap)
    y = _moe_group(blk, exp, fa, fb, nact, xs, *moe_w, tg)
    out = _combine(tile_base, x.reshape(T, D), lat_o, meta.reshape(T, LANES), row(final_norm), y, S, cap)
    return out.reshape(B, S, D)
```

```python
import functools

import jax
import jax.numpy as jnp
from jax import lax
from jax.experimental import pallas as pl
from jax.experimental.pallas import tpu as pltpu

F32 = jnp.float32
BF16 = jnp.bfloat16

D_MODEL = 1024
GRID_W = 64
EPS = 1e-6
NEG_INF = -1e30
CHUNK = 128
SGU_GROUPS = 4
SGU_WIDTH = 512
HEAD_DIM = 64
N_Q_HEADS = 8
N_KV_HEADS = 2
ATTN_WIDTH = 512
KV_WIDTH = 128
WINDOW = 128
ATTN_BLOCK = 128
ATTN_SCALE = HEAD_DIM ** -0.5
LOG2E = 1.4426950408889634
ROW_BLOCK = 32
COMBINE_TILE = 256
COMBINE_CHUNK = 16
DISPATCH_CHUNK = 64
ROPE_BASE = 10000.0
ROPE_FREQS = 16
IN_EVEN = 1792
D_RNN = 1280
LRU_BLOCKS = 10
LRU_BLOCK = 128
LRU_C = 8.0
D_FF = 2816
N_EXPERTS = 8
LANES = 128
SUBLANES = 8
VMEM_LIMIT = 56 * 1024 * 1024


def _cparams(sem):
    return pltpu.CompilerParams(dimension_semantics=sem, vmem_limit_bytes=VMEM_LIMIT)


def _dot(a, b):
    return jnp.dot(a, b, preferred_element_type=F32)


def _dot_nt(a, b):
    return lax.dot_general(a, b, (((1,), (1,)), ((), ())), preferred_element_type=F32)


def _gelu(x):
    return 0.5 * x * (1.0 + jnp.tanh(0.7978845608028654 * (x + 0.044715 * (x * x * x))))


def _sigmoid(x):
    return 0.5 * jnp.tanh(0.5 * x) + 0.5


def _rms(x, nw):
    return (x * lax.rsqrt(jnp.mean(x * x, axis=-1, keepdims=True) + EPS)) * nw


def _rms_mod(x, nw, scale, shift):
    return _rms(x, nw) * (1.0 + scale) + shift


def _ada_kernel(c_ref, w_ref, b_ref, o_ref):
    c = c_ref[...]
    act = c * _sigmoid(c)
    o_ref[...] = _dot(act.astype(BF16), w_ref[...].astype(BF16)) + b_ref[...]


def _ada_params(cvec, w, b):
    n = w.shape[1]
    tn = 1536
    out = pl.pallas_call(
        _ada_kernel,
        grid=(n // tn,),
        in_specs=[pl.BlockSpec((SUBLANES, D_MODEL), lambda j: (0, 0)),
                  pl.BlockSpec((D_MODEL, tn), lambda j: (0, j)),
                  pl.BlockSpec((1, tn), lambda j: (0, j))],
        out_specs=pl.BlockSpec((SUBLANES, tn), lambda j: (0, j)),
        out_shape=jax.ShapeDtypeStruct((SUBLANES, n), F32),
        compiler_params=_cparams(("parallel",)),
        name="ada_params",
    )(cvec, w, b.reshape(1, n))
    return out.reshape(SUBLANES, 6, D_MODEL)


def _cast_specs(cast, grid):
    n_steps = 1
    for g in grid:
        n_steps *= g
    specs, shapes = [], []
    for arr in cast:
        E, R, C = arr.shape
        per = n_steps // E
        assert per * E == n_steps and R % (per * 2 * SUBLANES) == 0, (arr.shape, n_steps)

        def cast_map(*idx, per=per):
            step = 0
            for g, i in zip(grid, idx):
                step = step * g + i
            return step // per, step % per, 0

        specs.append(pl.BlockSpec((1, R // per, C), cast_map))
        shapes.append(jax.ShapeDtypeStruct(arr.shape, BF16))
    return specs, shapes


def _proj_even_kernel(x_ref, mod_ref, nw_ref, w_ref, cos_ref, sin_ref, *rest, n_cast):
    cast_in = rest[0:n_cast]
    u_ref, v_ref, q_ref, k_ref, val_ref = rest[n_cast:n_cast + 5]
    cast_out = rest[n_cast + 5:]
    for src, dst in zip(cast_in, cast_out):
        dst[...] = src[...].astype(dst.dtype)
    m = mod_ref[0]
    h = _rms_mod(x_ref[0], nw_ref[...], m[1:2], m[0:1]).astype(BF16)
    u_ref[0] = _gelu(_dot(h, w_ref[:, 0:SGU_WIDTH])).astype(u_ref.dtype)
    v_ref[0] = _gelu(_dot(h, w_ref[:, SGU_WIDTH:2 * SGU_WIDTH])).astype(v_ref.dtype)
    cos = cos_ref[...]
    sin = sin_ref[...]
    lane = lax.broadcasted_iota(jnp.int32, cos.shape, 1)
    first_half = (lane % 32) < ROPE_FREQS

    def rope(t):
        partner = jnp.where(first_half, pltpu.roll(t, LANES - ROPE_FREQS, 1), pltpu.roll(t, ROPE_FREQS, 1))
        return t * cos + partner * sin

    q = _dot(h, w_ref[:, 2 * SGU_WIDTH:2 * SGU_WIDTH + ATTN_WIDTH]) * (ATTN_SCALE * LOG2E)
    for g in range(ATTN_WIDTH // LANES):
        q_ref[0, :, g * LANES:(g + 1) * LANES] = rope(q[:, g * LANES:(g + 1) * LANES]).astype(BF16)
    kv = _dot(h, w_ref[:, 2 * SGU_WIDTH + ATTN_WIDTH:IN_EVEN])
    k = rope(kv[:, 0:KV_WIDTH])
    val = kv[:, KV_WIDTH:2 * KV_WIDTH]
    k_ref[0, :, 0:LANES] = k.astype(BF16)
    k_ref[0, :, LANES:2 * LANES] = pltpu.roll(k, HEAD_DIM, 1).astype(BF16)
    val_ref[0, :, 0:LANES] = val.astype(BF16)
    val_ref[0, :, LANES:2 * LANES] = pltpu.roll(val, HEAD_DIM, 1).astype(BF16)


def _proj_even(x, mod, mod_per_batch, nw, w, cos, sin, tm, cast=()):
    B, S, D = x.shape
    mod_map = (lambda b, i: (b, 0, 0)) if mod_per_batch else (lambda b, i: (0, 0, 0))
    tok = lambda width: pl.BlockSpec((1, tm, width), lambda b, i: (b, i, 0))
    cast_specs, cast_shapes = _cast_specs(cast, (B, S // tm))
    outs = pl.pallas_call(
        functools.partial(_proj_even_kernel, n_cast=len(cast)),
        grid=(B, S // tm),
        in_specs=[tok(D),
                  pl.BlockSpec((1, 6, D), mod_map),
                  pl.BlockSpec((1, D), lambda b, i: (0, 0)),
                  pl.BlockSpec((D, IN_EVEN), lambda b, i: (0, 0)),
                  pl.BlockSpec((tm, LANES), lambda b, i: (i, 0)),
                  pl.BlockSpec((tm, LANES), lambda b, i: (i, 0))] + cast_specs,
        out_specs=[tok(SGU_WIDTH), tok(SGU_WIDTH), tok(ATTN_WIDTH), tok(2 * KV_WIDTH), tok(2 * KV_WIDTH)] + cast_specs,
        out_shape=[jax.ShapeDtypeStruct((B, S, SGU_WIDTH), BF16),
                   jax.ShapeDtypeStruct((B, S, SGU_WIDTH), BF16),
                   jax.ShapeDtypeStruct((B, S, ATTN_WIDTH), BF16),
                   jax.ShapeDtypeStruct((B, S, 2 * KV_WIDTH), BF16),
                   jax.ShapeDtypeStruct((B, S, 2 * KV_WIDTH), BF16)] + cast_shapes,
        compiler_params=_cparams(("parallel", "parallel")),
        name="proj_even",
    )(x, mod, nw, w, cos, sin, *cast)
    return tuple(outs[0:5]), tuple(outs[5:])


def _mixer_even_kernel(sink_ref, x_ref, mod_ref, u_ref, v_ref, q_ref, k_ref, val_ref, kc_ref, vc_ref,
                       ws_ref, bs_ref, wout_ref, bias_ref, o_ref, mix_ref, s_ref, p_ref, inv_ref,
                       *, seq_len, is_ctx):
    tq = x_ref.shape[1]
    n_chunks = tq // CHUNK
    i = pl.program_id(1)
    nk = kc_ref.shape[1] + (0 if is_ctx else 3 * ATTN_BLOCK)
    lane = lax.broadcasted_iota(jnp.int32, (1, LANES), 1)
    lo = lane < HEAD_DIM
    zero = jnp.zeros((), BF16)

    def halves(ref_slice, kh):
        nat, swp = ref_slice[:, 0:LANES], ref_slice[:, LANES:2 * LANES]
        if kh == 0:
            return jnp.where(lo, nat, zero), jnp.where(lo, zero, swp)
        return jnp.where(lo, swp, zero), jnp.where(lo, zero, nat)

    kc_all = kc_ref[0]
    vc_all = vc_ref[0]

    def chunk_body(c, carry):
        r0 = pl.multiple_of(c * CHUNK, CHUNK)
        rows = pl.ds(r0, CHUNK)
        vch = v_ref[0, rows, :].astype(F32)
        uch = u_ref[0, rows, :].astype(F32)
        for g in range(SGU_GROUPS):
            cols = slice(g * LANES, (g + 1) * LANES)
            vg = vch[:, cols]
            dev = vg - jnp.mean(vg, axis=-1, keepdims=True)
            vn = dev * lax.rsqrt(jnp.mean(dev * dev, axis=-1, keepdims=True) + EPS)
            mixed = _dot(ws_ref[g], vn.astype(BF16)) + bs_ref[:, g:g + 1]
            mix_ref[rows, cols] = (uch[:, cols] * mixed).astype(BF16)
        qch = q_ref[0, rows, :]
        if not is_ctx:
            blk = i * n_chunks + c
            n_blk = seq_len // ATTN_BLOCK
            start = pl.multiple_of(jnp.clip((blk - 1) * ATTN_BLOCK, 0, seq_len - 3 * ATTN_BLOCK), ATTN_BLOCK)
            k3 = k_ref[0, pl.ds(start, 3 * ATTN_BLOCK), :]
            v3 = val_ref[0, pl.ds(start, 3 * ATTN_BLOCK), :]
            case = jnp.where(blk == 0, 0, jnp.where(blk == n_blk - 1, 2, 1))
        for kh in range(N_KV_HEADS):
            kc_lo, kc_hi = halves(kc_all, kh)
            vc_lo, vc_hi = halves(vc_all, kh)
            if is_ctx:
                k_cat = jnp.concatenate([kc_lo, kc_hi], axis=0)
                v_cat = (vc_lo, vc_hi)
            else:
                k_lo, k_hi = halves(k3, kh)
                v_lo, v_hi = halves(v3, kh)
                k_cat = jnp.concatenate([k_lo, kc_lo, k_hi, kc_hi], axis=0)
                v_cat = (jnp.concatenate([v_lo, vc_lo], axis=0), jnp.concatenate([v_hi, vc_hi], axis=0))
            q2 = jnp.concatenate([qch[:, 2 * kh * LANES:(2 * kh + 1) * LANES],
                                  qch[:, (2 * kh + 1) * LANES:(2 * kh + 2) * LANES]], axis=0)
            s_ref[:, 0:2 * nk] = _dot_nt(q2, k_cat)
            for half in range(2):
                for rb in range(2 * ATTN_BLOCK // ROW_BLOCK):
                    rsl = slice(rb * ROW_BLOCK, (rb + 1) * ROW_BLOCK)
                    snk = sink_ref[2 * (2 * kh + rb * ROW_BLOCK // ATTN_BLOCK) + half]
                    s = s_ref[rsl, half * nk:(half + 1) * nk]
                    if not is_ctx:
                        qoff = (rb * ROW_BLOCK) % ATTN_BLOCK
                        s_loc = s[:, 0:3 * ATTN_BLOCK] + bias_ref[case, qoff:qoff + ROW_BLOCK, :]
                        s = jnp.concatenate([s_loc, s[:, 3 * ATTN_BLOCK:]], axis=1)
                    m = jnp.maximum(jnp.max(s, axis=-1, keepdims=True), snk)
                    p = jnp.exp2(s - m)
                    den = jnp.sum(p, axis=-1, keepdims=True) + jnp.exp2(snk - m)
                    p_ref[rsl, half * nk:(half + 1) * nk] = p.astype(BF16)
                    inv_ref[rsl, half * LANES:(half + 1) * LANES] = jnp.broadcast_to(1.0 / den, (ROW_BLOCK, LANES))
            o_lo = _dot(p_ref[:, 0:nk], v_cat[0])
            o_hi = _dot(p_ref[:, nk:2 * nk], v_cat[1])
            acc = (o_lo * inv_ref[:, 0:LANES] + o_hi * inv_ref[:, LANES:2 * LANES]).astype(BF16)
            for g in range(2):
                col = SGU_WIDTH + (2 * kh + g) * LANES
                mix_ref[rows, col:col + LANES] = acc[g * ATTN_BLOCK:(g + 1) * ATTN_BLOCK]
        return carry

    lax.fori_loop(0, n_chunks, chunk_body, 0)
    y = _dot(mix_ref[...], wout_ref[...])
    o_ref[0] = x_ref[0] + mod_ref[0][2:3] * y


def _window_bias():
    case = jnp.arange(3, dtype=jnp.int32)[:, None, None]
    qi = jnp.arange(ATTN_BLOCK, dtype=jnp.int32)[None, :, None]
    kj = jnp.arange(3 * ATTN_BLOCK, dtype=jnp.int32)[None, None, :]
    return jnp.where(jnp.abs(kj - case * ATTN_BLOCK - qi) <= WINDOW, 0.0, NEG_INF).astype(F32)


def _mixer_even(x, mod, mod_per_batch, u, v, q, k2, v2, kc2, vc2, ws, bs_t, sink, wout, tq, is_ctx):
    B, S, D = x.shape
    Sk = k2.shape[1]
    Lc = kc2.shape[1]
    nk = Lc + (0 if is_ctx else 3 * ATTN_BLOCK)
    mod_map = (lambda b, i: (b, 0, 0)) if mod_per_batch else (lambda b, i: (0, 0, 0))
    tok = lambda width: pl.BlockSpec((1, tq, width), lambda b, i: (b, i, 0))
    per_batch = lambda rows: pl.BlockSpec((1, rows, 2 * KV_WIDTH), lambda b, i: (b, 0, 0))
    return pl.pallas_call(
        functools.partial(_mixer_even_kernel, seq_len=S, is_ctx=is_ctx),
        grid=(B, S // tq),
        in_specs=[pl.BlockSpec(memory_space=pltpu.SMEM),
                  tok(D),
                  pl.BlockSpec((1, 6, D), mod_map),
                  tok(SGU_WIDTH), tok(SGU_WIDTH), tok(ATTN_WIDTH),
                  per_batch(Sk), per_batch(Sk), per_batch(Lc), per_batch(Lc),
                  pl.BlockSpec((SGU_GROUPS, CHUNK, CHUNK), lambda b, i: (0, 0, 0)),
                  pl.BlockSpec((CHUNK, SGU_GROUPS), lambda b, i: (0, 0)),
                  pl.BlockSpec((D, D), lambda b, i: (0, 0)),
                  pl.BlockSpec((3, ATTN_BLOCK, 3 * ATTN_BLOCK), lambda b, i: (0, 0, 0))],
        out_specs=tok(D),
        out_shape=jax.ShapeDtypeStruct((B, S, D), F32),
        scratch_shapes=[pltpu.VMEM((tq, D), BF16),
                        pltpu.VMEM((2 * ATTN_BLOCK, 2 * nk), F32),
                        pltpu.VMEM((2 * ATTN_BLOCK, 2 * nk), BF16),
                        pltpu.VMEM((2 * ATTN_BLOCK, 2 * LANES), F32)],
        compiler_params=_cparams(("parallel", "arbitrary")),
        name="mixer_ctx" if is_ctx else "mixer_even",
    )(sink, x, mod, u, v, q, k2, v2, kc2, vc2, ws, bs_t, wout, _window_bias())


def _route_kernel(x_ref, mod_ref, nw_ref, rw_ref, meta_ref, cnt_ref, tb_ref, xs_ref,
                  base_ref, stage_ref, zero_ref, sc_vmem, sc_smem, sem, zsem, ssem, *, cap, tg):
    step = pl.program_id(0) * pl.num_programs(1) + pl.program_id(1)
    n_steps = pl.num_programs(0) * pl.num_programs(1)
    slot = step % 2

    def run_copies(slot_, action):
        for e in range(N_EXPERTS):
            first = e * cap + sc_smem[slot_, 0, e]
            count = sc_smem[slot_, 1, e]
            off = sc_smem[slot_, 2, e]
            n_big = count >> (DISPATCH_CHUNK.bit_length() - 1)
            n_small = (count - n_big * DISPATCH_CHUNK) >> (SUBLANES.bit_length() - 1)

            def copy(row, rows, first=first, off=off):
                return pltpu.make_async_copy(
                    stage_ref.at[slot_, pl.ds(pl.multiple_of(off + row, SUBLANES), rows)],
                    xs_ref.at[pl.ds(pl.multiple_of(first + row, SUBLANES), rows)], sem.at[slot_])

            def big(c, carry, copy=copy):
                action(copy(c * DISPATCH_CHUNK, DISPATCH_CHUNK))
                return carry

            def small(c, carry, copy=copy, n_big=n_big):
                action(copy(n_big * DISPATCH_CHUNK + c * SUBLANES, SUBLANES))
                return carry

            lax.fori_loop(0, n_big, big, 0)
            lax.fori_loop(0, n_small, small, 0)

    @pl.when(step == 0)
    def _():
        base_ref[...] = jnp.zeros_like(base_ref)

    @pl.when(step >= 2)
    def _():
        run_copies(slot, lambda cp: cp.wait())

    m = mod_ref[0]
    h = _rms_mod(x_ref[0], nw_ref[...], m[4:5], m[3:4])
    w = rw_ref[...]
    w_hi = w.astype(BF16)
    w_lo = (w - w_hi.astype(F32)).astype(BF16)
    h_hi = h.astype(BF16)
    h_lo = (h - h_hi.astype(F32)).astype(BF16)
    logits = _dot(h_hi, w_hi) + (_dot(h_lo, w_hi) + _dot(h_hi, w_lo))
    tm = logits.shape[0]
    lane = lax.broadcasted_iota(jnp.int32, logits.shape, 1)
    lg = jnp.where(lane < N_EXPERTS, logits, -jnp.inf)
    m1 = jnp.max(lg, axis=-1, keepdims=True)
    i1 = jnp.min(jnp.where(lg == m1, lane, LANES), axis=-1, keepdims=True)
    lg2 = jnp.where(lane == i1, -jnp.inf, lg)
    m2 = jnp.max(lg2, axis=-1, keepdims=True)
    i2 = jnp.min(jnp.where(lg2 == m2, lane, LANES), axis=-1, keepdims=True)
    e2 = jnp.exp(m2 - m1)
    den = 1.0 + e2
    hot = jnp.where(jnp.logical_or(lane == i1, lane == i2), 1.0, 0.0)
    r = lax.broadcasted_iota(jnp.int32, (tm, tm), 0)
    c = lax.broadcasted_iota(jnp.int32, (tm, tm), 1)
    before = jnp.where(r > c, 1.0, 0.0).astype(BF16)
    excl = _dot(before, hot.astype(BF16))
    base = base_ref[...]
    count = jnp.sum(hot, axis=0, keepdims=True)
    padded = jnp.floor((count + (SUBLANES - 1)) * (1.0 / SUBLANES)) * SUBLANES
    er = lax.broadcasted_iota(jnp.int32, (LANES, LANES), 0)
    ec = lax.broadcasted_iota(jnp.int32, (LANES, LANES), 1)
    lower = jnp.where(er < ec, 1.0, 0.0).astype(BF16)
    groups = jnp.broadcast_to(padded * (1.0 / SUBLANES), (SUBLANES, LANES)).astype(BF16)
    stage_off = _dot(groups, lower)[0:1] * SUBLANES
    pick = lambda idx, table: jnp.sum(jnp.where(lane == idx, table, 0.0), axis=-1, keepdims=True)
    tot = base + excl
    pos1 = i1.astype(F32) * cap + pick(i1, tot)
    pos2 = i2.astype(F32) * cap + pick(i2, tot)
    loc1 = pick(i1, stage_off + excl)
    loc2 = pick(i2, stage_off + excl)
    meta = (jnp.where(lane == 0, pos1, 0.0) + jnp.where(lane == 1, pos2, 0.0)
            + jnp.where(lane == 2, 1.0 / den, 0.0) + jnp.where(lane == 3, e2 / den, 0.0)
            + jnp.where(lane == 4, i1.astype(F32), 0.0) + jnp.where(lane == 5, i2.astype(F32), 0.0)
            + jnp.where(lane == 6, loc1, 0.0) + jnp.where(lane == 7, loc2, 0.0))
    meta_ref[0] = meta
    sub = lax.broadcasted_iota(jnp.int32, (SUBLANES, LANES), 0)
    tb = jnp.zeros((SUBLANES, LANES), F32)
    for j in range(tm // COMBINE_TILE):
        tb = jnp.where(sub == j, tot[j * COMBINE_TILE:j * COMBINE_TILE + 1, :], tb)
    tb_ref[0, 0] = tb
    base_ref[...] = base + padded
    cnt_ref[...] = jnp.broadcast_to(base + padded, cnt_ref.shape)

    meta_t = meta.T
    j = lax.broadcasted_iota(jnp.int32, (stage_ref.shape[1], tm), 0)
    sel = jnp.logical_or(j == meta_t[6:7].astype(jnp.int32), j == meta_t[7:8].astype(jnp.int32))
    stage_ref[slot] = _dot(jnp.where(sel, 1.0, 0.0).astype(BF16), h_hi)
    record = jnp.where(sub == 0, base, jnp.where(sub == 1, padded, jnp.where(sub == 2, stage_off, 0.0)))
    sc_vmem[...] = record.astype(jnp.int32)
    to_smem = pltpu.make_async_copy(sc_vmem, sc_smem.at[slot], ssem)
    to_smem.start()
    to_smem.wait()
    run_copies(slot, lambda cp: cp.start())

    @pl.when(step == n_steps - 1)
    def _():
        zero_ref[...] = jnp.zeros_like(zero_ref)

        def tail_copy(e):
            end = sc_smem[slot, 0, e] + sc_smem[slot, 1, e]
            return pltpu.make_async_copy(zero_ref, xs_ref.at[pl.ds(pl.multiple_of(e * cap + end, SUBLANES), tg)], zsem)

        for e in range(N_EXPERTS):
            tail_copy(e).start()
        for e in range(N_EXPERTS):
            tail_copy(e).wait()
        run_copies(slot, lambda cp: cp.wait())

        @pl.when(n_steps >= 2)
        def _():
            run_copies(1 - slot, lambda cp: cp.wait())


def _route(x, mod, nw, rw, tm, cap, tg):
    B, S, D = x.shape
    n_stage = 2 * tm + N_EXPERTS * SUBLANES
    return pl.pallas_call(
        functools.partial(_route_kernel, cap=cap, tg=tg),
        grid=(B, S // tm),
        in_specs=[pl.BlockSpec((1, tm, D), lambda b, i: (b, i, 0)),
                  pl.BlockSpec((1, 6, D), lambda b, i: (b, 0, 0)),
                  pl.BlockSpec((1, D), lambda b, i: (0, 0)),
                  pl.BlockSpec((D, LANES), lambda b, i: (0, 0))],
        out_specs=[pl.BlockSpec((1, tm, LANES), lambda b, i: (b, i, 0)),
                   pl.BlockSpec((SUBLANES, LANES), lambda b, i: (0, 0)),
                   pl.BlockSpec((1, 1, SUBLANES, LANES), lambda b, i: (b, i, 0, 0)),
                   pl.BlockSpec(memory_space=pl.ANY)],
        out_shape=[jax.ShapeDtypeStruct((B, S, LANES), F32),
                   jax.ShapeDtypeStruct((SUBLANES, LANES), F32),
                   jax.ShapeDtypeStruct((B, S // tm, SUBLANES, LANES), F32),
                   jax.ShapeDtypeStruct((N_EXPERTS * cap, D), F32)],
        scratch_shapes=[pltpu.VMEM((1, LANES), F32),
                        pltpu.VMEM((2, n_stage, D), F32),
                        pltpu.VMEM((tg, D), F32),
                        pltpu.VMEM((SUBLANES, LANES), jnp.int32),
                        pltpu.SMEM((2, SUBLANES, LANES), jnp.int32),
                        pltpu.SemaphoreType.DMA((2,)),
                        pltpu.SemaphoreType.DMA,
                        pltpu.SemaphoreType.DMA],
        compiler_params=_cparams(("arbitrary", "arbitrary")),
        name="moe_route",
    )(x, mod, nw, rw)


def _moe_group_kernel(blk_ref, exp_ref, fa_ref, fb_ref, nact_ref, x_ref, w1_ref, w3_ref, w2_ref, o_ref, acc_ref):
    t = pl.program_id(0)
    f = pl.program_id(1)

    @pl.when(jnp.logical_and(t == 0, f == 0))
    def _():
        acc_ref[...] = jnp.zeros_like(acc_ref)

    @pl.when(t < nact_ref[0])
    def _():
        h = x_ref[...].astype(BF16)
        a = _dot(h, w1_ref[0])
        b = _dot(h, w3_ref[0])
        y = _dot(((a * _sigmoid(a)) * b).astype(BF16), w2_ref[0])
        total = jnp.where(f == 0, 0.0, acc_ref[...]) + y
        acc_ref[...] = total
        o_ref[...] = total.astype(o_ref.dtype)


def _moe_group(blk, exp, fa, fb, nact, xs, w1, w3, w2, tg):
    R, D = xs.shape
    F = w1.shape[2]
    tf = F // 2
    fsel = lambda f, fa, fb, t: jnp.where(f == 0, fa[t], fb[t])
    return pl.pallas_call(
        _moe_group_kernel,
        grid_spec=pltpu.PrefetchScalarGridSpec(
            num_scalar_prefetch=5,
            grid=(blk.shape[0], 2),
            in_specs=[pl.BlockSpec((tg, D), lambda t, f, blk, exp, fa, fb, na: (blk[t], 0)),
                      pl.BlockSpec((1, D, tf), lambda t, f, blk, exp, fa, fb, na: (exp[t], 0, fsel(f, fa, fb, t))),
                      pl.BlockSpec((1, D, tf), lambda t, f, blk, exp, fa, fb, na: (exp[t], 0, fsel(f, fa, fb, t))),
                      pl.BlockSpec((1, tf, D), lambda t, f, blk, exp, fa, fb, na: (exp[t], fsel(f, fa, fb, t), 0))],
            out_specs=pl.BlockSpec((tg, D), lambda t, f, blk, exp, fa, fb, na: (blk[t], 0)),
            scratch_shapes=[pltpu.VMEM((tg, D), F32)]),
        out_shape=jax.ShapeDtypeStruct((R, D), BF16),
        compiler_params=_cparams(("arbitrary", "arbitrary")),
        name="moe_experts",
    )(blk, exp, fa, fb, nact, xs, w1, w3, w2)


def _combine_kernel(tb_ref, x_ref, mod_ref, meta_ref, fn_ref, y_ref, o_ref, buf_ref, sem, *, cap):
    i = pl.program_id(0)
    n = pl.num_programs(0)
    tc = x_ref.shape[0]
    n_rows = buf_ref.shape[1]
    shift = COMBINE_CHUNK.bit_length() - 1

    def segments(tile):
        segs, off = [], 0
        for e in range(N_EXPERTS):
            first = tb_ref[tile * N_EXPERTS + e]
            count = tb_ref[(tile + 1) * N_EXPERTS + e] - first
            lead = first & (COMBINE_CHUNK - 1)
            n_chunks = jnp.where(count > 0, lax.shift_right_logical(lead + count + COMBINE_CHUNK - 1, shift), 0)
            segs.append((e * cap + first - lead, n_chunks, off))
            off = off + n_chunks * COMBINE_CHUNK
        return segs

    def chunk_copy(src_row, dst_row, slot):
        return pltpu.make_async_copy(y_ref.at[pl.ds(pl.multiple_of(src_row, COMBINE_CHUNK), COMBINE_CHUNK)],
                                     buf_ref.at[slot, pl.ds(pl.multiple_of(dst_row, COMBINE_CHUNK), COMBINE_CHUNK)],
                                     sem.at[slot])

    def for_each_chunk(tile, slot, action):
        for src, n_chunks, off in segments(tile):
            def body(c, carry):
                action(chunk_copy(src + c * COMBINE_CHUNK, off + c * COMBINE_CHUNK, slot))
                return carry

            lax.fori_loop(0, n_chunks, body, 0)

    @pl.when(i == 0)
    def _():
        buf_ref[...] = jnp.zeros_like(buf_ref)
        for_each_chunk(0, 0, lambda cp: cp.start())

    slot = i % 2

    @pl.when(i + 1 < n)
    def _():
        for_each_chunk(i + 1, 1 - slot, lambda cp: cp.start())

    for_each_chunk(i, slot, lambda cp: cp.wait())
    meta = meta_ref[...]
    segs = segments(i)

    def one_hot(pos, exp):
        delta = jnp.zeros_like(pos)
        for e, (src, _, off) in enumerate(segs):
            delta = jnp.where(exp == e, off - src, delta)
        col = lax.broadcasted_iota(jnp.int32, (tc, n_rows), 1)
        return jnp.where(col == pos + delta, 1.0, 0.0).astype(BF16)

    as_int = lambda lane: meta[:, lane:lane + 1].astype(jnp.int32)
    picks = jnp.concatenate([one_hot(as_int(0), as_int(4)), one_hot(as_int(1), as_int(5))], axis=0)
    y12 = _dot(picks, buf_ref[slot])
    mix = meta[:, 2:3] * y12[0:tc] + meta[:, 3:4] * y12[tc:2 * tc]
    o_ref[...] = _rms(x_ref[...] + mod_ref[0][5:6] * mix, fn_ref[...])


def _combine(tile_base, x, mod, meta, fin, y, seq_len, cap):
    T, D = x.shape
    tc = COMBINE_TILE
    per_batch = seq_len // tc
    n_rows = -(-(2 * tc + N_EXPERTS * (2 * (COMBINE_CHUNK - 1) + SUBLANES - 1)) // LANES) * LANES
    return pl.pallas_call(
        functools.partial(_combine_kernel, cap=cap),
        grid_spec=pltpu.PrefetchScalarGridSpec(
            num_scalar_prefetch=1,
            grid=(T // tc,),
            in_specs=[pl.BlockSpec((tc, D), lambda i, tb: (i, 0)),
                      pl.BlockSpec((1, 6, D), lambda i, tb: (i // per_batch, 0, 0)),
                      pl.BlockSpec((tc, LANES), lambda i, tb: (i, 0)),
                      pl.BlockSpec((1, D), lambda i, tb: (0, 0)),
                      pl.BlockSpec(memory_space=pl.ANY)],
            out_specs=pl.BlockSpec((tc, D), lambda i, tb: (i, 0)),
            scratch_shapes=[pltpu.VMEM((2, n_rows, D), BF16), pltpu.SemaphoreType.DMA((2,))]),
        out_shape=jax.ShapeDtypeStruct((T, D), F32),
        compiler_params=_cparams(("arbitrary",)),
        name="moe_combine",
    )(tile_base, x, mod, meta, fin, y)


def _moe_tiles(cnt, n_tiles, tg, cap):
    per = (cnt + tg - 1) // tg
    cum = jnp.cumsum(per)
    nact = cum[-1]
    t = jnp.arange(n_tiles, dtype=jnp.int32)
    tt = jnp.minimum(t, nact - 1)
    exp = jnp.minimum(jnp.sum((tt[:, None] >= cum[None, :]).astype(jnp.int32), axis=1), N_EXPERTS - 1)
    blk = exp * (cap // tg) + tt - (cum - per)[exp]
    odd = tt % 2
    fa = jnp.where(t < nact, odd, 1 - odd)
    fb = 1 - odd
    i32 = lambda v: v.astype(jnp.int32)
    return i32(blk), i32(exp), i32(fa), i32(fb), i32(nact.reshape(1))


def _ffn_kernel(*refs, n_cast):
    x_ref, mod_ref, nw_ref, w1_ref, w3_ref, w2_ref = refs[0:6]
    cast_in = refs[6:6 + n_cast]
    o_ref = refs[6 + n_cast]
    cast_out = refs[7 + n_cast:7 + 2 * n_cast]
    (acc_ref,) = refs[7 + 2 * n_cast:]
    f = pl.program_id(2)

    @pl.when(jnp.logical_and(jnp.logical_and(pl.program_id(0) == 0, pl.program_id(1) == 0), f == 0))
    def _():
        acc_ref[...] = jnp.zeros_like(acc_ref)

    for src, dst in zip(cast_in, cast_out):
        dst[...] = src[...].astype(dst.dtype)
    m = mod_ref[0]
    x = x_ref[0]
    h = _rms_mod(x, nw_ref[...], m[4:5], m[3:4]).astype(BF16)
    a = _dot(h, w1_ref[...])
    b = _dot(h, w3_ref[...])
    total = jnp.where(f == 0, 0.0, acc_ref[...]) + _dot(((a * _sigmoid(a)) * b).astype(BF16), w2_ref[...])
    acc_ref[...] = total
    o_ref[0] = x + m[5:6] * total


def _ffn(x, mod, mod_per_batch, nw, w1, w3, w2, tm, tf, cast=()):
    B, S, D = x.shape
    F = w1.shape[1]
    n_i, n_f = S // tm, F // tf
    mod_map = (lambda b, i, f: (b, 0, 0)) if mod_per_batch else (lambda b, i, f: (0, 0, 0))
    cast_specs, cast_shapes = _cast_specs(cast, (B, n_i, n_f))
    tok = pl.BlockSpec((1, tm, D), lambda b, i, f: (b, i, 0))
    outs = pl.pallas_call(
        functools.partial(_ffn_kernel, n_cast=len(cast)),
        grid=(B, n_i, n_f),
        in_specs=[tok,
                  pl.BlockSpec((1, 6, D), mod_map),
                  pl.BlockSpec((1, D), lambda b, i, f: (0, 0)),
                  pl.BlockSpec((D, tf), lambda b, i, f: (0, f)),
                  pl.BlockSpec((D, tf), lambda b, i, f: (0, f)),
                  pl.BlockSpec((tf, D), lambda b, i, f: (f, 0))] + cast_specs,
        out_specs=[tok] + cast_specs,
        out_shape=[jax.ShapeDtypeStruct((B, S, D), F32)] + cast_shapes,
        scratch_shapes=[pltpu.VMEM((tm, D), F32)],
        compiler_params=_cparams(("parallel", "parallel", "arbitrary")),
        name="ffn",
    )(x, mod, nw, w1, w3, w2, *cast)
    return outs[0], tuple(outs[1:])


def _proj_odd_kernel(x_ref, mod_ref, nw_ref, w_ref, gate_ref, rec_ref):
    m = mod_ref[0]
    h = _rms_mod(x_ref[0], nw_ref[...], m[1:2], m[0:1]).astype(BF16)
    gate_ref[0] = _gelu(_dot(h, w_ref[:, 0:D_RNN])).astype(gate_ref.dtype)
    rec_ref[0] = _dot(h, w_ref[:, D_RNN:2 * D_RNN])


def _proj_odd(x, mod, mod_per_batch, nw, w, tm):
    B, S, D = x.shape
    mod_map = (lambda b, i: (b, 0, 0)) if mod_per_batch else (lambda b, i: (0, 0, 0))
    tok = lambda width: pl.BlockSpec((1, tm, width), lambda b, i: (b, i, 0))
    return pl.pallas_call(
        _proj_odd_kernel,
        grid=(B, S // tm),
        in_specs=[tok(D),
                  pl.BlockSpec((1, 6, D), mod_map),
                  pl.BlockSpec((1, D), lambda b, i: (0, 0)),
                  pl.BlockSpec((D, 2 * D_RNN), lambda b, i: (0, 0))],
        out_specs=[tok(D_RNN), tok(D_RNN)],
        out_shape=[jax.ShapeDtypeStruct((B, S, D_RNN), BF16), jax.ShapeDtypeStruct((B, S, D_RNN), F32)],
        compiler_params=_cparams(("parallel", "parallel")),
        name="proj_odd",
    )(x, mod, nw, w)


def _scan8(a_ref, b_ref, d, r0, h, row, reverse):
    base = pl.multiple_of(SUBLANES + r0, SUBLANES)
    a = a_ref[d, pl.ds(base, SUBLANES), :]
    b = b_ref[d, pl.ds(base, SUBLANES), :]
    for s in (1, 2, 4):
        live = row < SUBLANES - s if reverse else row >= s
        if s == 1:
            shifted = base + 1 if reverse else base - 1
            a_s = a_ref[d, pl.ds(shifted, SUBLANES), :]
            b_s = b_ref[d, pl.ds(shifted, SUBLANES), :]
        else:
            shift = SUBLANES - s if reverse else s
            a_s, b_s = pltpu.roll(a, shift, 0), pltpu.roll(b, shift, 0)
        b = jnp.where(live, a * b_s + b, b)
        a = jnp.where(live, a * a_s, a)
    hr = a * h + b
    return hr, (hr[0:1] if reverse else hr[SUBLANES - 1:SUBLANES])


def _lru_kernel(rec_ref, recc_ref, cw_ref, cb_ref, wa_ref, ba_ref, wx_ref, bx_ref, lam_ref,
                s_ref, pad_ref, a_ref, b_ref, cpad_ref, ca_ref, cbb_ref, park_ref, *, tile):
    S = rec_ref.shape[1]
    L = recc_ref.shape[1]
    cw = cw_ref[...]
    cb = cb_ref[...]
    lam = lam_ref[...]
    sp = jnp.maximum(-lam, 0.0) + jnp.log1p(jnp.exp(-jnp.abs(lam)))
    z_half = (-0.5 * LRU_C) * sp
    z_half_log2e = z_half * LOG2E
    neg_z_quarter = -0.5 * z_half
    half_ba = 0.5 * ba_ref[...]
    half_bx = 0.5 * bx_ref[...]
    zeros8 = jnp.zeros((SUBLANES, LANES), F32)

    def coefficients(src_ref, dst_a, dst_b, n_rows, t):
        pad = cpad_ref if src_ref is recc_ref else pad_ref
        pad[0:SUBLANES, :] = zeros8
        pad[SUBLANES + n_rows:2 * SUBLANES + n_rows, :] = zeros8
        for d in range(2):
            for dst in (dst_a, dst_b):
                dst[d, 0:SUBLANES, :] = zeros8
                dst[d, SUBLANES + n_rows:2 * SUBLANES + n_rows, :] = zeros8

        def copy(j, carry):
            r0 = pl.multiple_of(j * t, t)
            pad[pl.ds(SUBLANES + r0, t), :] = src_ref[0, pl.ds(r0, t), :]
            return carry

        lax.fori_loop(0, n_rows // t, copy, 0)

        def body(j, carry):
            r0 = pl.multiple_of(j * t, t)
            conv = cb
            for tap in range(4):
                conv = conv + cw[tap:tap + 1] * pad[pl.ds(r0 + (SUBLANES - 2 + tap), t), :]
            cbf = conv.astype(BF16)
            half_conv = 0.5 * conv
            for d in range(2):
                tr = jnp.tanh(_dot(cbf, wa_ref[d, 0]) + half_ba[d:d + 1])
                ti = jnp.tanh(_dot(cbf, wx_ref[d, 0]) + half_bx[d:d + 1])
                a = jnp.exp2(z_half_log2e[d:d + 1] * tr + z_half_log2e[d:d + 1])
                y = jnp.tanh(neg_z_quarter[d:d + 1] * tr + neg_z_quarter[d:d + 1])
                root = jnp.where(y > 0.0, y * lax.rsqrt(y), 0.0)
                dst_a[d, pl.ds(SUBLANES + r0, t), :] = a
                dst_b[d, pl.ds(SUBLANES + r0, t), :] = (root * (1.0 + a)) * ((ti + 1.0) * half_conv)
            return carry

        lax.fori_loop(0, n_rows // t, body, 0)

    row = lax.broadcasted_iota(jnp.int32, (SUBLANES, LANES), 0)
    h_zero = jnp.zeros((1, LANES), F32)

    coefficients(recc_ref, ca_ref, cbb_ref, L, L)
    nc = L // SUBLANES

    def ctx_body(j, carry):
        hf, hb = carry
        rf = pl.multiple_of(j * SUBLANES, SUBLANES)
        rb = pl.multiple_of((nc - 1 - j) * SUBLANES, SUBLANES)
        _, hf = _scan8(ca_ref, cbb_ref, 0, rf, hf, row, False)
        _, hb = _scan8(ca_ref, cbb_ref, 1, rb, hb, row, True)
        return hf, hb

    h0f, h0b = lax.fori_loop(0, nc, ctx_body, (h_zero, h_zero))

    coefficients(rec_ref, a_ref, b_ref, S, tile)
    n = S // SUBLANES

    def lat_body(second_half):
        def body(j, carry):
            hf, hb = carry
            rf = pl.multiple_of(j * SUBLANES, SUBLANES)
            rb = pl.multiple_of((n - 1 - j) * SUBLANES, SUBLANES)
            of, hf = _scan8(a_ref, b_ref, 0, rf, hf, row, False)
            ob, hb = _scan8(a_ref, b_ref, 1, rb, hb, row, True)
            if second_half:
                park_ref[0, pl.ds(rf - half, SUBLANES), :] = of
                park_ref[1, pl.ds(rb, SUBLANES), :] = ob
            else:
                s_ref[0, pl.ds(rf, SUBLANES), :] = of
                s_ref[0, pl.ds(rb, SUBLANES), :] = ob
            return hf, hb
        return body

    half = S // 2
    mid = lax.fori_loop(0, n // 2, lat_body(False), (h0f, h0b), unroll=8)
    lax.fori_loop(n // 2, n, lat_body(True), mid, unroll=8)

    def add_parked(j, carry):
        r0 = pl.multiple_of(j * tile, tile)
        s_ref[0, pl.ds(half + r0, tile), :] += park_ref[0, pl.ds(r0, tile), :]
        s_ref[0, pl.ds(r0, tile), :] += park_ref[1, pl.ds(r0, tile), :]
        return carry

    lax.fori_loop(0, half // tile, add_parked, 0)


def _lru(rec, rec_c, conv_w, conv_b, wa, ba, wx, bx, lam, tile):
    B, S, _ = rec.shape
    L = rec_c.shape[1]
    blk = lambda rows: pl.BlockSpec((1, rows, LRU_BLOCK), lambda b, j: (b, 0, j))
    vec = lambda rows: pl.BlockSpec((rows, LRU_BLOCK), lambda b, j: (0, j))
    wspec = pl.BlockSpec((2, 1, LRU_BLOCK, LRU_BLOCK), lambda b, j: (0, j, 0, 0))
    return pl.pallas_call(
        functools.partial(_lru_kernel, tile=tile),
        grid=(B, LRU_BLOCKS),
        in_specs=[blk(S), blk(L), vec(4), vec(1), wspec, vec(2), wspec, vec(2), vec(2)],
        out_specs=blk(S),
        out_shape=jax.ShapeDtypeStruct((B, S, D_RNN), F32),
        scratch_shapes=[pltpu.VMEM((S + 2 * SUBLANES, LRU_BLOCK), F32),
                        pltpu.VMEM((2, S + 2 * SUBLANES, LRU_BLOCK), F32),
                        pltpu.VMEM((2, S + 2 * SUBLANES, LRU_BLOCK), F32),
                        pltpu.VMEM((L + 2 * SUBLANES, LRU_BLOCK), F32),
                        pltpu.VMEM((2, L + 2 * SUBLANES, LRU_BLOCK), F32),
                        pltpu.VMEM((2, L + 2 * SUBLANES, LRU_BLOCK), F32),
                        pltpu.VMEM((2, S // 2, LRU_BLOCK), F32)],
        compiler_params=_cparams(("parallel", "parallel")),
        name="lru_scan",
    )(rec, rec_c, conv_w, conv_b, wa, ba, wx, bx, lam)


def _lru_out_kernel(x_ref, mod_ref, gate_ref, s_ref, w_ref, o_ref):
    y = (gate_ref[0].astype(F32) * s_ref[0]).astype(BF16)
    o_ref[0] = x_ref[0] + mod_ref[0][2:3] * _dot(y, w_ref[...])


def _lru_out(x, mod, gate, s, w, tm):
    B, S, D = x.shape
    tok = lambda width: pl.BlockSpec((1, tm, width), lambda b, i: (b, i, 0))
    return pl.pallas_call(
        _lru_out_kernel,
        grid=(B, S // tm),
        in_specs=[tok(D), pl.BlockSpec((1, 6, D), lambda b, i: (b, 0, 0)), tok(D_RNN), tok(D_RNN),
                  pl.BlockSpec((D_RNN, D), lambda b, i: (0, 0))],
        out_specs=tok(D),
        out_shape=jax.ShapeDtypeStruct((B, S, D), F32),
        compiler_params=_cparams(("parallel", "parallel")),
        name="lru_out",
    )(x, mod, gate, s, w)


def _rope_tables(n_tok):
    rows = n_tok // GRID_W
    row = jnp.repeat(jnp.arange(rows, dtype=F32), GRID_W)
    col = jnp.tile(jnp.arange(GRID_W, dtype=F32), rows)
    freqs = ROPE_BASE ** (-jnp.arange(ROPE_FREQS, dtype=F32) / ROPE_FREQS)
    ar, ac = row[:, None] * freqs, col[:, None] * freqs
    cos = jnp.concatenate([jnp.cos(ar), jnp.cos(ar), jnp.cos(ac), jnp.cos(ac)], axis=-1)
    sin = jnp.concatenate([-jnp.sin(ar), jnp.sin(ar), -jnp.sin(ac), jnp.sin(ac)], axis=-1)
    return jnp.tile(cos, (1, LANES // HEAD_DIM)), jnp.tile(sin, (1, LANES // HEAD_DIM))


def kernel(x, c, ctx, c_ctx, ada_w_e, ada_b_e, norm1_e, norm2_e, w_in_e, sgu_w, sgu_b, attn_sink, w_out_e, ffn_w1, ffn_w3, ffn_w2, ada_w_o, ada_b_o, norm1_o, norm2_o, w_in_o, conv_w, conv_b, lru_wa, lru_ba, lru_wx, lru_bx, lru_lambda, w_out_o, router_w, moe_w1, moe_w3, moe_w2, final_norm):
    B, S, D = x.shape
    L = ctx.shape[1]
    cvec = jnp.concatenate([c, c_ctx[None], jnp.zeros((SUBLANES - B - 1, D), F32)], axis=0)
    mod_e = _ada_params(cvec, ada_w_e[0], ada_b_e[0])
    mod_o = _ada_params(cvec, ada_w_o[0], ada_b_o[0])
    lat_e, ctx_e = mod_e[0:B], mod_e[B:B + 1]
    lat_o, ctx_o = mod_o[0:B], mod_o[B:B + 1]
    bf = lambda t: t.astype(BF16)
    row = lambda t: t.reshape(1, -1)

    cos, sin = _rope_tables(S)
    cos_c, sin_c = jnp.ones((L, LANES), F32), jnp.zeros((L, LANES), F32)
    w_in = bf(w_in_e[0])
    n1, n2 = row(norm1_e[0]), row(norm2_e[0])
    (uc, vc, qc, kc2, vc2), _ = _proj_even(ctx, ctx_e, False, n1, w_in, cos_c, sin_c, L)
    (u, v, q, k2, v2), later_w = _proj_even(x, lat_e, True, n1, w_in, cos, sin, 1024,
                                            cast=(ffn_w1, ffn_w3, ffn_w2, w_out_e, w_in_o, w_out_o))
    w1, w3, w2, wout, w_in_odd, w_out_odd = [t[0] for t in later_w]
    ws, bs_t = bf(sgu_w[0]), sgu_b[0].T
    sink = attn_sink[0] * LOG2E
    x = _mixer_even(x, lat_e, True, u, v, q, k2, v2, kc2, vc2, ws, bs_t, sink, wout, 1024, False)
    xc = _mixer_even(ctx, ctx_e, False, uc, vc, qc, kc2, vc2, kc2, vc2, ws, bs_t, sink, wout, L, True)
    x, moe_w = _ffn(x, lat_e, True, n2, w1, w3, w2, 512, 1408, cast=(moe_w1[0], moe_w3[0], moe_w2[0]))
    xc, _ = _ffn(xc, ctx_e, False, n2, w1, w3, w2, L, 1408)

    w_in = w_in_odd
    n1, n2 = row(norm1_o[0]), row(norm2_o[0])
    _, rec_c = _proj_odd(xc, ctx_o, False, n1, w_in, L)
    gate, rec = _proj_odd(x, lat_o, True, n1, w_in, 1024)
    s = _lru(rec, rec_c, conv_w[0], row(conv_b[0]), bf(0.5 * lru_wa[0]), lru_ba[0], bf(0.5 * lru_wx[0]), lru_bx[0],
             lru_lambda[0], 512)
    x = _lru_out(x, lat_o, gate, s, w_out_odd, 1024)
    rw = jnp.pad(router_w[0], ((0, 0), (0, LANES - N_EXPERTS)))
    T = B * S
    tg = 512
    tm = 512
    run_pad = (T // tm) * (SUBLANES - 1)
    cap = -(-(T + run_pad) // tg) * tg + 2 * tg
    meta, cnt, tb4, xs = _route(x, lat_o, n2, rw, tm, cap, tg)
    cnt = cnt[0, 0:N_EXPERTS].astype(jnp.int32)
    tile_base = jnp.concatenate([tb4[:, :, 0:tm // COMBINE_TILE, 0:N_EXPERTS].astype(jnp.int32).reshape(-1), cnt])
    n_tiles = -(-(2 * T + N_EXPERTS * run_pad) // tg) + N_EXPERTS
    blk, exp, fa, fb, nact = _moe_tiles(cnt, n_tiles, tg, cap)
    y = _moe_group(blk, exp, fa, fb, nact, xs, *moe_w, tg)
    out = _combine(tile_base, x.reshape(T, D), lat_o, meta.reshape(T, LANES), row(final_norm), y, S, cap)
    return out.reshape(B, S, D)
```

```python
import functools

import jax
import jax.numpy as jnp
from jax import lax
from jax.experimental import pallas as pl
from jax.experimental.pallas import tpu as pltpu

F32 = jnp.float32
BF16 = jnp.bfloat16

D_MODEL = 1024
GRID_W = 64
EPS = 1e-6
NEG_INF = -1e30
CHUNK = 128
SGU_GROUPS = 4
SGU_WIDTH = 512
HEAD_DIM = 64
N_Q_HEADS = 8
N_KV_HEADS = 2
ATTN_WIDTH = 512
KV_WIDTH = 128
WINDOW = 128
ATTN_BLOCK = 128
ATTN_SCALE = HEAD_DIM ** -0.5
LOG2E = 1.4426950408889634
ROW_BLOCK = 32
COMBINE_TILE = 256
COMBINE_CHUNK = 16
DISPATCH_CHUNK = 64
ROPE_BASE = 10000.0
ROPE_FREQS = 16
IN_EVEN = 1792
D_RNN = 1280
LRU_BLOCKS = 10
LRU_BLOCK = 128
LRU_C = 8.0
D_FF = 2816
N_EXPERTS = 8
LANES = 128
SUBLANES = 8
VMEM_LIMIT = 56 * 1024 * 1024


def _cparams(sem):
    return pltpu.CompilerParams(dimension_semantics=sem, vmem_limit_bytes=VMEM_LIMIT)


def _dot(a, b):
    return jnp.dot(a, b, preferred_element_type=F32)


def _dot_nt(a, b):
    return lax.dot_general(a, b, (((1,), (1,)), ((), ())), preferred_element_type=F32)


def _gelu(x):
    return 0.5 * x * (1.0 + jnp.tanh(0.7978845608028654 * (x + 0.044715 * (x * x * x))))


def _sigmoid(x):
    return 0.5 * jnp.tanh(0.5 * x) + 0.5


def _rms(x, nw):
    return (x * lax.rsqrt(jnp.mean(x * x, axis=-1, keepdims=True) + EPS)) * nw


def _rms_mod(x, nw, scale, shift):
    return _rms(x, nw) * (1.0 + scale) + shift


def _ada_kernel(c_ref, w_ref, b_ref, o_ref):
    c = c_ref[...]
    act = c * _sigmoid(c)
    o_ref[...] = _dot(act.astype(BF16), w_ref[...].astype(BF16)) + b_ref[...]


def _ada_params(cvec, w, b):
    n = w.shape[1]
    tn = 1536
    out = pl.pallas_call(
        _ada_kernel,
        grid=(n // tn,),
        in_specs=[pl.BlockSpec((SUBLANES, D_MODEL), lambda j: (0, 0)),
                  pl.BlockSpec((D_MODEL, tn), lambda j: (0, j)),
                  pl.BlockSpec((1, tn), lambda j: (0, j))],
        out_specs=pl.BlockSpec((SUBLANES, tn), lambda j: (0, j)),
        out_shape=jax.ShapeDtypeStruct((SUBLANES, n), F32),
        compiler_params=_cparams(("parallel",)),
        name="ada_params",
    )(cvec, w, b.reshape(1, n))
    return out.reshape(SUBLANES, 6, D_MODEL)


def _cast_specs(cast, grid):
    n_steps = 1
    for g in grid:
        n_steps *= g
    specs, shapes = [], []
    for arr in cast:
        E, R, C = arr.shape
        per = n_steps // E
        assert per * E == n_steps and R % (per * 2 * SUBLANES) == 0, (arr.shape, n_steps)

        def cast_map(*idx, per=per):
            step = 0
            for g, i in zip(grid, idx):
                step = step * g + i
            return step // per, step % per, 0

        specs.append(pl.BlockSpec((1, R // per, C), cast_map))
        shapes.append(jax.ShapeDtypeStruct(arr.shape, BF16))
    return specs, shapes


def _proj_even_kernel(x_ref, mod_ref, nw_ref, w_ref, cos_ref, sin_ref, *rest, n_cast):
    cast_in = rest[0:n_cast]
    u_ref, v_ref, q_ref, k_ref, val_ref = rest[n_cast:n_cast + 5]
    cast_out = rest[n_cast + 5:]
    for src, dst in zip(cast_in, cast_out):
        dst[...] = src[...].astype(dst.dtype)
    m = mod_ref[0]
    h = _rms_mod(x_ref[0], nw_ref[...], m[1:2], m[0:1]).astype(BF16)
    u_ref[0] = _gelu(_dot(h, w_ref[:, 0:SGU_WIDTH])).astype(u_ref.dtype)
    v_ref[0] = _gelu(_dot(h, w_ref[:, SGU_WIDTH:2 * SGU_WIDTH])).astype(v_ref.dtype)
    cos = cos_ref[...]
    sin = sin_ref[...]
    lane = lax.broadcasted_iota(jnp.int32, cos.shape, 1)
    first_half = (lane % 32) < ROPE_FREQS

    def rope(t):
        partner = jnp.where(first_half, pltpu.roll(t, LANES - ROPE_FREQS, 1), pltpu.roll(t, ROPE_FREQS, 1))
        return t * cos + partner * sin

    q = _dot(h, w_ref[:, 2 * SGU_WIDTH:2 * SGU_WIDTH + ATTN_WIDTH]) * (ATTN_SCALE * LOG2E)
    for g in range(ATTN_WIDTH // LANES):
        q_ref[0, :, g * LANES:(g + 1) * LANES] = rope(q[:, g * LANES:(g + 1) * LANES]).astype(BF16)
    kv = _dot(h, w_ref[:, 2 * SGU_WIDTH + ATTN_WIDTH:IN_EVEN])
    k = rope(kv[:, 0:KV_WIDTH])
    val = kv[:, KV_WIDTH:2 * KV_WIDTH]
    k_ref[0, :, 0:LANES] = k.astype(BF16)
    k_ref[0, :, LANES:2 * LANES] = pltpu.roll(k, HEAD_DIM, 1).astype(BF16)
    val_ref[0, :, 0:LANES] = val.astype(BF16)
    val_ref[0, :, LANES:2 * LANES] = pltpu.roll(val, HEAD_DIM, 1).astype(BF16)


def _proj_even(x, mod, mod_per_batch, nw, w, cos, sin, tm, cast=()):
    B, S, D = x.shape
    mod_map = (lambda b, i: (b, 0, 0)) if mod_per_batch else (lambda b, i: (0, 0, 0))
    tok = lambda width: pl.BlockSpec((1, tm, width), lambda b, i: (b, i, 0))
    cast_specs, cast_shapes = _cast_specs(cast, (B, S // tm))
    outs = pl.pallas_call(
        functools.partial(_proj_even_kernel, n_cast=len(cast)),
        grid=(B, S // tm),
        in_specs=[tok(D),
                  pl.BlockSpec((1, 6, D), mod_map),
                  pl.BlockSpec((1, D), lambda b, i: (0, 0)),
                  pl.BlockSpec((D, IN_EVEN), lambda b, i: (0, 0)),
                  pl.BlockSpec((tm, LANES), lambda b, i: (i, 0)),
                  pl.BlockSpec((tm, LANES), lambda b, i: (i, 0))] + cast_specs,
        out_specs=[tok(SGU_WIDTH), tok(SGU_WIDTH), tok(ATTN_WIDTH), tok(2 * KV_WIDTH), tok(2 * KV_WIDTH)] + cast_specs,
        out_shape=[jax.ShapeDtypeStruct((B, S, SGU_WIDTH), BF16),
                   jax.ShapeDtypeStruct((B, S, SGU_WIDTH), BF16),
                   jax.ShapeDtypeStruct((B, S, ATTN_WIDTH), BF16),
                   jax.ShapeDtypeStruct((B, S, 2 * KV_WIDTH), BF16),
                   jax.ShapeDtypeStruct((B, S, 2 * KV_WIDTH), BF16)] + cast_shapes,
        compiler_params=_cparams(("parallel", "parallel")),
        name="proj_even",
    )(x, mod, nw, w, cos, sin, *cast)
    return tuple(outs[0:5]), tuple(outs[5:])


def _mixer_even_kernel(sink_ref, x_ref, mod_ref, u_ref, v_ref, q_ref, k_ref, val_ref, kc_ref, vc_ref,
                       ws_ref, bs_ref, wout_ref, bias_ref, o_ref, mix_ref, s_ref, p_ref, inv_ref,
                       *, seq_len, is_ctx):
    tq = x_ref.shape[1]
    n_chunks = tq // CHUNK
    i = pl.program_id(1)
    nk = kc_ref.shape[1] + (0 if is_ctx else 3 * ATTN_BLOCK)
    lane = lax.broadcasted_iota(jnp.int32, (1, LANES), 1)
    lo = lane < HEAD_DIM
    zero = jnp.zeros((), BF16)

    def halves(ref_slice, kh):
        nat, swp = ref_slice[:, 0:LANES], ref_slice[:, LANES:2 * LANES]
        if kh == 0:
            return jnp.where(lo, nat, zero), jnp.where(lo, zero, swp)
        return jnp.where(lo, swp, zero), jnp.where(lo, zero, nat)

    kc_all = kc_ref[0]
    vc_all = vc_ref[0]

    def chunk_body(c, carry):
        r0 = pl.multiple_of(c * CHUNK, CHUNK)
        rows = pl.ds(r0, CHUNK)
        vch = v_ref[0, rows, :].astype(F32)
        uch = u_ref[0, rows, :].astype(F32)
        for g in range(SGU_GROUPS):
            cols = slice(g * LANES, (g + 1) * LANES)
            vg = vch[:, cols]
            dev = vg - jnp.mean(vg, axis=-1, keepdims=True)
            vn = dev * lax.rsqrt(jnp.mean(dev * dev, axis=-1, keepdims=True) + EPS)
            mixed = _dot(ws_ref[g], vn.astype(BF16)) + bs_ref[:, g:g + 1]
            mix_ref[rows, cols] = (uch[:, cols] * mixed).astype(BF16)
        qch = q_ref[0, rows, :]
        if not is_ctx:
            blk = i * n_chunks + c
            n_blk = seq_len // ATTN_BLOCK
            start = pl.multiple_of(jnp.clip((blk - 1) * ATTN_BLOCK, 0, seq_len - 3 * ATTN_BLOCK), ATTN_BLOCK)
            k3 = k_ref[0, pl.ds(start, 3 * ATTN_BLOCK), :]
            v3 = val_ref[0, pl.ds(start, 3 * ATTN_BLOCK), :]
            case = jnp.where(blk == 0, 0, jnp.where(blk == n_blk - 1, 2, 1))
        for kh in range(N_KV_HEADS):
            kc_lo, kc_hi = halves(kc_all, kh)
            vc_lo, vc_hi = halves(vc_all, kh)
            if is_ctx:
                k_cat = jnp.concatenate([kc_lo, kc_hi], axis=0)
                v_cat = (vc_lo, vc_hi)
            else:
                k_lo, k_hi = halves(k3, kh)
                v_lo, v_hi = halves(v3, kh)
                k_cat = jnp.concatenate([k_lo, kc_lo, k_hi, kc_hi], axis=0)
                v_cat = (jnp.concatenate([v_lo, vc_lo], axis=0), jnp.concatenate([v_hi, vc_hi], axis=0))
            q2 = jnp.concatenate([qch[:, 2 * kh * LANES:(2 * kh + 1) * LANES],
                                  qch[:, (2 * kh + 1) * LANES:(2 * kh + 2) * LANES]], axis=0)
            s_ref[:, 0:2 * nk] = _dot_nt(q2, k_cat)
            for half in range(2):
                for rb in range(2 * ATTN_BLOCK // ROW_BLOCK):
                    rsl = slice(rb * ROW_BLOCK, (rb + 1) * ROW_BLOCK)
                    snk = sink_ref[2 * (2 * kh + rb * ROW_BLOCK // ATTN_BLOCK) + half]
                    s = s_ref[rsl, half * nk:(half + 1) * nk]
                    if not is_ctx:
                        qoff = (rb * ROW_BLOCK) % ATTN_BLOCK
                        s_loc = s[:, 0:3 * ATTN_BLOCK] + bias_ref[case, qoff:qoff + ROW_BLOCK, :]
                        s = jnp.concatenate([s_loc, s[:, 3 * ATTN_BLOCK:]], axis=1)
                    m = jnp.maximum(jnp.max(s, axis=-1, keepdims=True), snk)
                    p = jnp.exp2(s - m)
                    den = jnp.sum(p, axis=-1, keepdims=True) + jnp.exp2(snk - m)
                    p_ref[rsl, half * nk:(half + 1) * nk] = p.astype(BF16)
                    inv_ref[rsl, half * LANES:(half + 1) * LANES] = jnp.broadcast_to(1.0 / den, (ROW_BLOCK, LANES))
            o_lo = _dot(p_ref[:, 0:nk], v_cat[0])
            o_hi = _dot(p_ref[:, nk:2 * nk], v_cat[1])
            acc = (o_lo * inv_ref[:, 0:LANES] + o_hi * inv_ref[:, LANES:2 * LANES]).astype(BF16)
            for g in range(2):
                col = SGU_WIDTH + (2 * kh + g) * LANES
                mix_ref[rows, col:col + LANES] = acc[g * ATTN_BLOCK:(g + 1) * ATTN_BLOCK]
        return carry

    lax.fori_loop(0, n_chunks, chunk_body, 0)
    y = _dot(mix_ref[...], wout_ref[...])
    o_ref[0] = x_ref[0] + mod_ref[0][2:3] * y


def _window_bias():
    case = jnp.arange(3, dtype=jnp.int32)[:, None, None]
    qi = jnp.arange(ATTN_BLOCK, dtype=jnp.int32)[None, :, None]
    kj = jnp.arange(3 * ATTN_BLOCK, dtype=jnp.int32)[None, None, :]
    return jnp.where(jnp.abs(kj - case * ATTN_BLOCK - qi) <= WINDOW, 0.0, NEG_INF).astype(F32)


def _mixer_even(x, mod, mod_per_batch, u, v, q, k2, v2, kc2, vc2, ws, bs_t, sink, wout, tq, is_ctx):
    B, S, D = x.shape
    Sk = k2.shape[1]
    Lc = kc2.shape[1]
    nk = Lc + (0 if is_ctx else 3 * ATTN_BLOCK)
    mod_map = (lambda b, i: (b, 0, 0)) if mod_per_batch else (lambda b, i: (0, 0, 0))
    tok = lambda width: pl.BlockSpec((1, tq, width), lambda b, i: (b, i, 0))
    per_batch = lambda rows: pl.BlockSpec((1, rows, 2 * KV_WIDTH), lambda b, i: (b, 0, 0))
    return pl.pallas_call(
        functools.partial(_mixer_even_kernel, seq_len=S, is_ctx=is_ctx),
        grid=(B, S // tq),
        in_specs=[pl.BlockSpec(memory_space=pltpu.SMEM),
                  tok(D),
                  pl.BlockSpec((1, 6, D), mod_map),
                  tok(SGU_WIDTH), tok(SGU_WIDTH), tok(ATTN_WIDTH),
                  per_batch(Sk), per_batch(Sk), per_batch(Lc), per_batch(Lc),
                  pl.BlockSpec((SGU_GROUPS, CHUNK, CHUNK), lambda b, i: (0, 0, 0)),
                  pl.BlockSpec((CHUNK, SGU_GROUPS), lambda b, i: (0, 0)),
                  pl.BlockSpec((D, D), lambda b, i: (0, 0)),
                  pl.BlockSpec((3, ATTN_BLOCK, 3 * ATTN_BLOCK), lambda b, i: (0, 0, 0))],
        out_specs=tok(D),
        out_shape=jax.ShapeDtypeStruct((B, S, D), F32),
        scratch_shapes=[pltpu.VMEM((tq, D), BF16),
                        pltpu.VMEM((2 * ATTN_BLOCK, 2 * nk), F32),
                        pltpu.VMEM((2 * ATTN_BLOCK, 2 * nk), BF16),
                        pltpu.VMEM((2 * ATTN_BLOCK, 2 * LANES), F32)],
        compiler_params=_cparams(("parallel", "arbitrary")),
        name="mixer_ctx" if is_ctx else "mixer_even",
    )(sink, x, mod, u, v, q, k2, v2, kc2, vc2, ws, bs_t, wout, _window_bias())


def _route_kernel(x_ref, mod_ref, nw_ref, rw_ref, meta_ref, cnt_ref, tb_ref, xs_ref,
                  base_ref, stage_ref, zero_ref, sc_vmem, sc_smem, sem, zsem, ssem, *, cap, tg):
    step = pl.program_id(0) * pl.num_programs(1) + pl.program_id(1)
    n_steps = pl.num_programs(0) * pl.num_programs(1)
    slot = step % 2

    def run_copies(slot_, action):
        for e in range(N_EXPERTS):
            first = e * cap + sc_smem[slot_, 0, e]
            count = sc_smem[slot_, 1, e]
            off = sc_smem[slot_, 2, e]
            n_big = count >> (DISPATCH_CHUNK.bit_length() - 1)
            n_small = (count - n_big * DISPATCH_CHUNK) >> (SUBLANES.bit_length() - 1)

            def copy(row, rows, first=first, off=off):
                return pltpu.make_async_copy(
                    stage_ref.at[slot_, pl.ds(pl.multiple_of(off + row, SUBLANES), rows)],
                    xs_ref.at[pl.ds(pl.multiple_of(first + row, SUBLANES), rows)], sem.at[slot_])

            def big(c, carry, copy=copy):
                action(copy(c * DISPATCH_CHUNK, DISPATCH_CHUNK))
                return carry

            def small(c, carry, copy=copy, n_big=n_big):
                action(copy(n_big * DISPATCH_CHUNK + c * SUBLANES, SUBLANES))
                return carry

            lax.fori_loop(0, n_big, big, 0)
            lax.fori_loop(0, n_small, small, 0)

    @pl.when(step == 0)
    def _():
        base_ref[...] = jnp.zeros_like(base_ref)

    @pl.when(step >= 2)
    def _():
        run_copies(slot, lambda cp: cp.wait())

    m = mod_ref[0]
    h = _rms_mod(x_ref[0], nw_ref[...], m[4:5], m[3:4])
    w = rw_ref[...]
    w_hi = w.astype(BF16)
    w_lo = (w - w_hi.astype(F32)).astype(BF16)
    h_hi = h.astype(BF16)
    h_lo = (h - h_hi.astype(F32)).astype(BF16)
    logits = _dot(h_hi, w_hi) + (_dot(h_lo, w_hi) + _dot(h_hi, w_lo))
    tm = logits.shape[0]
    lane = lax.broadcasted_iota(jnp.int32, logits.shape, 1)
    lg = jnp.where(lane < N_EXPERTS, logits, -jnp.inf)
    m1 = jnp.max(lg, axis=-1, keepdims=True)
    i1 = jnp.min(jnp.where(lg == m1, lane, LANES), axis=-1, keepdims=True)
    lg2 = jnp.where(lane == i1, -jnp.inf, lg)
    m2 = jnp.max(lg2, axis=-1, keepdims=True)
    i2 = jnp.min(jnp.where(lg2 == m2, lane, LANES), axis=-1, keepdims=True)
    e2 = jnp.exp(m2 - m1)
    den = 1.0 + e2
    hot = jnp.where(jnp.logical_or(lane == i1, lane == i2), 1.0, 0.0)
    r = lax.broadcasted_iota(jnp.int32, (tm, tm), 0)
    c = lax.broadcasted_iota(jnp.int32, (tm, tm), 1)
    before = jnp.where(r > c, 1.0, 0.0).astype(BF16)
    excl = _dot(before, hot.astype(BF16))
    base = base_ref[...]
    count = jnp.sum(hot, axis=0, keepdims=True)
    padded = jnp.floor((count + (SUBLANES - 1)) * (1.0 / SUBLANES)) * SUBLANES
    er = lax.broadcasted_iota(jnp.int32, (LANES, LANES), 0)
    ec = lax.broadcasted_iota(jnp.int32, (LANES, LANES), 1)
    lower = jnp.where(er < ec, 1.0, 0.0).astype(BF16)
    groups = jnp.broadcast_to(padded * (1.0 / SUBLANES), (SUBLANES, LANES)).astype(BF16)
    stage_off = _dot(groups, lower)[0:1] * SUBLANES
    pick = lambda idx, table: jnp.sum(jnp.where(lane == idx, table, 0.0), axis=-1, keepdims=True)
    tot = base + excl
    pos1 = i1.astype(F32) * cap + pick(i1, tot)
    pos2 = i2.astype(F32) * cap + pick(i2, tot)
    loc1 = pick(i1, stage_off + excl)
    loc2 = pick(i2, stage_off + excl)
    meta = (jnp.where(lane == 0, pos1, 0.0) + jnp.where(lane == 1, pos2, 0.0)
            + jnp.where(lane == 2, 1.0 / den, 0.0) + jnp.where(lane == 3, e2 / den, 0.0)
            + jnp.where(lane == 4, i1.astype(F32), 0.0) + jnp.where(lane == 5, i2.astype(F32), 0.0)
            + jnp.where(lane == 6, loc1, 0.0) + jnp.where(lane == 7, loc2, 0.0))
    meta_ref[0] = meta
    sub = lax.broadcasted_iota(jnp.int32, (SUBLANES, LANES), 0)
    tb = jnp.zeros((SUBLANES, LANES), F32)
    for j in range(tm // COMBINE_TILE):
        tb = jnp.where(sub == j, tot[j * COMBINE_TILE:j * COMBINE_TILE + 1, :], tb)
    tb_ref[0, 0] = tb
    base_ref[...] = base + padded
    cnt_ref[...] = jnp.broadcast_to(base + padded, cnt_ref.shape)

    meta_t = meta.T
    j = lax.broadcasted_iota(jnp.int32, (stage_ref.shape[1], tm), 0)
    sel = jnp.logical_or(j == meta_t[6:7].astype(jnp.int32), j == meta_t[7:8].astype(jnp.int32))
    stage_ref[slot] = _dot(jnp.where(sel, 1.0, 0.0).astype(BF16), h_hi)
    record = jnp.where(sub == 0, base, jnp.where(sub == 1, padded, jnp.where(sub == 2, stage_off, 0.0)))
    sc_vmem[...] = record.astype(jnp.int32)
    to_smem = pltpu.make_async_copy(sc_vmem, sc_smem.at[slot], ssem)
    to_smem.start()
    to_smem.wait()
    run_copies(slot, lambda cp: cp.start())

    @pl.when(step == n_steps - 1)
    def _():
        zero_ref[...] = jnp.zeros_like(zero_ref)

        def tail_copy(e):
            end = sc_smem[slot, 0, e] + sc_smem[slot, 1, e]
            return pltpu.make_async_copy(zero_ref, xs_ref.at[pl.ds(pl.multiple_of(e * cap + end, SUBLANES), tg)], zsem)

        for e in range(N_EXPERTS):
            tail_copy(e).start()
        for e in range(N_EXPERTS):
            tail_copy(e).wait()
        run_copies(slot, lambda cp: cp.wait())

        @pl.when(n_steps >= 2)
        def _():
            run_copies(1 - slot, lambda cp: cp.wait())


def _route(x, mod, nw, rw, tm, cap, tg):
    B, S, D = x.shape
    n_stage = 2 * tm + N_EXPERTS * SUBLANES
    return pl.pallas_call(
        functools.partial(_route_kernel, cap=cap, tg=tg),
        grid=(B, S // tm),
        in_specs=[pl.BlockSpec((1, tm, D), lambda b, i: (b, i, 0)),
                  pl.BlockSpec((1, 6, D), lambda b, i: (b, 0, 0)),
                  pl.BlockSpec((1, D), lambda b, i: (0, 0)),
                  pl.BlockSpec((D, LANES), lambda b, i: (0, 0))],
        out_specs=[pl.BlockSpec((1, tm, LANES), lambda b, i: (b, i, 0)),
                   pl.BlockSpec((SUBLANES, LANES), lambda b, i: (0, 0)),
                   pl.BlockSpec((1, 1, SUBLANES, LANES), lambda b, i: (b, i, 0, 0)),
                   pl.BlockSpec(memory_space=pl.ANY)],
        out_shape=[jax.ShapeDtypeStruct((B, S, LANES), F32),
                   jax.ShapeDtypeStruct((SUBLANES, LANES), F32),
                   jax.ShapeDtypeStruct((B, S // tm, SUBLANES, LANES), F32),
                   jax.ShapeDtypeStruct((N_EXPERTS * cap, D), F32)],
        scratch_shapes=[pltpu.VMEM((1, LANES), F32),
                        pltpu.VMEM((2, n_stage, D), F32),
                        pltpu.VMEM((tg, D), F32),
                        pltpu.VMEM((SUBLANES, LANES), jnp.int32),
                        pltpu.SMEM((2, SUBLANES, LANES), jnp.int32),
                        pltpu.SemaphoreType.DMA((2,)),
                        pltpu.SemaphoreType.DMA,
                        pltpu.SemaphoreType.DMA],
        compiler_params=_cparams(("arbitrary", "arbitrary")),
        name="moe_route",
    )(x, mod, nw, rw)


def _moe_group_kernel(blk_ref, exp_ref, fa_ref, fb_ref, nact_ref, x_ref, w1_ref, w3_ref, w2_ref, o_ref, acc_ref):
    t = pl.program_id(0)
    f = pl.program_id(1)

    @pl.when(jnp.logical_and(t == 0, f == 0))
    def _():
        acc_ref[...] = jnp.zeros_like(acc_ref)

    @pl.when(t < nact_ref[0])
    def _():
        h = x_ref[...].astype(BF16)
        a = _dot(h, w1_ref[0])
        b = _dot(h, w3_ref[0])
        y = _dot(((a * _sigmoid(a)) * b).astype(BF16), w2_ref[0])
        total = jnp.where(f == 0, 0.0, acc_ref[...]) + y
        acc_ref[...] = total
        o_ref[...] = total.astype(o_ref.dtype)


def _moe_group(blk, exp, fa, fb, nact, xs, w1, w3, w2, tg):
    R, D = xs.shape
    F = w1.shape[2]
    tf = F // 2
    fsel = lambda f, fa, fb, t: jnp.where(f == 0, fa[t], fb[t])
    return pl.pallas_call(
        _moe_group_kernel,
        grid_spec=pltpu.PrefetchScalarGridSpec(
            num_scalar_prefetch=5,
            grid=(blk.shape[0], 2),
            in_specs=[pl.BlockSpec((tg, D), lambda t, f, blk, exp, fa, fb, na: (blk[t], 0)),
                      pl.BlockSpec((1, D, tf), lambda t, f, blk, exp, fa, fb, na: (exp[t], 0, fsel(f, fa, fb, t))),
                      pl.BlockSpec((1, D, tf), lambda t, f, blk, exp, fa, fb, na: (exp[t], 0, fsel(f, fa, fb, t))),
                      pl.BlockSpec((1, tf, D), lambda t, f, blk, exp, fa, fb, na: (exp[t], fsel(f, fa, fb, t), 0))],
            out_specs=pl.BlockSpec((tg, D), lambda t, f, blk, exp, fa, fb, na: (blk[t], 0)),
            scratch_shapes=[pltpu.VMEM((tg, D), F32)]),
        out_shape=jax.ShapeDtypeStruct((R, D), BF16),
        compiler_params=_cparams(("arbitrary", "arbitrary")),
        name="moe_experts",
    )(blk, exp, fa, fb, nact, xs, w1, w3, w2)


def _combine_kernel(tb_ref, x_ref, mod_ref, meta_ref, fn_ref, y_ref, o_ref, buf_ref, sem, *, cap):
    i = pl.program_id(0)
    n = pl.num_programs(0)
    tc = x_ref.shape[0]
    n_rows = buf_ref.shape[1]
    shift = COMBINE_CHUNK.bit_length() - 1

    def segments(tile):
        segs, off = [], 0
        for e in range(N_EXPERTS):
            first = tb_ref[tile * N_EXPERTS + e]
            count = tb_ref[(tile + 1) * N_EXPERTS + e] - first
            lead = first & (COMBINE_CHUNK - 1)
            n_chunks = jnp.where(count > 0, lax.shift_right_logical(lead + count + COMBINE_CHUNK - 1, shift), 0)
            segs.append((e * cap + first - lead, n_chunks, off))
            off = off + n_chunks * COMBINE_CHUNK
        return segs

    def chunk_copy(src_row, dst_row, slot):
        return pltpu.make_async_copy(y_ref.at[pl.ds(pl.multiple_of(src_row, COMBINE_CHUNK), COMBINE_CHUNK)],
                                     buf_ref.at[slot, pl.ds(pl.multiple_of(dst_row, COMBINE_CHUNK), COMBINE_CHUNK)],
                                     sem.at[slot])

    def for_each_chunk(tile, slot, action):
        for src, n_chunks, off in segments(tile):
            def body(c, carry):
                action(chunk_copy(src + c * COMBINE_CHUNK, off + c * COMBINE_CHUNK, slot))
                return carry

            lax.fori_loop(0, n_chunks, body, 0)

    @pl.when(i == 0)
    def _():
        buf_ref[...] = jnp.zeros_like(buf_ref)
        for_each_chunk(0, 0, lambda cp: cp.start())

    slot = i % 2

    @pl.when(i + 1 < n)
    def _():
        for_each_chunk(i + 1, 1 - slot, lambda cp: cp.start())

    for_each_chunk(i, slot, lambda cp: cp.wait())
    meta = meta_ref[...]
    segs = segments(i)

    def one_hot(pos, exp):
        delta = jnp.zeros_like(pos)
        for e, (src, _, off) in enumerate(segs):
            delta = jnp.where(exp == e, off - src, delta)
        col = lax.broadcasted_iota(jnp.int32, (tc, n_rows), 1)
        return jnp.where(col == pos + delta, 1.0, 0.0).astype(BF16)

    as_int = lambda lane: meta[:, lane:lane + 1].astype(jnp.int32)
    picks = jnp.concatenate([one_hot(as_int(0), as_int(4)), one_hot(as_int(1), as_int(5))], axis=0)
    y12 = _dot(picks, buf_ref[slot])
    mix = meta[:, 2:3] * y12[0:tc] + meta[:, 3:4] * y12[tc:2 * tc]
    o_ref[...] = _rms(x_ref[...] + mod_ref[0][5:6] * mix, fn_ref[...])


def _combine(tile_base, x, mod, meta, fin, y, seq_len, cap):
    T, D = x.shape
    tc = COMBINE_TILE
    per_batch = seq_len // tc
    n_rows = -(-(2 * tc + N_EXPERTS * (2 * (COMBINE_CHUNK - 1) + SUBLANES - 1)) // LANES) * LANES
    return pl.pallas_call(
        functools.partial(_combine_kernel, cap=cap),
        grid_spec=pltpu.PrefetchScalarGridSpec(
            num_scalar_prefetch=1,
            grid=(T // tc,),
            in_specs=[pl.BlockSpec((tc, D), lambda i, tb: (i, 0)),
                      pl.BlockSpec((1, 6, D), lambda i, tb: (i // per_batch, 0, 0)),
                      pl.BlockSpec((tc, LANES), lambda i, tb: (i, 0)),
                      pl.BlockSpec((1, D), lambda i, tb: (0, 0)),
                      pl.BlockSpec(memory_space=pl.ANY)],
            out_specs=pl.BlockSpec((tc, D), lambda i, tb: (i, 0)),
            scratch_shapes=[pltpu.VMEM((2, n_rows, D), BF16), pltpu.SemaphoreType.DMA((2,))]),
        out_shape=jax.ShapeDtypeStruct((T, D), F32),
        compiler_params=_cparams(("arbitrary",)),
        name="moe_combine",
    )(tile_base, x, mod, meta, fin, y)


def _moe_tiles(cnt, n_tiles, tg, cap):
    per = (cnt + tg - 1) // tg
    cum = jnp.cumsum(per)
    nact = cum[-1]
    t = jnp.arange(n_tiles, dtype=jnp.int32)
    tt = jnp.minimum(t, nact - 1)
    exp = jnp.minimum(jnp.sum((tt[:, None] >= cum[None, :]).astype(jnp.int32), axis=1), N_EXPERTS - 1)
    blk = exp * (cap // tg) + tt - (cum - per)[exp]
    odd = tt % 2
    fa = jnp.where(t < nact, odd, 1 - odd)
    fb = 1 - odd
    i32 = lambda v: v.astype(jnp.int32)
    return i32(blk), i32(exp), i32(fa), i32(fb), i32(nact.reshape(1))


def _ffn_kernel(*refs, n_cast):
    x_ref, mod_ref, nw_ref, w1_ref, w3_ref, w2_ref = refs[0:6]
    cast_in = refs[6:6 + n_cast]
    o_ref = refs[6 + n_cast]
    cast_out = refs[7 + n_cast:7 + 2 * n_cast]
    (acc_ref,) = refs[7 + 2 * n_cast:]
    f = pl.program_id(2)

    @pl.when(jnp.logical_and(jnp.logical_and(pl.program_id(0) == 0, pl.program_id(1) == 0), f == 0))
    def _():
        acc_ref[...] = jnp.zeros_like(acc_ref)

    for src, dst in zip(cast_in, cast_out):
        dst[...] = src[...].astype(dst.dtype)
    m = mod_ref[0]
    x = x_ref[0]
    h = _rms_mod(x, nw_ref[...], m[4:5], m[3:4]).astype(BF16)
    a = _dot(h, w1_ref[...])
    b = _dot(h, w3_ref[...])
    total = jnp.where(f == 0, 0.0, acc_ref[...]) + _dot(((a * _sigmoid(a)) * b).astype(BF16), w2_ref[...])
    acc_ref[...] = total
    o_ref[0] = x + m[5:6] * total


def _ffn(x, mod, mod_per_batch, nw, w1, w3, w2, tm, tf, cast=()):
    B, S, D = x.shape
    F = w1.shape[1]
    n_i, n_f = S // tm, F // tf
    mod_map = (lambda b, i, f: (b, 0, 0)) if mod_per_batch else (lambda b, i, f: (0, 0, 0))
    cast_specs, cast_shapes = _cast_specs(cast, (B, n_i, n_f))
    tok = pl.BlockSpec((1, tm, D), lambda b, i, f: (b, i, 0))
    outs = pl.pallas_call(
        functools.partial(_ffn_kernel, n_cast=len(cast)),
        grid=(B, n_i, n_f),
        in_specs=[tok,
                  pl.BlockSpec((1, 6, D), mod_map),
                  pl.BlockSpec((1, D), lambda b, i, f: (0, 0)),
                  pl.BlockSpec((D, tf), lambda b, i, f: (0, f)),
                  pl.BlockSpec((D, tf), lambda b, i, f: (0, f)),
                  pl.BlockSpec((tf, D), lambda b, i, f: (f, 0))] + cast_specs,
        out_specs=[tok] + cast_specs,
        out_shape=[jax.ShapeDtypeStruct((B, S, D), F32)] + cast_shapes,
        scratch_shapes=[pltpu.VMEM((tm, D), F32)],
        compiler_params=_cparams(("parallel", "parallel", "arbitrary")),
        name="ffn",
    )(x, mod, nw, w1, w3, w2, *cast)
    return outs[0], tuple(outs[1:])


def _proj_odd_kernel(x_ref, mod_ref, nw_ref, w_ref, gate_ref, rec_ref):
    m = mod_ref[0]
    h = _rms_mod(x_ref[0], nw_ref[...], m[1:2], m[0:1]).astype(BF16)
    gate_ref[0] = _gelu(_dot(h, w_ref[:, 0:D_RNN])).astype(gate_ref.dtype)
    rec_ref[0] = _dot(h, w_ref[:, D_RNN:2 * D_RNN])


def _proj_odd(x, mod, mod_per_batch, nw, w, tm):
    B, S, D = x.shape
    mod_map = (lambda b, i: (b, 0, 0)) if mod_per_batch else (lambda b, i: (0, 0, 0))
    tok = lambda width: pl.BlockSpec((1, tm, width), lambda b, i: (b, i, 0))
    return pl.pallas_call(
        _proj_odd_kernel,
        grid=(B, S // tm),
        in_specs=[tok(D),
                  pl.BlockSpec((1, 6, D), mod_map),
                  pl.BlockSpec((1, D), lambda b, i: (0, 0)),
                  pl.BlockSpec((D, 2 * D_RNN), lambda b, i: (0, 0))],
        out_specs=[tok(D_RNN), tok(D_RNN)],
        out_shape=[jax.ShapeDtypeStruct((B, S, D_RNN), BF16), jax.ShapeDtypeStruct((B, S, D_RNN), F32)],
        compiler_params=_cparams(("parallel", "parallel")),
        name="proj_odd",
    )(x, mod, nw, w)


def _scan8(a_ref, b_ref, d, r0, h, row, reverse):
    base = pl.multiple_of(SUBLANES + r0, SUBLANES)
    a = a_ref[d, pl.ds(base, SUBLANES), :]
    b = b_ref[d, pl.ds(base, SUBLANES), :]
    for s in (1, 2, 4):
        live = row < SUBLANES - s if reverse else row >= s
        if s == 1:
            shifted = base + 1 if reverse else base - 1
            a_s = a_ref[d, pl.ds(shifted, SUBLANES), :]
            b_s = b_ref[d, pl.ds(shifted, SUBLANES), :]
        else:
            shift = SUBLANES - s if reverse else s
            a_s, b_s = pltpu.roll(a, shift, 0), pltpu.roll(b, shift, 0)
        b = jnp.where(live, a * b_s + b, b)
        a = jnp.where(live, a * a_s, a)
    hr = a * h + b
    return hr, (hr[0:1] if reverse else hr[SUBLANES - 1:SUBLANES])


def _lru_kernel(rec_ref, recc_ref, gate_ref, cw_ref, cb_ref, wa_ref, ba_ref, wx_ref, bx_ref, lam_ref,
                y_ref, pad_ref, a_ref, b_ref, cpad_ref, ca_ref, cbb_ref, park_ref, s_ref, *, tile):
    S = rec_ref.shape[1]
    L = recc_ref.shape[1]
    cw = cw_ref[...]
    cb = cb_ref[...]
    lam = lam_ref[...]
    sp = jnp.maximum(-lam, 0.0) + jnp.log1p(jnp.exp(-jnp.abs(lam)))
    z_half = (-0.5 * LRU_C) * sp
    z_half_log2e = z_half * LOG2E
    neg_z_quarter = -0.5 * z_half
    half_ba = 0.5 * ba_ref[...]
    half_bx = 0.5 * bx_ref[...]
    zeros8 = jnp.zeros((SUBLANES, LANES), F32)

    def coefficients(src_ref, dst_a, dst_b, n_rows, t):
        pad = cpad_ref if src_ref is recc_ref else pad_ref
        pad[0:SUBLANES, :] = zeros8
        pad[SUBLANES + n_rows:2 * SUBLANES + n_rows, :] = zeros8
        for d in range(2):
            for dst in (dst_a, dst_b):
                dst[d, 0:SUBLANES, :] = zeros8
                dst[d, SUBLANES + n_rows:2 * SUBLANES + n_rows, :] = zeros8

        def copy(j, carry):
            r0 = pl.multiple_of(j * t, t)
            pad[pl.ds(SUBLANES + r0, t), :] = src_ref[0, pl.ds(r0, t), :]
            return carry

        lax.fori_loop(0, n_rows // t, copy, 0)

        def body(j, carry):
            r0 = pl.multiple_of(j * t, t)
            conv = cb
            for tap in range(4):
                conv = conv + cw[tap:tap + 1] * pad[pl.ds(r0 + (SUBLANES - 2 + tap), t), :]
            cbf = conv.astype(BF16)
            half_conv = 0.5 * conv
            for d in range(2):
                tr = jnp.tanh(_dot(cbf, wa_ref[d, 0]) + half_ba[d:d + 1])
                ti = jnp.tanh(_dot(cbf, wx_ref[d, 0]) + half_bx[d:d + 1])
                a = jnp.exp2(z_half_log2e[d:d + 1] * tr + z_half_log2e[d:d + 1])
                y = jnp.tanh(neg_z_quarter[d:d + 1] * tr + neg_z_quarter[d:d + 1])
                root = jnp.where(y > 0.0, y * lax.rsqrt(y), 0.0)
                dst_a[d, pl.ds(SUBLANES + r0, t), :] = a
                dst_b[d, pl.ds(SUBLANES + r0, t), :] = (root * (1.0 + a)) * ((ti + 1.0) * half_conv)
            return carry

        lax.fori_loop(0, n_rows // t, body, 0)

    row = lax.broadcasted_iota(jnp.int32, (SUBLANES, LANES), 0)
    h_zero = jnp.zeros((1, LANES), F32)

    coefficients(recc_ref, ca_ref, cbb_ref, L, L)
    nc = L // SUBLANES

    def ctx_body(j, carry):
        hf, hb = carry
        rf = pl.multiple_of(j * SUBLANES, SUBLANES)
        rb = pl.multiple_of((nc - 1 - j) * SUBLANES, SUBLANES)
        _, hf = _scan8(ca_ref, cbb_ref, 0, rf, hf, row, False)
        _, hb = _scan8(ca_ref, cbb_ref, 1, rb, hb, row, True)
        return hf, hb

    h0f, h0b = lax.fori_loop(0, nc, ctx_body, (h_zero, h_zero))

    coefficients(rec_ref, a_ref, b_ref, S, tile)
    n = S // SUBLANES

    def lat_body(second_half):
        def body(j, carry):
            hf, hb = carry
            rf = pl.multiple_of(j * SUBLANES, SUBLANES)
            rb = pl.multiple_of((n - 1 - j) * SUBLANES, SUBLANES)
            of, hf = _scan8(a_ref, b_ref, 0, rf, hf, row, False)
            ob, hb = _scan8(a_ref, b_ref, 1, rb, hb, row, True)
            if second_half:
                park_ref[0, pl.ds(rf - half, SUBLANES), :] = of
                park_ref[1, pl.ds(rb, SUBLANES), :] = ob
            else:
                s_ref[pl.ds(rf, SUBLANES), :] = of
                s_ref[pl.ds(rb, SUBLANES), :] = ob
            return hf, hb
        return body

    half = S // 2
    mid = lax.fori_loop(0, n // 2, lat_body(False), (h0f, h0b), unroll=8)
    lax.fori_loop(n // 2, n, lat_body(True), mid, unroll=8)

    def finish(j, carry):
        for rows, parked in ((pl.ds(pl.multiple_of(j * tile, tile), tile), 1),
                             (pl.ds(pl.multiple_of(half + j * tile, tile), tile), 0)):
            state = s_ref[rows, :] + park_ref[parked, pl.ds(pl.multiple_of(j * tile, tile), tile), :]
            y_ref[0, rows, :] = (gate_ref[0, rows, :].astype(F32) * state).astype(y_ref.dtype)
        return carry

    lax.fori_loop(0, half // tile, finish, 0)


def _lru(rec, rec_c, gate, conv_w, conv_b, wa, ba, wx, bx, lam, tile):
    B, S, _ = rec.shape
    L = rec_c.shape[1]
    blk = lambda rows: pl.BlockSpec((1, rows, LRU_BLOCK), lambda b, j: (b, 0, j))
    vec = lambda rows: pl.BlockSpec((rows, LRU_BLOCK), lambda b, j: (0, j))
    wspec = pl.BlockSpec((2, 1, LRU_BLOCK, LRU_BLOCK), lambda b, j: (0, j, 0, 0))
    return pl.pallas_call(
        functools.partial(_lru_kernel, tile=tile),
        grid=(B, LRU_BLOCKS),
        in_specs=[blk(S), blk(L), blk(S), vec(4), vec(1), wspec, vec(2), wspec, vec(2), vec(2)],
        out_specs=blk(S),
        out_shape=jax.ShapeDtypeStruct((B, S, D_RNN), BF16),
        scratch_shapes=[pltpu.VMEM((S + 2 * SUBLANES, LRU_BLOCK), F32),
                        pltpu.VMEM((2, S + 2 * SUBLANES, LRU_BLOCK), F32),
                        pltpu.VMEM((2, S + 2 * SUBLANES, LRU_BLOCK), F32),
                        pltpu.VMEM((L + 2 * SUBLANES, LRU_BLOCK), F32),
                        pltpu.VMEM((2, L + 2 * SUBLANES, LRU_BLOCK), F32),
                        pltpu.VMEM((2, L + 2 * SUBLANES, LRU_BLOCK), F32),
                        pltpu.VMEM((2, S // 2, LRU_BLOCK), F32),
                        pltpu.VMEM((S, LRU_BLOCK), F32)],
        compiler_params=_cparams(("parallel", "parallel")),
        name="lru_scan",
    )(rec, rec_c, gate, conv_w, conv_b, wa, ba, wx, bx, lam)


def _lru_out_kernel(x_ref, mod_ref, y_ref, w_ref, o_ref):
    o_ref[0] = x_ref[0] + mod_ref[0][2:3] * _dot(y_ref[0], w_ref[...])


def _lru_out(x, mod, y, w, tm):
    B, S, D = x.shape
    tok = lambda width: pl.BlockSpec((1, tm, width), lambda b, i: (b, i, 0))
    return pl.pallas_call(
        _lru_out_kernel,
        grid=(B, S // tm),
        in_specs=[tok(D), pl.BlockSpec((1, 6, D), lambda b, i: (b, 0, 0)), tok(D_RNN),
                  pl.BlockSpec((D_RNN, D), lambda b, i: (0, 0))],
        out_specs=tok(D),
        out_shape=jax.ShapeDtypeStruct((B, S, D), F32),
        compiler_params=_cparams(("parallel", "parallel")),
        name="lru_out",
    )(x, mod, y, w)


def _rope_tables(n_tok):
    rows = n_tok // GRID_W
    row = jnp.repeat(jnp.arange(rows, dtype=F32), GRID_W)
    col = jnp.tile(jnp.arange(GRID_W, dtype=F32), rows)
    freqs = ROPE_BASE ** (-jnp.arange(ROPE_FREQS, dtype=F32) / ROPE_FREQS)
    ar, ac = row[:, None] * freqs, col[:, None] * freqs
    cos = jnp.concatenate([jnp.cos(ar), jnp.cos(ar), jnp.cos(ac), jnp.cos(ac)], axis=-1)
    sin = jnp.concatenate([-jnp.sin(ar), jnp.sin(ar), -jnp.sin(ac), jnp.sin(ac)], axis=-1)
    return jnp.tile(cos, (1, LANES // HEAD_DIM)), jnp.tile(sin, (1, LANES // HEAD_DIM))


def kernel(x, c, ctx, c_ctx, ada_w_e, ada_b_e, norm1_e, norm2_e, w_in_e, sgu_w, sgu_b, attn_sink, w_out_e, ffn_w1, ffn_w3, ffn_w2, ada_w_o, ada_b_o, norm1_o, norm2_o, w_in_o, conv_w, conv_b, lru_wa, lru_ba, lru_wx, lru_bx, lru_lambda, w_out_o, router_w, moe_w1, moe_w3, moe_w2, final_norm):
    B, S, D = x.shape
    L = ctx.shape[1]
    cvec = jnp.concatenate([c, c_ctx[None], jnp.zeros((SUBLANES - B - 1, D), F32)], axis=0)
    mod_e = _ada_params(cvec, ada_w_e[0], ada_b_e[0])
    mod_o = _ada_params(cvec, ada_w_o[0], ada_b_o[0])
    lat_e, ctx_e = mod_e[0:B], mod_e[B:B + 1]
    lat_o, ctx_o = mod_o[0:B], mod_o[B:B + 1]
    bf = lambda t: t.astype(BF16)
    row = lambda t: t.reshape(1, -1)

    cos, sin = _rope_tables(S)
    cos_c, sin_c = jnp.ones((L, LANES), F32), jnp.zeros((L, LANES), F32)
    w_in = bf(w_in_e[0])
    n1, n2 = row(norm1_e[0]), row(norm2_e[0])
    (uc, vc, qc, kc2, vc2), _ = _proj_even(ctx, ctx_e, False, n1, w_in, cos_c, sin_c, L)
    (u, v, q, k2, v2), later_w = _proj_even(x, lat_e, True, n1, w_in, cos, sin, 1024,
                                            cast=(ffn_w1, ffn_w3, ffn_w2, w_out_e, w_in_o, w_out_o))
    w1, w3, w2, wout, w_in_odd, w_out_odd = [t[0] for t in later_w]
    ws, bs_t = bf(sgu_w[0]), sgu_b[0].T
    sink = attn_sink[0] * LOG2E
    x = _mixer_even(x, lat_e, True, u, v, q, k2, v2, kc2, vc2, ws, bs_t, sink, wout, 1024, False)
    xc = _mixer_even(ctx, ctx_e, False, uc, vc, qc, kc2, vc2, kc2, vc2, ws, bs_t, sink, wout, L, True)
    x, moe_w = _ffn(x, lat_e, True, n2, w1, w3, w2, 512, 1408, cast=(moe_w1[0], moe_w3[0], moe_w2[0]))
    xc, _ = _ffn(xc, ctx_e, False, n2, w1, w3, w2, L, 1408)

    w_in = w_in_odd
    n1, n2 = row(norm1_o[0]), row(norm2_o[0])
    _, rec_c = _proj_odd(xc, ctx_o, False, n1, w_in, L)
    gate, rec = _proj_odd(x, lat_o, True, n1, w_in, 1024)
    y = _lru(rec, rec_c, gate, conv_w[0], row(conv_b[0]), bf(0.5 * lru_wa[0]), lru_ba[0], bf(0.5 * lru_wx[0]),
             lru_bx[0], lru_lambda[0], 512)
    x = _lru_out(x, lat_o, y, w_out_odd, 1024)
    rw = jnp.pad(router_w[0], ((0, 0), (0, LANES - N_EXPERTS)))
    T = B * S
    tg = 512
    tm = 512
    run_pad = (T // tm) * (SUBLANES - 1)
    cap = -(-(T + run_pad) // tg) * tg + 2 * tg
    meta, cnt, tb4, xs = _route(x, lat_o, n2, rw, tm, cap, tg)
    cnt = cnt[0, 0:N_EXPERTS].astype(jnp.int32)
    tile_base = jnp.concatenate([tb4[:, :, 0:tm // COMBINE_TILE, 0:N_EXPERTS].astype(jnp.int32).reshape(-1), cnt])
    n_tiles = -(-(2 * T + N_EXPERTS * run_pad) // tg) + N_EXPERTS
    blk, exp, fa, fb, nact = _moe_tiles(cnt, n_tiles, tg, cap)
    y = _moe_group(blk, exp, fa, fb, nact, xs, *moe_w, tg)
    out = _combine(tile_base, x.reshape(T, D), lat_o, meta.reshape(T, LANES), row(final_norm), y, S, cap)
    return out.reshape(B, S, D)
```

```python
import functools

import jax
import jax.numpy as jnp
from jax import lax
from jax.experimental import pallas as pl
from jax.experimental.pallas import tpu as pltpu

F32 = jnp.float32
BF16 = jnp.bfloat16

D_MODEL = 1024
GRID_W = 64
EPS = 1e-6
NEG_INF = -1e30
CHUNK = 128
SGU_GROUPS = 4
SGU_WIDTH = 512
HEAD_DIM = 64
N_Q_HEADS = 8
N_KV_HEADS = 2
ATTN_WIDTH = 512
KV_WIDTH = 128
WINDOW = 128
ATTN_BLOCK = 128
ATTN_SCALE = HEAD_DIM ** -0.5
LOG2E = 1.4426950408889634
ROW_BLOCK = 32
COMBINE_TILE = 256
COMBINE_CHUNK = 16
DISPATCH_CHUNK = 64
ROPE_BASE = 10000.0
ROPE_FREQS = 16
IN_EVEN = 1792
D_RNN = 1280
LRU_BLOCKS = 10
LRU_BLOCK = 128
LRU_C = 8.0
D_FF = 2816
N_EXPERTS = 8
LANES = 128
SUBLANES = 8
VMEM_LIMIT = 56 * 1024 * 1024


def _cparams(sem):
    return pltpu.CompilerParams(dimension_semantics=sem, vmem_limit_bytes=VMEM_LIMIT)


def _dot(a, b):
    return jnp.dot(a, b, preferred_element_type=F32)


def _dot_nt(a, b):
    return lax.dot_general(a, b, (((1,), (1,)), ((), ())), preferred_element_type=F32)


def _gelu(x):
    return 0.5 * x * (1.0 + jnp.tanh(0.7978845608028654 * (x + 0.044715 * (x * x * x))))


def _sigmoid(x):
    return 0.5 * jnp.tanh(0.5 * x) + 0.5


def _rms(x, nw):
    return (x * lax.rsqrt(jnp.mean(x * x, axis=-1, keepdims=True) + EPS)) * nw


def _rms_mod(x, nw, scale, shift):
    return _rms(x, nw) * (1.0 + scale) + shift


def _ada_kernel(c_ref, w_ref, b_ref, o_ref):
    c = c_ref[...]
    act = c * _sigmoid(c)
    o_ref[...] = _dot(act.astype(BF16), w_ref[...].astype(BF16)) + b_ref[...]


def _ada_params(cvec, w, b):
    n = w.shape[1]
    tn = 1536
    out = pl.pallas_call(
        _ada_kernel,
        grid=(n // tn,),
        in_specs=[pl.BlockSpec((SUBLANES, D_MODEL), lambda j: (0, 0)),
                  pl.BlockSpec((D_MODEL, tn), lambda j: (0, j)),
                  pl.BlockSpec((1, tn), lambda j: (0, j))],
        out_specs=pl.BlockSpec((SUBLANES, tn), lambda j: (0, j)),
        out_shape=jax.ShapeDtypeStruct((SUBLANES, n), F32),
        compiler_params=_cparams(("parallel",)),
        name="ada_params",
    )(cvec, w, b.reshape(1, n))
    return out.reshape(SUBLANES, 6, D_MODEL)


def _cast_specs(cast, grid):
    n_steps = 1
    for g in grid:
        n_steps *= g
    specs, shapes = [], []
    for arr in cast:
        E, R, C = arr.shape
        per = n_steps // E
        assert per * E == n_steps and R % (per * 2 * SUBLANES) == 0, (arr.shape, n_steps)

        def cast_map(*idx, per=per):
            step = 0
            for g, i in zip(grid, idx):
                step = step * g + i
            return step // per, step % per, 0

        specs.append(pl.BlockSpec((1, R // per, C), cast_map))
        shapes.append(jax.ShapeDtypeStruct(arr.shape, BF16))
    return specs, shapes


def _proj_even_kernel(x_ref, mod_ref, nw_ref, w_ref, cos_ref, sin_ref, *rest, n_cast):
    cast_in = rest[0:n_cast]
    u_ref, v_ref, q_ref, k_ref, val_ref = rest[n_cast:n_cast + 5]
    cast_out = rest[n_cast + 5:]
    for src, dst in zip(cast_in, cast_out):
        dst[...] = src[...].astype(dst.dtype)
    m = mod_ref[0]
    h = _rms_mod(x_ref[0], nw_ref[...], m[1:2], m[0:1]).astype(BF16)
    u_ref[0] = _gelu(_dot(h, w_ref[:, 0:SGU_WIDTH])).astype(u_ref.dtype)
    v_ref[0] = _gelu(_dot(h, w_ref[:, SGU_WIDTH:2 * SGU_WIDTH])).astype(v_ref.dtype)
    cos = cos_ref[...]
    sin = sin_ref[...]
    lane = lax.broadcasted_iota(jnp.int32, cos.shape, 1)
    first_half = (lane % 32) < ROPE_FREQS

    def rope(t):
        partner = jnp.where(first_half, pltpu.roll(t, LANES - ROPE_FREQS, 1), pltpu.roll(t, ROPE_FREQS, 1))
        return t * cos + partner * sin

    q = _dot(h, w_ref[:, 2 * SGU_WIDTH:2 * SGU_WIDTH + ATTN_WIDTH]) * (ATTN_SCALE * LOG2E)
    for g in range(ATTN_WIDTH // LANES):
        q_ref[0, :, g * LANES:(g + 1) * LANES] = rope(q[:, g * LANES:(g + 1) * LANES]).astype(BF16)
    kv = _dot(h, w_ref[:, 2 * SGU_WIDTH + ATTN_WIDTH:IN_EVEN])
    k = rope(kv[:, 0:KV_WIDTH])
    val = kv[:, KV_WIDTH:2 * KV_WIDTH]
    k_ref[0, :, 0:LANES] = k.astype(BF16)
    k_ref[0, :, LANES:2 * LANES] = pltpu.roll(k, HEAD_DIM, 1).astype(BF16)
    val_ref[0, :, 0:LANES] = val.astype(BF16)
    val_ref[0, :, LANES:2 * LANES] = pltpu.roll(val, HEAD_DIM, 1).astype(BF16)


def _proj_even(x, mod, mod_per_batch, nw, w, cos, sin, tm, cast=()):
    B, S, D = x.shape
    mod_map = (lambda b, i: (b, 0, 0)) if mod_per_batch else (lambda b, i: (0, 0, 0))
    tok = lambda width: pl.BlockSpec((1, tm, width), lambda b, i: (b, i, 0))
    cast_specs, cast_shapes = _cast_specs(cast, (B, S // tm))
    outs = pl.pallas_call(
        functools.partial(_proj_even_kernel, n_cast=len(cast)),
        grid=(B, S // tm),
        in_specs=[tok(D),
                  pl.BlockSpec((1, 6, D), mod_map),
                  pl.BlockSpec((1, D), lambda b, i: (0, 0)),
                  pl.BlockSpec((D, IN_EVEN), lambda b, i: (0, 0)),
                  pl.BlockSpec((tm, LANES), lambda b, i: (i, 0)),
                  pl.BlockSpec((tm, LANES), lambda b, i: (i, 0))] + cast_specs,
        out_specs=[tok(SGU_WIDTH), tok(SGU_WIDTH), tok(ATTN_WIDTH), tok(2 * KV_WIDTH), tok(2 * KV_WIDTH)] + cast_specs,
        out_shape=[jax.ShapeDtypeStruct((B, S, SGU_WIDTH), BF16),
                   jax.ShapeDtypeStruct((B, S, SGU_WIDTH), BF16),
                   jax.ShapeDtypeStruct((B, S, ATTN_WIDTH), BF16),
                   jax.ShapeDtypeStruct((B, S, 2 * KV_WIDTH), BF16),
                   jax.ShapeDtypeStruct((B, S, 2 * KV_WIDTH), BF16)] + cast_shapes,
        compiler_params=_cparams(("parallel", "parallel")),
        name="proj_even",
    )(x, mod, nw, w, cos, sin, *cast)
    return tuple(outs[0:5]), tuple(outs[5:])


def _mixer_even_kernel(sink_ref, x_ref, mod_ref, u_ref, v_ref, q_ref, k_ref, val_ref, kc_ref, vc_ref,
                       ws_ref, bs_ref, wout_ref, bias_ref, o_ref, mix_ref, s_ref, p_ref, inv_ref,
                       *, seq_len, is_ctx):
    tq = x_ref.shape[1]
    n_chunks = tq // CHUNK
    i = pl.program_id(1)
    nk = kc_ref.shape[1] + (0 if is_ctx else 3 * ATTN_BLOCK)
    lane = lax.broadcasted_iota(jnp.int32, (1, LANES), 1)
    lo = lane < HEAD_DIM
    zero = jnp.zeros((), BF16)

    def halves(ref_slice, kh):
        nat, swp = ref_slice[:, 0:LANES], ref_slice[:, LANES:2 * LANES]
        if kh == 0:
            return jnp.where(lo, nat, zero), jnp.where(lo, zero, swp)
        return jnp.where(lo, swp, zero), jnp.where(lo, zero, nat)

    kc_all = kc_ref[0]
    vc_all = vc_ref[0]

    def chunk_body(c, carry):
        r0 = pl.multiple_of(c * CHUNK, CHUNK)
        rows = pl.ds(r0, CHUNK)
        vch = v_ref[0, rows, :].astype(F32)
        uch = u_ref[0, rows, :].astype(F32)
        for g in range(SGU_GROUPS):
            cols = slice(g * LANES, (g + 1) * LANES)
            vg = vch[:, cols]
            dev = vg - jnp.mean(vg, axis=-1, keepdims=True)
            vn = dev * lax.rsqrt(jnp.mean(dev * dev, axis=-1, keepdims=True) + EPS)
            mixed = _dot(ws_ref[g], vn.astype(BF16)) + bs_ref[:, g:g + 1]
            mix_ref[rows, cols] = (uch[:, cols] * mixed).astype(BF16)
        qch = q_ref[0, rows, :]
        if not is_ctx:
            blk = i * n_chunks + c
            n_blk = seq_len // ATTN_BLOCK
            start = pl.multiple_of(jnp.clip((blk - 1) * ATTN_BLOCK, 0, seq_len - 3 * ATTN_BLOCK), ATTN_BLOCK)
            k3 = k_ref[0, pl.ds(start, 3 * ATTN_BLOCK), :]
            v3 = val_ref[0, pl.ds(start, 3 * ATTN_BLOCK), :]
            case = jnp.where(blk == 0, 0, jnp.where(blk == n_blk - 1, 2, 1))
        for kh in range(N_KV_HEADS):
            kc_lo, kc_hi = halves(kc_all, kh)
            vc_lo, vc_hi = halves(vc_all, kh)
            if is_ctx:
                k_cat = jnp.concatenate([kc_lo, kc_hi], axis=0)
                v_cat = (vc_lo, vc_hi)
            else:
                k_lo, k_hi = halves(k3, kh)
                v_lo, v_hi = halves(v3, kh)
                k_cat = jnp.concatenate([k_lo, kc_lo, k_hi, kc_hi], axis=0)
                v_cat = (jnp.concatenate([v_lo, vc_lo], axis=0), jnp.concatenate([v_hi, vc_hi], axis=0))
            q2 = jnp.concatenate([qch[:, 2 * kh * LANES:(2 * kh + 1) * LANES],
                                  qch[:, (2 * kh + 1) * LANES:(2 * kh + 2) * LANES]], axis=0)
            s_ref[:, 0:2 * nk] = _dot_nt(q2, k_cat)
            for half in range(2):
                for rb in range(2 * ATTN_BLOCK // ROW_BLOCK):
                    rsl = slice(rb * ROW_BLOCK, (rb + 1) * ROW_BLOCK)
                    snk = sink_ref[2 * (2 * kh + rb * ROW_BLOCK // ATTN_BLOCK) + half]
                    s = s_ref[rsl, half * nk:(half + 1) * nk]
                    if not is_ctx:
                        qoff = (rb * ROW_BLOCK) % ATTN_BLOCK
                        s_loc = s[:, 0:3 * ATTN_BLOCK] + bias_ref[case, qoff:qoff + ROW_BLOCK, :]
                        s = jnp.concatenate([s_loc, s[:, 3 * ATTN_BLOCK:]], axis=1)
                    m = jnp.maximum(jnp.max(s, axis=-1, keepdims=True), snk)
                    p = jnp.exp2(s - m)
                    den = jnp.sum(p, axis=-1, keepdims=True) + jnp.exp2(snk - m)
                    p_ref[rsl, half * nk:(half + 1) * nk] = p.astype(BF16)
                    inv_ref[rsl, half * LANES:(half + 1) * LANES] = jnp.broadcast_to(1.0 / den, (ROW_BLOCK, LANES))
            o_lo = _dot(p_ref[:, 0:nk], v_cat[0])
            o_hi = _dot(p_ref[:, nk:2 * nk], v_cat[1])
            acc = (o_lo * inv_ref[:, 0:LANES] + o_hi * inv_ref[:, LANES:2 * LANES]).astype(BF16)
            for g in range(2):
                col = SGU_WIDTH + (2 * kh + g) * LANES
                mix_ref[rows, col:col + LANES] = acc[g * ATTN_BLOCK:(g + 1) * ATTN_BLOCK]
        return carry

    lax.fori_loop(0, n_chunks, chunk_body, 0, unroll=2)
    y = _dot(mix_ref[...], wout_ref[...])
    o_ref[0] = x_ref[0] + mod_ref[0][2:3] * y


def _window_bias():
    case = jnp.arange(3, dtype=jnp.int32)[:, None, None]
    qi = jnp.arange(ATTN_BLOCK, dtype=jnp.int32)[None, :, None]
    kj = jnp.arange(3 * ATTN_BLOCK, dtype=jnp.int32)[None, None, :]
    return jnp.where(jnp.abs(kj - case * ATTN_BLOCK - qi) <= WINDOW, 0.0, NEG_INF).astype(F32)


def _mixer_even(x, mod, mod_per_batch, u, v, q, k2, v2, kc2, vc2, ws, bs_t, sink, wout, tq, is_ctx):
    B, S, D = x.shape
    Sk = k2.shape[1]
    Lc = kc2.shape[1]
    nk = Lc + (0 if is_ctx else 3 * ATTN_BLOCK)
    mod_map = (lambda b, i: (b, 0, 0)) if mod_per_batch else (lambda b, i: (0, 0, 0))
    tok = lambda width: pl.BlockSpec((1, tq, width), lambda b, i: (b, i, 0))
    per_batch = lambda rows: pl.BlockSpec((1, rows, 2 * KV_WIDTH), lambda b, i: (b, 0, 0))
    return pl.pallas_call(
        functools.partial(_mixer_even_kernel, seq_len=S, is_ctx=is_ctx),
        grid=(B, S // tq),
        in_specs=[pl.BlockSpec(memory_space=pltpu.SMEM),
                  tok(D),
                  pl.BlockSpec((1, 6, D), mod_map),
                  tok(SGU_WIDTH), tok(SGU_WIDTH), tok(ATTN_WIDTH),
                  per_batch(Sk), per_batch(Sk), per_batch(Lc), per_batch(Lc),
                  pl.BlockSpec((SGU_GROUPS, CHUNK, CHUNK), lambda b, i: (0, 0, 0)),
                  pl.BlockSpec((CHUNK, SGU_GROUPS), lambda b, i: (0, 0)),
                  pl.BlockSpec((D, D), lambda b, i: (0, 0)),
                  pl.BlockSpec((3, ATTN_BLOCK, 3 * ATTN_BLOCK), lambda b, i: (0, 0, 0))],
        out_specs=tok(D),
        out_shape=jax.ShapeDtypeStruct((B, S, D), F32),
        scratch_shapes=[pltpu.VMEM((tq, D), BF16),
                        pltpu.VMEM((2 * ATTN_BLOCK, 2 * nk), F32),
                        pltpu.VMEM((2 * ATTN_BLOCK, 2 * nk), BF16),
                        pltpu.VMEM((2 * ATTN_BLOCK, 2 * LANES), F32)],
        compiler_params=_cparams(("parallel", "arbitrary")),
        name="mixer_ctx" if is_ctx else "mixer_even",
    )(sink, x, mod, u, v, q, k2, v2, kc2, vc2, ws, bs_t, wout, _window_bias())


def _route_kernel(x_ref, mod_ref, nw_ref, rw_ref, meta_ref, cnt_ref, tb_ref, xs_ref,
                  base_ref, stage_ref, zero_ref, sc_vmem, sc_smem, sem, zsem, ssem, *, cap, tg):
    step = pl.program_id(0) * pl.num_programs(1) + pl.program_id(1)
    n_steps = pl.num_programs(0) * pl.num_programs(1)
    slot = step % 2

    def run_copies(slot_, action):
        for e in range(N_EXPERTS):
            first = e * cap + sc_smem[slot_, 0, e]
            count = sc_smem[slot_, 1, e]
            off = sc_smem[slot_, 2, e]
            n_big = count >> (DISPATCH_CHUNK.bit_length() - 1)
            n_small = (count - n_big * DISPATCH_CHUNK) >> (SUBLANES.bit_length() - 1)

            def copy(row, rows, first=first, off=off):
                return pltpu.make_async_copy(
                    stage_ref.at[slot_, pl.ds(pl.multiple_of(off + row, SUBLANES), rows)],
                    xs_ref.at[pl.ds(pl.multiple_of(first + row, SUBLANES), rows)], sem.at[slot_])

            def big(c, carry, copy=copy):
                action(copy(c * DISPATCH_CHUNK, DISPATCH_CHUNK))
                return carry

            def small(c, carry, copy=copy, n_big=n_big):
                action(copy(n_big * DISPATCH_CHUNK + c * SUBLANES, SUBLANES))
                return carry

            lax.fori_loop(0, n_big, big, 0)
            lax.fori_loop(0, n_small, small, 0)

    @pl.when(step == 0)
    def _():
        base_ref[...] = jnp.zeros_like(base_ref)

    @pl.when(step >= 2)
    def _():
        run_copies(slot, lambda cp: cp.wait())

    m = mod_ref[0]
    h = _rms_mod(x_ref[0], nw_ref[...], m[4:5], m[3:4])
    w = rw_ref[...]
    w_hi = w.astype(BF16)
    w_lo = (w - w_hi.astype(F32)).astype(BF16)
    h_hi = h.astype(BF16)
    h_lo = (h - h_hi.astype(F32)).astype(BF16)
    logits = _dot(h_hi, w_hi) + (_dot(h_lo, w_hi) + _dot(h_hi, w_lo))
    tm = logits.shape[0]
    lane = lax.broadcasted_iota(jnp.int32, logits.shape, 1)
    lg = jnp.where(lane < N_EXPERTS, logits, -jnp.inf)
    m1 = jnp.max(lg, axis=-1, keepdims=True)
    i1 = jnp.min(jnp.where(lg == m1, lane, LANES), axis=-1, keepdims=True)
    lg2 = jnp.where(lane == i1, -jnp.inf, lg)
    m2 = jnp.max(lg2, axis=-1, keepdims=True)
    i2 = jnp.min(jnp.where(lg2 == m2, lane, LANES), axis=-1, keepdims=True)
    e2 = jnp.exp(m2 - m1)
    den = 1.0 + e2
    hot = jnp.where(jnp.logical_or(lane == i1, lane == i2), 1.0, 0.0)
    r = lax.broadcasted_iota(jnp.int32, (tm, tm), 0)
    c = lax.broadcasted_iota(jnp.int32, (tm, tm), 1)
    before = jnp.where(r > c, 1.0, 0.0).astype(BF16)
    excl = _dot(before, hot.astype(BF16))
    base = base_ref[...]
    count = jnp.sum(hot, axis=0, keepdims=True)
    padded = jnp.floor((count + (SUBLANES - 1)) * (1.0 / SUBLANES)) * SUBLANES
    er = lax.broadcasted_iota(jnp.int32, (LANES, LANES), 0)
    ec = lax.broadcasted_iota(jnp.int32, (LANES, LANES), 1)
    lower = jnp.where(er < ec, 1.0, 0.0).astype(BF16)
    groups = jnp.broadcast_to(padded * (1.0 / SUBLANES), (SUBLANES, LANES)).astype(BF16)
    stage_off = _dot(groups, lower)[0:1] * SUBLANES
    pick = lambda idx, table: jnp.sum(jnp.where(lane == idx, table, 0.0), axis=-1, keepdims=True)
    tot = base + excl
    pos1 = i1.astype(F32) * cap + pick(i1, tot)
    pos2 = i2.astype(F32) * cap + pick(i2, tot)
    loc1 = pick(i1, stage_off + excl)
    loc2 = pick(i2, stage_off + excl)
    meta = (jnp.where(lane == 0, pos1, 0.0) + jnp.where(lane == 1, pos2, 0.0)
            + jnp.where(lane == 2, 1.0 / den, 0.0) + jnp.where(lane == 3, e2 / den, 0.0)
            + jnp.where(lane == 4, i1.astype(F32), 0.0) + jnp.where(lane == 5, i2.astype(F32), 0.0)
            + jnp.where(lane == 6, loc1, 0.0) + jnp.where(lane == 7, loc2, 0.0))
    meta_ref[0] = meta
    sub = lax.broadcasted_iota(jnp.int32, (SUBLANES, LANES), 0)
    tb = jnp.zeros((SUBLANES, LANES), F32)
    for j in range(tm // COMBINE_TILE):
        tb = jnp.where(sub == j, tot[j * COMBINE_TILE:j * COMBINE_TILE + 1, :], tb)
    tb_ref[0, 0] = tb
    base_ref[...] = base + padded
    cnt_ref[...] = jnp.broadcast_to(base + padded, cnt_ref.shape)

    meta_t = meta.T
    j = lax.broadcasted_iota(jnp.int32, (stage_ref.shape[1], tm), 0)
    sel = jnp.logical_or(j == meta_t[6:7].astype(jnp.int32), j == meta_t[7:8].astype(jnp.int32))
    stage_ref[slot] = _dot(jnp.where(sel, 1.0, 0.0).astype(BF16), h_hi)
    record = jnp.where(sub == 0, base, jnp.where(sub == 1, padded, jnp.where(sub == 2, stage_off, 0.0)))
    sc_vmem[...] = record.astype(jnp.int32)
    to_smem = pltpu.make_async_copy(sc_vmem, sc_smem.at[slot], ssem)
    to_smem.start()
    to_smem.wait()
    run_copies(slot, lambda cp: cp.start())

    @pl.when(step == n_steps - 1)
    def _():
        zero_ref[...] = jnp.zeros_like(zero_ref)

        def tail_copy(e):
            end = sc_smem[slot, 0, e] + sc_smem[slot, 1, e]
            return pltpu.make_async_copy(zero_ref, xs_ref.at[pl.ds(pl.multiple_of(e * cap + end, SUBLANES), tg)], zsem)

        for e in range(N_EXPERTS):
            tail_copy(e).start()
        for e in range(N_EXPERTS):
            tail_copy(e).wait()
        run_copies(slot, lambda cp: cp.wait())

        @pl.when(n_steps >= 2)
        def _():
            run_copies(1 - slot, lambda cp: cp.wait())


def _route(x, mod, nw, rw, tm, cap, tg):
    B, S, D = x.shape
    n_stage = 2 * tm + N_EXPERTS * SUBLANES
    return pl.pallas_call(
        functools.partial(_route_kernel, cap=cap, tg=tg),
        grid=(B, S // tm),
        in_specs=[pl.BlockSpec((1, tm, D), lambda b, i: (b, i, 0)),
                  pl.BlockSpec((1, 6, D), lambda b, i: (b, 0, 0)),
                  pl.BlockSpec((1, D), lambda b, i: (0, 0)),
                  pl.BlockSpec((D, LANES), lambda b, i: (0, 0))],
        out_specs=[pl.BlockSpec((1, tm, LANES), lambda b, i: (b, i, 0)),
                   pl.BlockSpec((SUBLANES, LANES), lambda b, i: (0, 0)),
                   pl.BlockSpec((1, 1, SUBLANES, LANES), lambda b, i: (b, i, 0, 0)),
                   pl.BlockSpec(memory_space=pl.ANY)],
        out_shape=[jax.ShapeDtypeStruct((B, S, LANES), F32),
                   jax.ShapeDtypeStruct((SUBLANES, LANES), F32),
                   jax.ShapeDtypeStruct((B, S // tm, SUBLANES, LANES), F32),
                   jax.ShapeDtypeStruct((N_EXPERTS * cap, D), F32)],
        scratch_shapes=[pltpu.VMEM((1, LANES), F32),
                        pltpu.VMEM((2, n_stage, D), F32),
                        pltpu.VMEM((tg, D), F32),
                        pltpu.VMEM((SUBLANES, LANES), jnp.int32),
                        pltpu.SMEM((2, SUBLANES, LANES), jnp.int32),
                        pltpu.SemaphoreType.DMA((2,)),
                        pltpu.SemaphoreType.DMA,
                        pltpu.SemaphoreType.DMA],
        compiler_params=_cparams(("arbitrary", "arbitrary")),
        name="moe_route",
    )(x, mod, nw, rw)


def _moe_group_kernel(blk_ref, exp_ref, fa_ref, fb_ref, nact_ref, x_ref, w1_ref, w3_ref, w2_ref, o_ref, acc_ref):
    t = pl.program_id(0)
    f = pl.program_id(1)

    @pl.when(jnp.logical_and(t == 0, f == 0))
    def _():
        acc_ref[...] = jnp.zeros_like(acc_ref)

    @pl.when(t < nact_ref[0])
    def _():
        h = x_ref[...].astype(BF16)
        a = _dot(h, w1_ref[0])
        b = _dot(h, w3_ref[0])
        y = _dot(((a * _sigmoid(a)) * b).astype(BF16), w2_ref[0])
        total = jnp.where(f == 0, 0.0, acc_ref[...]) + y
        acc_ref[...] = total
        o_ref[...] = total.astype(o_ref.dtype)


def _moe_group(blk, exp, fa, fb, nact, xs, w1, w3, w2, tg):
    R, D = xs.shape
    F = w1.shape[2]
    tf = F // 2
    fsel = lambda f, fa, fb, t: jnp.where(f == 0, fa[t], fb[t])
    return pl.pallas_call(
        _moe_group_kernel,
        grid_spec=pltpu.PrefetchScalarGridSpec(
            num_scalar_prefetch=5,
            grid=(blk.shape[0], 2),
            in_specs=[pl.BlockSpec((tg, D), lambda t, f, blk, exp, fa, fb, na: (blk[t], 0)),
                      pl.BlockSpec((1, D, tf), lambda t, f, blk, exp, fa, fb, na: (exp[t], 0, fsel(f, fa, fb, t))),
                      pl.BlockSpec((1, D, tf), lambda t, f, blk, exp, fa, fb, na: (exp[t], 0, fsel(f, fa, fb, t))),
                      pl.BlockSpec((1, tf, D), lambda t, f, blk, exp, fa, fb, na: (exp[t], fsel(f, fa, fb, t), 0))],
            out_specs=pl.BlockSpec((tg, D), lambda t, f, blk, exp, fa, fb, na: (blk[t], 0)),
            scratch_shapes=[pltpu.VMEM((tg, D), F32)]),
        out_shape=jax.ShapeDtypeStruct((R, D), BF16),
        compiler_params=_cparams(("arbitrary", "arbitrary")),
        name="moe_experts",
    )(blk, exp, fa, fb, nact, xs, w1, w3, w2)


def _combine_kernel(tb_ref, x_ref, mod_ref, meta_ref, fn_ref, y_ref, o_ref, buf_ref, sem, *, cap):
    i = pl.program_id(0)
    n = pl.num_programs(0)
    tc = x_ref.shape[0]
    n_rows = buf_ref.shape[1]
    shift = COMBINE_CHUNK.bit_length() - 1

    def segments(tile):
        segs, off = [], 0
        for e in range(N_EXPERTS):
            first = tb_ref[tile * N_EXPERTS + e]
            count = tb_ref[(tile + 1) * N_EXPERTS + e] - first
            lead = first & (COMBINE_CHUNK - 1)
            n_chunks = jnp.where(count > 0, lax.shift_right_logical(lead + count + COMBINE_CHUNK - 1, shift), 0)
            segs.append((e * cap + first - lead, n_chunks, off))
            off = off + n_chunks * COMBINE_CHUNK
        return segs

    def chunk_copy(src_row, dst_row, slot):
        return pltpu.make_async_copy(y_ref.at[pl.ds(pl.multiple_of(src_row, COMBINE_CHUNK), COMBINE_CHUNK)],
                                     buf_ref.at[slot, pl.ds(pl.multiple_of(dst_row, COMBINE_CHUNK), COMBINE_CHUNK)],
                                     sem.at[slot])

    def for_each_chunk(tile, slot, action):
        for src, n_chunks, off in segments(tile):
            def body(c, carry):
                action(chunk_copy(src + c * COMBINE_CHUNK, off + c * COMBINE_CHUNK, slot))
                return carry

            lax.fori_loop(0, n_chunks, body, 0)

    @pl.when(i == 0)
    def _():
        buf_ref[...] = jnp.zeros_like(buf_ref)
        for_each_chunk(0, 0, lambda cp: cp.start())

    slot = i % 2

    @pl.when(i + 1 < n)
    def _():
        for_each_chunk(i + 1, 1 - slot, lambda cp: cp.start())

    for_each_chunk(i, slot, lambda cp: cp.wait())
    meta = meta_ref[...]
    segs = segments(i)

    def one_hot(pos, exp):
        delta = jnp.zeros_like(pos)
        for e, (src, _, off) in enumerate(segs):
            delta = jnp.where(exp == e, off - src, delta)
        col = lax.broadcasted_iota(jnp.int32, (tc, n_rows), 1)
        return jnp.where(col == pos + delta, 1.0, 0.0).astype(BF16)

    as_int = lambda lane: meta[:, lane:lane + 1].astype(jnp.int32)
    picks = jnp.concatenate([one_hot(as_int(0), as_int(4)), one_hot(as_int(1), as_int(5))], axis=0)
    y12 = _dot(picks, buf_ref[slot])
    mix = meta[:, 2:3] * y12[0:tc] + meta[:, 3:4] * y12[tc:2 * tc]
    o_ref[...] = _rms(x_ref[...] + mod_ref[0][5:6] * mix, fn_ref[...])


def _combine(tile_base, x, mod, meta, fin, y, seq_len, cap):
    T, D = x.shape
    tc = COMBINE_TILE
    per_batch = seq_len // tc
    n_rows = -(-(2 * tc + N_EXPERTS * (2 * (COMBINE_CHUNK - 1) + SUBLANES - 1)) // LANES) * LANES
    return pl.pallas_call(
        functools.partial(_combine_kernel, cap=cap),
        grid_spec=pltpu.PrefetchScalarGridSpec(
            num_scalar_prefetch=1,
            grid=(T // tc,),
            in_specs=[pl.BlockSpec((tc, D), lambda i, tb: (i, 0)),
                      pl.BlockSpec((1, 6, D), lambda i, tb: (i // per_batch, 0, 0)),
                      pl.BlockSpec((tc, LANES), lambda i, tb: (i, 0)),
                      pl.BlockSpec((1, D), lambda i, tb: (0, 0)),
                      pl.BlockSpec(memory_space=pl.ANY)],
            out_specs=pl.BlockSpec((tc, D), lambda i, tb: (i, 0)),
            scratch_shapes=[pltpu.VMEM((2, n_rows, D), BF16), pltpu.SemaphoreType.DMA((2,))]),
        out_shape=jax.ShapeDtypeStruct((T, D), F32),
        compiler_params=_cparams(("arbitrary",)),
        name="moe_combine",
    )(tile_base, x, mod, meta, fin, y)


def _moe_tiles(cnt, n_tiles, tg, cap):
    per = (cnt + tg - 1) // tg
    cum = jnp.cumsum(per)
    nact = cum[-1]
    t = jnp.arange(n_tiles, dtype=jnp.int32)
    tt = jnp.minimum(t, nact - 1)
    exp = jnp.minimum(jnp.sum((tt[:, None] >= cum[None, :]).astype(jnp.int32), axis=1), N_EXPERTS - 1)
    blk = exp * (cap // tg) + tt - (cum - per)[exp]
    odd = tt % 2
    fa = jnp.where(t < nact, odd, 1 - odd)
    fb = 1 - odd
    i32 = lambda v: v.astype(jnp.int32)
    return i32(blk), i32(exp), i32(fa), i32(fb), i32(nact.reshape(1))


def _ffn_kernel(*refs, n_cast):
    x_ref, mod_ref, nw_ref, w1_ref, w3_ref, w2_ref = refs[0:6]
    cast_in = refs[6:6 + n_cast]
    o_ref = refs[6 + n_cast]
    cast_out = refs[7 + n_cast:7 + 2 * n_cast]
    (acc_ref,) = refs[7 + 2 * n_cast:]
    f = pl.program_id(2)

    @pl.when(jnp.logical_and(jnp.logical_and(pl.program_id(0) == 0, pl.program_id(1) == 0), f == 0))
    def _():
        acc_ref[...] = jnp.zeros_like(acc_ref)

    for src, dst in zip(cast_in, cast_out):
        dst[...] = src[...].astype(dst.dtype)
    m = mod_ref[0]
    x = x_ref[0]
    h = _rms_mod(x, nw_ref[...], m[4:5], m[3:4]).astype(BF16)
    a = _dot(h, w1_ref[...])
    b = _dot(h, w3_ref[...])
    total = jnp.where(f == 0, 0.0, acc_ref[...]) + _dot(((a * _sigmoid(a)) * b).astype(BF16), w2_ref[...])
    acc_ref[...] = total
    o_ref[0] = x + m[5:6] * total


def _ffn(x, mod, mod_per_batch, nw, w1, w3, w2, tm, tf, cast=()):
    B, S, D = x.shape
    F = w1.shape[1]
    n_i, n_f = S // tm, F // tf
    mod_map = (lambda b, i, f: (b, 0, 0)) if mod_per_batch else (lambda b, i, f: (0, 0, 0))
    cast_specs, cast_shapes = _cast_specs(cast, (B, n_i, n_f))
    tok = pl.BlockSpec((1, tm, D), lambda b, i, f: (b, i, 0))
    outs = pl.pallas_call(
        functools.partial(_ffn_kernel, n_cast=len(cast)),
        grid=(B, n_i, n_f),
        in_specs=[tok,
                  pl.BlockSpec((1, 6, D), mod_map),
                  pl.BlockSpec((1, D), lambda b, i, f: (0, 0)),
                  pl.BlockSpec((D, tf), lambda b, i, f: (0, f)),
                  pl.BlockSpec((D, tf), lambda b, i, f: (0, f)),
                  pl.BlockSpec((tf, D), lambda b, i, f: (f, 0))] + cast_specs,
        out_specs=[tok] + cast_specs,
        out_shape=[jax.ShapeDtypeStruct((B, S, D), F32)] + cast_shapes,
        scratch_shapes=[pltpu.VMEM((tm, D), F32)],
        compiler_params=_cparams(("parallel", "parallel", "arbitrary")),
        name="ffn",
    )(x, mod, nw, w1, w3, w2, *cast)
    return outs[0], tuple(outs[1:])


def _proj_odd_kernel(x_ref, mod_ref, nw_ref, w_ref, gate_ref, rec_ref):
    m = mod_ref[0]
    h = _rms_mod(x_ref[0], nw_ref[...], m[1:2], m[0:1]).astype(BF16)
    gate_ref[0] = _gelu(_dot(h, w_ref[:, 0:D_RNN])).astype(gate_ref.dtype)
    rec_ref[0] = _dot(h, w_ref[:, D_RNN:2 * D_RNN])


def _proj_odd(x, mod, mod_per_batch, nw, w, tm):
    B, S, D = x.shape
    mod_map = (lambda b, i: (b, 0, 0)) if mod_per_batch else (lambda b, i: (0, 0, 0))
    tok = lambda width: pl.BlockSpec((1, tm, width), lambda b, i: (b, i, 0))
    return pl.pallas_call(
        _proj_odd_kernel,
        grid=(B, S // tm),
        in_specs=[tok(D),
                  pl.BlockSpec((1, 6, D), mod_map),
                  pl.BlockSpec((1, D), lambda b, i: (0, 0)),
                  pl.BlockSpec((D, 2 * D_RNN), lambda b, i: (0, 0))],
        out_specs=[tok(D_RNN), tok(D_RNN)],
        out_shape=[jax.ShapeDtypeStruct((B, S, D_RNN), BF16), jax.ShapeDtypeStruct((B, S, D_RNN), F32)],
        compiler_params=_cparams(("parallel", "parallel")),
        name="proj_odd",
    )(x, mod, nw, w)


def _scan8(a_ref, b_ref, d, r0, h, row, reverse):
    base = pl.multiple_of(SUBLANES + r0, SUBLANES)
    a = a_ref[d, pl.ds(base, SUBLANES), :]
    b = b_ref[d, pl.ds(base, SUBLANES), :]
    for s in (1, 2, 4):
        live = row < SUBLANES - s if reverse else row >= s
        if s == 1:
            shifted = base + 1 if reverse else base - 1
            a_s = a_ref[d, pl.ds(shifted, SUBLANES), :]
            b_s = b_ref[d, pl.ds(shifted, SUBLANES), :]
        else:
            shift = SUBLANES - s if reverse else s
            a_s, b_s = pltpu.roll(a, shift, 0), pltpu.roll(b, shift, 0)
        b = jnp.where(live, a * b_s + b, b)
        a = jnp.where(live, a * a_s, a)
    hr = a * h + b
    return hr, (hr[0:1] if reverse else hr[SUBLANES - 1:SUBLANES])


def _lru_kernel(rec_ref, recc_ref, gate_ref, cw_ref, cb_ref, wa_ref, ba_ref, wx_ref, bx_ref, lam_ref,
                y_ref, pad_ref, a_ref, b_ref, cpad_ref, ca_ref, cbb_ref, park_ref, s_ref, *, tile):
    S = rec_ref.shape[1]
    L = recc_ref.shape[1]
    cw = cw_ref[...]
    cb = cb_ref[...]
    lam = lam_ref[...]
    sp = jnp.maximum(-lam, 0.0) + jnp.log1p(jnp.exp(-jnp.abs(lam)))
    z_half = (-0.5 * LRU_C) * sp
    z_half_log2e = z_half * LOG2E
    neg_z_quarter = -0.5 * z_half
    half_ba = 0.5 * ba_ref[...]
    half_bx = 0.5 * bx_ref[...]
    zeros8 = jnp.zeros((SUBLANES, LANES), F32)

    def coefficients(src_ref, dst_a, dst_b, n_rows, t):
        pad = cpad_ref if src_ref is recc_ref else pad_ref
        pad[0:SUBLANES, :] = zeros8
        pad[SUBLANES + n_rows:2 * SUBLANES + n_rows, :] = zeros8
        for d in range(2):
            for dst in (dst_a, dst_b):
                dst[d, 0:SUBLANES, :] = zeros8
                dst[d, SUBLANES + n_rows:2 * SUBLANES + n_rows, :] = zeros8

        def copy(j, carry):
            r0 = pl.multiple_of(j * t, t)
            pad[pl.ds(SUBLANES + r0, t), :] = src_ref[0, pl.ds(r0, t), :]
            return carry

        lax.fori_loop(0, n_rows // t, copy, 0)

        def body(j, carry):
            r0 = pl.multiple_of(j * t, t)
            conv = cb
            for tap in range(4):
                conv = conv + cw[tap:tap + 1] * pad[pl.ds(r0 + (SUBLANES - 2 + tap), t), :]
            cbf = conv.astype(BF16)
            half_conv = 0.5 * conv
            for d in range(2):
                tr = jnp.tanh(_dot(cbf, wa_ref[d, 0]) + half_ba[d:d + 1])
                ti = jnp.tanh(_dot(cbf, wx_ref[d, 0]) + half_bx[d:d + 1])
                a = jnp.exp2(z_half_log2e[d:d + 1] * tr + z_half_log2e[d:d + 1])
                y = jnp.tanh(neg_z_quarter[d:d + 1] * tr + neg_z_quarter[d:d + 1])
                root = jnp.where(y > 0.0, y * lax.rsqrt(y), 0.0)
                dst_a[d, pl.ds(SUBLANES + r0, t), :] = a
                dst_b[d, pl.ds(SUBLANES + r0, t), :] = (root * (1.0 + a)) * ((ti + 1.0) * half_conv)
            return carry

        lax.fori_loop(0, n_rows // t, body, 0)

    row = lax.broadcasted_iota(jnp.int32, (SUBLANES, LANES), 0)
    h_zero = jnp.zeros((1, LANES), F32)

    coefficients(recc_ref, ca_ref, cbb_ref, L, L)
    nc = L // SUBLANES

    def ctx_body(j, carry):
        hf, hb = carry
        rf = pl.multiple_of(j * SUBLANES, SUBLANES)
        rb = pl.multiple_of((nc - 1 - j) * SUBLANES, SUBLANES)
        _, hf = _scan8(ca_ref, cbb_ref, 0, rf, hf, row, False)
        _, hb = _scan8(ca_ref, cbb_ref, 1, rb, hb, row, True)
        return hf, hb

    h0f, h0b = lax.fori_loop(0, nc, ctx_body, (h_zero, h_zero))

    coefficients(rec_ref, a_ref, b_ref, S, tile)
    n = S // SUBLANES

    def lat_body(second_half):
        def body(j, carry):
            hf, hb = carry
            rf = pl.multiple_of(j * SUBLANES, SUBLANES)
            rb = pl.multiple_of((n - 1 - j) * SUBLANES, SUBLANES)
            of, hf = _scan8(a_ref, b_ref, 0, rf, hf, row, False)
            ob, hb = _scan8(a_ref, b_ref, 1, rb, hb, row, True)
            if second_half:
                park_ref[0, pl.ds(rf - half, SUBLANES), :] = of
                park_ref[1, pl.ds(rb, SUBLANES), :] = ob
            else:
                s_ref[pl.ds(rf, SUBLANES), :] = of
                s_ref[pl.ds(rb, SUBLANES), :] = ob
            return hf, hb
        return body

    half = S // 2
    mid = lax.fori_loop(0, n // 2, lat_body(False), (h0f, h0b), unroll=8)
    lax.fori_loop(n // 2, n, lat_body(True), mid, unroll=8)

    def finish(j, carry):
        for rows, parked in ((pl.ds(pl.multiple_of(j * tile, tile), tile), 1),
                             (pl.ds(pl.multiple_of(half + j * tile, tile), tile), 0)):
            state = s_ref[rows, :] + park_ref[parked, pl.ds(pl.multiple_of(j * tile, tile), tile), :]
            y_ref[0, rows, :] = (gate_ref[0, rows, :].astype(F32) * state).astype(y_ref.dtype)
        return carry

    lax.fori_loop(0, half // tile, finish, 0)


def _lru(rec, rec_c, gate, conv_w, conv_b, wa, ba, wx, bx, lam, tile):
    B, S, _ = rec.shape
    L = rec_c.shape[1]
    blk = lambda rows: pl.BlockSpec((1, rows, LRU_BLOCK), lambda b, j: (b, 0, j))
    vec = lambda rows: pl.BlockSpec((rows, LRU_BLOCK), lambda b, j: (0, j))
    wspec = pl.BlockSpec((2, 1, LRU_BLOCK, LRU_BLOCK), lambda b, j: (0, j, 0, 0))
    return pl.pallas_call(
        functools.partial(_lru_kernel, tile=tile),
        grid=(B, LRU_BLOCKS),
        in_specs=[blk(S), blk(L), blk(S), vec(4), vec(1), wspec, vec(2), wspec, vec(2), vec(2)],
        out_specs=blk(S),
        out_shape=jax.ShapeDtypeStruct((B, S, D_RNN), BF16),
        scratch_shapes=[pltpu.VMEM((S + 2 * SUBLANES, LRU_BLOCK), F32),
                        pltpu.VMEM((2, S + 2 * SUBLANES, LRU_BLOCK), F32),
                        pltpu.VMEM((2, S + 2 * SUBLANES, LRU_BLOCK), F32),
                        pltpu.VMEM((L + 2 * SUBLANES, LRU_BLOCK), F32),
                        pltpu.VMEM((2, L + 2 * SUBLANES, LRU_BLOCK), F32),
                        pltpu.VMEM((2, L + 2 * SUBLANES, LRU_BLOCK), F32),
                        pltpu.VMEM((2, S // 2, LRU_BLOCK), F32),
                        pltpu.VMEM((S, LRU_BLOCK), F32)],
        compiler_params=_cparams(("parallel", "parallel")),
        name="lru_scan",
    )(rec, rec_c, gate, conv_w, conv_b, wa, ba, wx, bx, lam)


def _lru_out_kernel(x_ref, mod_ref, y_ref, w_ref, o_ref):
    o_ref[0] = x_ref[0] + mod_ref[0][2:3] * _dot(y_ref[0], w_ref[...])


def _lru_out(x, mod, y, w, tm):
    B, S, D = x.shape
    tok = lambda width: pl.BlockSpec((1, tm, width), lambda b, i: (b, i, 0))
    return pl.pallas_call(
        _lru_out_kernel,
        grid=(B, S // tm),
        in_specs=[tok(D), pl.BlockSpec((1, 6, D), lambda b, i: (b, 0, 0)), tok(D_RNN),
                  pl.BlockSpec((D_RNN, D), lambda b, i: (0, 0))],
        out_specs=tok(D),
        out_shape=jax.ShapeDtypeStruct((B, S, D), F32),
        compiler_params=_cparams(("parallel", "parallel")),
        name="lru_out",
    )(x, mod, y, w)


def _rope_tables(n_tok):
    rows = n_tok // GRID_W
    row = jnp.repeat(jnp.arange(rows, dtype=F32), GRID_W)
    col = jnp.tile(jnp.arange(GRID_W, dtype=F32), rows)
    freqs = ROPE_BASE ** (-jnp.arange(ROPE_FREQS, dtype=F32) / ROPE_FREQS)
    ar, ac = row[:, None] * freqs, col[:, None] * freqs
    cos = jnp.concatenate([jnp.cos(ar), jnp.cos(ar), jnp.cos(ac), jnp.cos(ac)], axis=-1)
    sin = jnp.concatenate([-jnp.sin(ar), jnp.sin(ar), -jnp.sin(ac), jnp.sin(ac)], axis=-1)
    return jnp.tile(cos, (1, LANES // HEAD_DIM)), jnp.tile(sin, (1, LANES // HEAD_DIM))


def kernel(x, c, ctx, c_ctx, ada_w_e, ada_b_e, norm1_e, norm2_e, w_in_e, sgu_w, sgu_b, attn_sink, w_out_e, ffn_w1, ffn_w3, ffn_w2, ada_w_o, ada_b_o, norm1_o, norm2_o, w_in_o, conv_w, conv_b, lru_wa, lru_ba, lru_wx, lru_bx, lru_lambda, w_out_o, router_w, moe_w1, moe_w3, moe_w2, final_norm):
    B, S, D = x.shape
    L = ctx.shape[1]
    cvec = jnp.concatenate([c, c_ctx[None], jnp.zeros((SUBLANES - B - 1, D), F32)], axis=0)
    mod_e = _ada_params(cvec, ada_w_e[0], ada_b_e[0])
    mod_o = _ada_params(cvec, ada_w_o[0], ada_b_o[0])
    lat_e, ctx_e = mod_e[0:B], mod_e[B:B + 1]
    lat_o, ctx_o = mod_o[0:B], mod_o[B:B + 1]
    bf = lambda t: t.astype(BF16)
    row = lambda t: t.reshape(1, -1)

    cos, sin = _rope_tables(S)
    cos_c, sin_c = jnp.ones((L, LANES), F32), jnp.zeros((L, LANES), F32)
    w_in = bf(w_in_e[0])
    n1, n2 = row(norm1_e[0]), row(norm2_e[0])
    (uc, vc, qc, kc2, vc2), _ = _proj_even(ctx, ctx_e, False, n1, w_in, cos_c, sin_c, L)
    (u, v, q, k2, v2), later_w = _proj_even(x, lat_e, True, n1, w_in, cos, sin, 1024,
                                            cast=(ffn_w1, ffn_w3, ffn_w2, w_out_e, w_in_o, w_out_o))
    w1, w3, w2, wout, w_in_odd, w_out_odd = [t[0] for t in later_w]
    ws, bs_t = bf(sgu_w[0]), sgu_b[0].T
    sink = attn_sink[0] * LOG2E
    x = _mixer_even(x, lat_e, True, u, v, q, k2, v2, kc2, vc2, ws, bs_t, sink, wout, 1024, False)
    xc = _mixer_even(ctx, ctx_e, False, uc, vc, qc, kc2, vc2, kc2, vc2, ws, bs_t, sink, wout, L, True)
    x, moe_w = _ffn(x, lat_e, True, n2, w1, w3, w2, 512, 1408, cast=(moe_w1[0], moe_w3[0], moe_w2[0]))
    xc, _ = _ffn(xc, ctx_e, False, n2, w1, w3, w2, L, 1408)

    w_in = w_in_odd
    n1, n2 = row(norm1_o[0]), row(norm2_o[0])
    _, rec_c = _proj_odd(xc, ctx_o, False, n1, w_in, L)
    gate, rec = _proj_odd(x, lat_o, True, n1, w_in, 1024)
    y = _lru(rec, rec_c, gate, conv_w[0], row(conv_b[0]), bf(0.5 * lru_wa[0]), lru_ba[0], bf(0.5 * lru_wx[0]),
             lru_bx[0], lru_lambda[0], 512)
    x = _lru_out(x, lat_o, y, w_out_odd, 1024)
    rw = jnp.pad(router_w[0], ((0, 0), (0, LANES - N_EXPERTS)))
    T = B * S
    tg = 512
    tm = 512
    run_pad = (T // tm) * (SUBLANES - 1)
    cap = -(-(T + run_pad) // tg) * tg + 2 * tg
    meta, cnt, tb4, xs = _route(x, lat_o, n2, rw, tm, cap, tg)
    cnt = cnt[0, 0:N_EXPERTS].astype(jnp.int32)
    tile_base = jnp.concatenate([tb4[:, :, 0:tm // COMBINE_TILE, 0:N_EXPERTS].astype(jnp.int32).reshape(-1), cnt])
    n_tiles = -(-(2 * T + N_EXPERTS * run_pad) // tg) + N_EXPERTS
    blk, exp, fa, fb, nact = _moe_tiles(cnt, n_tiles, tg, cap)
    y = _moe_group(blk, exp, fa, fb, nact, xs, *moe_w, tg)
    out = _combine(tile_base, x.reshape(T, D), lat_o, meta.reshape(T, LANES), row(final_norm), y, S, cap)
    return out.reshape(B, S, D)
```
